```python
import jax, jax.numpy as jnp
from jax import lax
import numpy as np

D_MODEL = 1024
BATCH = 8
SEQ = 4096
DEPTH = 4

N_META = 16
GRID_W = 64
GLA_HEADS = 4
GLA_DK = 64
GLA_DV = 128
GLA_KEY = GLA_HEADS * GLA_DK
GLA_VAL = GLA_HEADS * GLA_DV
GLA_RANK = 16
GLA_TAU = 16.0
CHUNK = 64
ATT_HEADS = 8
ATT_KV_HEADS = 2
HEAD_DIM = 64
ATT_Q = ATT_HEADS * HEAD_DIM
ATT_KV = ATT_KV_HEADS * HEAD_DIM
Q_BLOCK = 128
ROPE_THETA = 10000.0
D_FF = 2816
EPS = 1e-6

IN_SIZES = (GLA_KEY, GLA_KEY, GLA_VAL, GLA_VAL, GLA_RANK, GLA_RANK,
            ATT_Q, ATT_KV, ATT_KV, D_MODEL, D_MODEL)
D_IN = sum(IN_SIZES)
SPLIT_POINTS = tuple(int(s) for s in np.cumsum(IN_SIZES)[:-1])

kernel_name = "hybrid_gla_gqa_macaron_encoder"


def rmsnorm(x, g):
    xf = x.astype(jnp.float32)
    y = xf * lax.rsqrt(jnp.mean(xf * xf, axis=-1, keepdims=True) + EPS)
    return (y * g.astype(jnp.float32)).astype(x.dtype)


def swiglu(x, w_gate, w_up, w_down):
    return (jax.nn.silu(x @ w_gate) * (x @ w_up)) @ w_down


def gla_causal_chunked(q, k, v, log_a):
    f32 = jnp.float32
    B, H, T, dk = q.shape
    dv = v.shape[-1]
    n = T // CHUNK
    qc = q.astype(f32).reshape(B, H, n, CHUNK, dk)
    kc = k.astype(f32).reshape(B, H, n, CHUNK, dk)
    vc = v.astype(f32).reshape(B, H, n, CHUNK, dv)
    bcum = jnp.cumsum(log_a.astype(f32).reshape(B, H, n, CHUNK, dk), axis=3)
    btot = bcum[:, :, :, -1:, :]
    q_dec = qc * jnp.exp(bcum)
    k_inv = kc * jnp.exp(-bcum)
    k_end = kc * jnp.exp(btot - bcum)
    mask = jnp.tril(jnp.ones((CHUNK, CHUNK), dtype=bool))
    att = jnp.where(mask, jnp.einsum('bhnid,bhnjd->bhnij', q_dec, k_inv), 0.0)
    o_intra = jnp.einsum('bhnij,bhnjv->bhniv', att, vc)
    kv_chunk = jnp.einsum('bhnjd,bhnjv->bhndv', k_end, vc)
    decay = jnp.exp(btot[:, :, :, 0, :])

    def step(S, inp):
        d, kv = inp
        return d[..., None] * S + kv, S

    S0 = jnp.zeros((B, H, dk, dv), f32)
    _, S_prev = lax.scan(step, S0, (jnp.moveaxis(decay, 2, 0), jnp.moveaxis(kv_chunk, 2, 0)))
    S_prev = jnp.moveaxis(S_prev, 0, 2)
    o_inter = jnp.einsum('bhnid,bhndv->bhniv', q_dec, S_prev)
    return (o_intra + o_inter).reshape(B, H, T, dv)


def gla_branch(q, k, v, r, lr_f, lr_b, w2, b2, gn_gain):
    B, L, _ = q.shape
    pad = CHUNK - N_META

    def heads(t, d):
        return t.reshape(B, L, GLA_HEADS, d).transpose(0, 2, 1, 3)

    def log_gate(lr, w, b):
        return jax.nn.log_sigmoid((lr @ w + b).astype(jnp.float32)) / GLA_TAU

    def padseq(t):
        return jnp.pad(t, ((0, 0), (0, 0), (pad, 0), (0, 0)))

    def flip(t):
        return jnp.flip(t, axis=2)

    qh = padseq(heads(q * GLA_DK ** -0.5, GLA_DK))
    kh = padseq(heads(k, GLA_DK))
    vh = padseq(heads(v, GLA_DV))
    gf = padseq(heads(log_gate(lr_f, w2[0], b2[0]), GLA_DK))
    gb = padseq(heads(log_gate(lr_b, w2[1], b2[1]), GLA_DK))
    o_f = gla_causal_chunked(qh, kh, vh, gf)
    o_b = flip(gla_causal_chunked(flip(qh), flip(kh), flip(vh), flip(gb)))
    o = (o_f + o_b)[:, :, pad:, :].transpose(0, 2, 1, 3)
    o = o * lax.rsqrt(jnp.mean(o * o, axis=-1, keepdims=True) + EPS)
    o = o.reshape(B, L, GLA_VAL) * gn_gain.astype(jnp.float32)
    return o.astype(r.dtype) * jax.nn.silu(r)


def axial_rope_tables(n_tok):
    f32 = jnp.float32
    rows = n_tok // GRID_W
    row = jnp.repeat(jnp.arange(rows), GRID_W).astype(f32)
    col = jnp.tile(jnp.arange(GRID_W), rows).astype(f32)
    axis_dim = HEAD_DIM // 2
    inv = ROPE_THETA ** (-jnp.arange(0, axis_dim, 2, dtype=f32) / axis_dim)
    ang = jnp.concatenate([row[:, None] * inv, col[:, None] * inv], axis=-1)
    ang = jnp.concatenate([jnp.zeros((N_META, axis_dim), f32), ang], axis=0)
    return jnp.cos(ang), jnp.sin(ang)


def apply_axial_rope(x, cos, sin):
    L = x.shape[-2]
    q4 = HEAD_DIM // 4
    xs = x.astype(jnp.float32).reshape(*x.shape[:-1], 2, 2, q4)
    x1, x2 = xs[..., 0, :], xs[..., 1, :]
    c = cos.reshape(L, 2, q4)
    s = sin.reshape(L, 2, q4)
    return jnp.stack([x1 * c - x2 * s, x2 * c + x1 * s], axis=-2).reshape(x.shape)


def attention_branch(q, k, v, gq, gk, cos, sin):
    B, L, _ = q.shape
    G = ATT_HEADS // ATT_KV_HEADS
    qh = q.reshape(B, L, ATT_KV_HEADS, G, HEAD_DIM).transpose(0, 2, 3, 1, 4)
    kh = k.reshape(B, L, ATT_KV_HEADS, HEAD_DIM).transpose(0, 2, 1, 3)
    vh = v.reshape(B, L, ATT_KV_HEADS, HEAD_DIM).transpose(0, 2, 1, 3)
    qh = apply_axial_rope(rmsnorm(qh, gq), cos, sin) * HEAD_DIM ** -0.5
    kh = apply_axial_rope(rmsnorm(kh, gk), cos, sin)
    pad = Q_BLOCK - N_META
    qp = jnp.pad(qh, ((0, 0), (0, 0), (0, 0), (pad, 0), (0, 0)))
    nblk = qp.shape[3] // Q_BLOCK
    qb = jnp.moveaxis(qp.reshape(B, ATT_KV_HEADS, G, nblk, Q_BLOCK, HEAD_DIM), 3, 0)

    def block(qblk):
        s = jnp.einsum('bkgqd,bksd->bkgqs', qblk, kh)
        p = jax.nn.softmax(s, axis=-1)
        return jnp.einsum('bkgqs,bksd->bkgqd', p.astype(vh.dtype), vh)

    o = lax.map(block, qb)
    o = jnp.moveaxis(o, 0, 3).reshape(B, ATT_KV_HEADS, G, nblk * Q_BLOCK, HEAD_DIM)[:, :, :, pad:]
    return o.transpose(0, 3, 1, 2, 4).reshape(B, L, ATT_Q).astype(q.dtype)


def mixer(z, w_in, gla_w2, gla_b2, gla_gn, q_norm, k_norm, w_pa, w_pb, b_merge, w_out, cos, sin):
    h = z @ w_in
    (q_a, k_a, v_a, r_a, lr_f, lr_b, q_b, k_b, v_b, g_a, g_b) = jnp.split(h, SPLIT_POINTS, axis=-1)
    a = gla_branch(q_a, k_a, v_a, r_a, lr_f, lr_b, gla_w2, gla_b2, gla_gn)
    b = attention_branch(q_b, k_b, v_b, q_norm, k_norm, cos, sin)
    y = (jax.nn.sigmoid(g_a + b_merge[0]) * (a @ w_pa)
         + jax.nn.sigmoid(g_b + b_merge[1]) * (b @ w_pb))
    return y @ w_out


def setup_inputs(seed: int = 0) -> dict:
    key = jax.random.key(seed)
    ks = jax.random.split(key, 18)

    def nrm(k, shape, scale):
        return jax.random.normal(k, shape, jnp.float32) * scale

    return {
        "x": nrm(ks[0], (BATCH, SEQ, D_MODEL), 1.0),
        "meta_tokens": nrm(ks[1], (N_META, D_MODEL), 1.0),
        "norm_gains": 1.0 + nrm(ks[2], (DEPTH, 3, D_MODEL), 0.02),
        "ffn_w_gate": nrm(ks[3], (DEPTH, 2, D_MODEL, D_FF), D_MODEL ** -0.5),
        "ffn_w_up": nrm(ks[4], (DEPTH, 2, D_MODEL, D_FF), D_MODEL ** -0.5),
        "ffn_w_down": nrm(ks[5], (DEPTH, 2, D_FF, D_MODEL), D_FF ** -0.5),
        "w_in": nrm(ks[6], (DEPTH, D_MODEL, D_IN), D_MODEL ** -0.5),
        "gla_w2": nrm(ks[7], (DEPTH, 2, GLA_RANK, GLA_KEY), GLA_RANK ** -0.5),
        "gla_b2": nrm(ks[8], (DEPTH, 2, GLA_KEY), 0.1),
        "gla_gn": 1.0 + nrm(ks[9], (DEPTH, GLA_VAL), 0.02),
        "q_norm": 1.0 + nrm(ks[10], (DEPTH, HEAD_DIM), 0.02),
        "k_norm": 1.0 + nrm(ks[11], (DEPTH, HEAD_DIM), 0.02),
        "w_pa": nrm(ks[12], (DEPTH, GLA_VAL, D_MODEL), GLA_VAL ** -0.5),
        "w_pb": nrm(ks[13], (DEPTH, ATT_Q, D_MODEL), ATT_Q ** -0.5),
        "b_merge": nrm(ks[14], (DEPTH, 2, D_MODEL), 0.02),
        "w_out": nrm(ks[15], (DEPTH, D_MODEL, D_MODEL), D_MODEL ** -0.5),
        "final_norm": 1.0 + nrm(ks[16], (D_MODEL,), 0.02),
    }


def reference(x, meta_tokens, norm_gains, ffn_w_gate, ffn_w_up, ffn_w_down, w_in, gla_w2, gla_b2,
              gla_gn, q_norm, k_norm, w_pa, w_pb, b_merge, w_out, final_norm):
    B, N, D = x.shape
    meta = jnp.broadcast_to(meta_tokens[None].astype(x.dtype), (B, N_META, D))
    h = jnp.concatenate([meta, x], axis=1)
    cos, sin = axial_rope_tables(N)
    for l in range(DEPTH):
        h = h + 0.5 * swiglu(rmsnorm(h, norm_gains[l, 0]),
                             ffn_w_gate[l, 0], ffn_w_up[l, 0], ffn_w_down[l, 0])
        h = h + mixer(rmsnorm(h, norm_gains[l, 1]), w_in[l], gla_w2[l], gla_b2[l], gla_gn[l],
                      q_norm[l], k_norm[l], w_pa[l], w_pb[l], b_merge[l], w_out[l], cos, sin)
        h = h + 0.5 * swiglu(rmsnorm(h, norm_gains[l, 2]),
                             ffn_w_gate[l, 1], ffn_w_up[l, 1], ffn_w_down[l, 1])
    return rmsnorm(h, final_norm)[:, N_META:]
```

```python
import functools

import jax
import jax.numpy as jnp
import numpy as np
from jax import lax
from jax.experimental import pallas as pl
from jax.experimental.pallas import tpu as pltpu

F32 = jnp.float32
BF16 = jnp.bfloat16

N_META = 16
GRID_W = 64
GLA_HEADS = 4
GLA_DK = 64
GLA_DV = 128
GLA_KEY = GLA_HEADS * GLA_DK
GLA_VAL = GLA_HEADS * GLA_DV
GLA_RANK = 16
GLA_TAU = 16.0
CHUNK = 64
ATT_HEADS = 8
ATT_KV_HEADS = 2
HEAD_DIM = 64
ATT_Q = ATT_HEADS * HEAD_DIM
ATT_KV = ATT_KV_HEADS * HEAD_DIM
ROPE_THETA = 10000.0
EPS = 1e-6

LANES = 128
VMEM_LIMIT = 56 * 1024 * 1024

TM_FFN = 512
TF_FFN = 1408
TM_TOK = 384
TQ_ATT = 384
TK_ATT = 512
TM_FIN = 128


def _resident(shape):
    nd = len(shape)
    return pl.BlockSpec(shape, lambda *_: (0,) * nd, pipeline_mode=pl.Buffered(1))


def _rmsnorm_rows(x, gain):
    return x * lax.rsqrt(jnp.mean(x * x, axis=-1, keepdims=True) + EPS) * gain


def _dot(a, b):
    return jnp.dot(a, b, preferred_element_type=F32)


def _dot_nt(a, b):
    return lax.dot_general(a, b, (((1,), (1,)), ((), ())), preferred_element_type=F32)


def _dot_tn(a, b):
    return lax.dot_general(a, b, (((0,), (0,)), ((), ())), preferred_element_type=F32)


def _silu(x):
    return x * jax.nn.sigmoid(x)


def _ffn_body(x_ref, g_ref, wg_ref, wu_ref, wd_ref, o_ref):
    x = x_ref[...]
    xb = _rmsnorm_rows(x, g_ref[...]).astype(BF16)
    d_ff = wg_ref.shape[1]
    acc = jnp.zeros(x.shape, F32)
    for j in range(d_ff // TF_FFN):
        sl = slice(j * TF_FFN, (j + 1) * TF_FFN)
        gate = _dot(xb, wg_ref[:, sl])
        up = _dot(xb, wu_ref[:, sl])
        acc = acc + _dot((_silu(gate) * up).astype(BF16), wd_ref[sl, :])
    o_ref[...] = x + 0.5 * acc


def _ffn(hp, gain, wg, wu, wd):
    n, d = hp.shape
    d_ff = wg.shape[1]
    assert n % TM_FFN == 0 and d_ff % TF_FFN == 0
    row = pl.BlockSpec((TM_FFN, d), lambda i: (i, 0))
    return pl.pallas_call(
        _ffn_body,
        out_shape=jax.ShapeDtypeStruct((n, d), F32),
        grid=(n // TM_FFN,),
        in_specs=[row, _resident((1, d)), _resident((d, d_ff)), _resident((d, d_ff)),
                  _resident((d_ff, d))],
        out_specs=row,
        compiler_params=pltpu.CompilerParams(
            dimension_semantics=("parallel",), vmem_limit_bytes=VMEM_LIMIT),
        name="ffn",
    )(hp, gain, wg, wu, wd)


_C_QA = 0
_C_KA = _C_QA + GLA_KEY
_C_VA = _C_KA + GLA_KEY
_C_RA = _C_VA + GLA_VAL
_C_QB = _C_RA + GLA_VAL
_C_KB = _C_QB + ATT_Q
_C_VB = _C_KB + 2 * ATT_KV
_C_LR = _C_VB + 2 * ATT_KV
_C_END = _C_LR + LANES


def _proj_body(h_ref, g_ref, w_ref, w2_ref, b2_ref, gq_ref, gk_ref, cos_ref, sin_ref,
               qa_ref, ka_ref, va_ref, sr_ref, gf_ref, gb_ref, qb_ref, kb_ref, vb_ref):
    zb = _rmsnorm_rows(h_ref[...], g_ref[...]).astype(BF16)

    def proj(lo, hi):
        return _dot(zb, w_ref[:, lo:hi])

    qa_ref[...] = (proj(_C_QA, _C_KA) * GLA_DK ** -0.5).astype(BF16)
    ka_ref[...] = proj(_C_KA, _C_VA).astype(BF16)
    va_ref[...] = proj(_C_VA, _C_RA).astype(BF16)
    sr_ref[...] = _silu(proj(_C_RA, _C_QB)).astype(BF16)

    lr = proj(_C_LR, _C_END).astype(BF16)
    pre = _dot(lr, w2_ref[...]) + b2_ref[...]
    lg = (jnp.minimum(pre, 0.0) - jnp.log(1.0 + jnp.exp(-jnp.abs(pre)))) * (1.0 / GLA_TAU)
    gf_ref[...] = lg[:, :GLA_KEY]
    gb_ref[...] = lg[:, GLA_KEY:]

    cos = cos_ref[...]
    sin = sin_ref[...]
    lane = lax.broadcasted_iota(jnp.int32, cos.shape, 1)
    low_head = lane < HEAD_DIM
    first_half = (lane % (HEAD_DIM // 2)) < (HEAD_DIM // 4)

    def norm_rope(x, gain, scale):
        x2 = x * x
        s_lo = jnp.sum(jnp.where(low_head, x2, 0.0), axis=-1, keepdims=True)
        s_hi = jnp.sum(jnp.where(low_head, 0.0, x2), axis=-1, keepdims=True)
        inv = jnp.where(low_head, lax.rsqrt(s_lo * (1.0 / HEAD_DIM) + EPS),
                        lax.rsqrt(s_hi * (1.0 / HEAD_DIM) + EPS))
        y = x * inv * gain
        partner = jnp.where(first_half, pltpu.roll(y, LANES - HEAD_DIM // 4, 1),
                            pltpu.roll(y, HEAD_DIM // 4, 1))
        return (y * cos + partner * sin) * scale

    for s in range(ATT_Q // LANES):
        sl = slice(s * LANES, (s + 1) * LANES)
        qb_ref[:, sl] = norm_rope(proj(_C_QB + s * LANES, _C_QB + (s + 1) * LANES),
                                  gq_ref[...], HEAD_DIM ** -0.5).astype(BF16)
    for s in range(ATT_KV_HEADS):
        sl = slice(s * LANES, (s + 1) * LANES)
        kb_ref[:, sl] = norm_rope(proj(_C_KB + s * LANES, _C_KB + (s + 1) * LANES),
                                  gk_ref[...], 1.0).astype(BF16)
    vb_ref[...] = proj(_C_VB, _C_LR).astype(BF16)


def _proj(hp, gain, w_all, w2, b2, gq, gk, cos_t, sin_t, lp):
    n, d = hp.shape
    assert lp % TM_TOK == 0
    tiles_per_seq = lp // TM_TOK

    def row(c):
        return pl.BlockSpec((TM_TOK, c), lambda i: (i, 0))

    table = pl.BlockSpec((TM_TOK, LANES), lambda i: (i % tiles_per_seq, 0))
    outs = [(GLA_KEY, BF16), (GLA_KEY, BF16), (GLA_VAL, BF16), (GLA_VAL, BF16),
            (GLA_KEY, F32), (GLA_KEY, F32), (ATT_Q, BF16), (2 * ATT_KV, BF16),
            (2 * ATT_KV, BF16)]
    return pl.pallas_call(
        _proj_body,
        out_shape=[jax.ShapeDtypeStruct((n, c), t) for c, t in outs],
        grid=(n // TM_TOK,),
        in_specs=[row(d), _resident((1, d)), _resident(w_all.shape), _resident(w2.shape),
                  _resident(b2.shape), _resident((1, LANES)), _resident((1, LANES)),
                  table, table],
        out_specs=[row(c) for c, _ in outs],
        compiler_params=pltpu.CompilerParams(
            dimension_semantics=("parallel",), vmem_limit_bytes=VMEM_LIMIT),
        name="proj",
    )(hp, gain, w_all, w2, b2, gq, gk, cos_t, sin_t)


def _split3(x):
    a = x.astype(BF16)
    r = x - a.astype(F32)
    b = r.astype(BF16)
    c = (r - b.astype(F32)).astype(BF16)
    return a, b, c


def _gla_body(q_ref, k_ref, v_ref, gf_ref, gb_ref, sr_ref, gn_ref, o_ref, ob_ref, st_ref):
    lp = q_ref.shape[0]
    n_chunks = lp // CHUNK
    r_i = lax.broadcasted_iota(jnp.int32, (CHUNK, CHUNK), 0)
    c_i = lax.broadcasted_iota(jnp.int32, (CHUNK, CHUNK), 1)
    lane = lax.broadcasted_iota(jnp.int32, (CHUNK, LANES), 1)
    low_head = lane < GLA_DK

    def chunk(c, g_ref, tri, edge_row, final):
        rows = pl.ds(pl.multiple_of(c * CHUNK, CHUNK), CHUNK)
        g1, g2, g3 = _split3(g_ref[rows, :])
        tri_b = tri.astype(BF16)
        bc = _dot(tri_b, g1) + _dot(tri_b, g2) + _dot(tri_b, g3)
        bt = bc[edge_row:edge_row + 1, :]
        q = q_ref[rows, :].astype(F32)
        k = k_ref[rows, :].astype(F32)
        qd = q * jnp.exp(bc)
        ki = (k * jnp.exp(-bc)).astype(BF16)
        ke = (k * jnp.exp(bt - bc)).astype(BF16)
        dec = jnp.exp(bt)
        for h in range(2):
            hs = slice(h * GLA_DV, (h + 1) * GLA_DV)
            qh = jnp.where(low_head if h == 0 else ~low_head, qd, 0.0).astype(BF16)
            vh = v_ref[rows, hs]
            att = jnp.where(tri, _dot_nt(qh, ki), 0.0).astype(BF16)
            st = st_ref[h]
            o = _dot(att, vh) + _dot_nt(qh, st.astype(BF16))
            st_ref[h] = st * dec + _dot_tn(vh, ke)
            if not final:
                ob_ref[rows, hs] = o
            else:
                o = o + ob_ref[rows, hs]
                o = o * lax.rsqrt(jnp.mean(o * o, axis=-1, keepdims=True) + EPS) * gn_ref[:, hs]
                o_ref[rows, hs] = (o * sr_ref[rows, hs].astype(F32)).astype(BF16)

    st_ref[...] = jnp.zeros(st_ref.shape, F32)

    def bwd(i, carry):
        chunk(n_chunks - 1 - i, gb_ref, r_i <= c_i, 0, False)
        return carry

    lax.fori_loop(0, n_chunks, bwd, 0)

    st_ref[...] = jnp.zeros(st_ref.shape, F32)

    def fwd(i, carry):
        chunk(i, gf_ref, r_i >= c_i, CHUNK - 1, True)
        return carry

    lax.fori_loop(0, n_chunks, fwd, 0)


def _gla(qa, ka, va, gf, gb, sr, gn, lp):
    n = qa.shape[0]
    nb = n // lp
    pairs = GLA_HEADS // 2

    def blk(c):
        return pl.BlockSpec((lp, c), lambda b, p: (b, p))

    return pl.pallas_call(
        _gla_body,
        out_shape=jax.ShapeDtypeStruct((n, GLA_VAL), BF16),
        grid=(nb, pairs),
        in_specs=[blk(LANES), blk(LANES), blk(2 * GLA_DV), blk(LANES), blk(LANES),
                  blk(2 * GLA_DV), pl.BlockSpec((1, 2 * GLA_DV), lambda b, p: (0, p))],
        out_specs=blk(2 * GLA_DV),
        scratch_shapes=[pltpu.VMEM((lp, 2 * GLA_DV), F32),
                        pltpu.VMEM((2, GLA_DV, LANES), F32)],
        compiler_params=pltpu.CompilerParams(
            dimension_semantics=("parallel", "parallel"), vmem_limit_bytes=VMEM_LIMIT),
        name="gla",
    )(qa, ka, va, gf, gb, sr, gn)


def _attn_body(q_ref, k_ref, v_ref, o_ref, *, padf):
    tq = q_ref.shape[0]
    lp = k_ref.shape[0]
    group = ATT_HEADS // ATT_KV_HEADS
    n_main = (lp - LANES) // TK_ATT

    def kv_block(start, size, kvh):
        ks = slice(kvh * LANES, (kvh + 1) * LANES)
        kk = k_ref[pl.ds(start, size), ks]
        vv = v_ref[pl.ds(start, size), ks]
        low = lax.broadcasted_iota(jnp.int32, kk.shape, 1) < HEAD_DIM
        zero = jnp.zeros_like(kk)
        return (jnp.where(low, kk, zero), jnp.where(low, zero, kk),
                jnp.where(low, vv, zero), jnp.where(low, zero, vv))

    def step(q, blocks, carry, key_ok):
        k_lo, k_hi, v_lo, v_hi = blocks
        m_e, m_o, l_e, l_o, acc = carry
        low = lax.broadcasted_iota(jnp.int32, acc.shape, 1) < HEAD_DIM

        def one(kh, m, l):
            s = _dot_nt(q, kh)
            if key_ok is not None:
                s = jnp.where(key_ok, s, -1e30)
            m_new = jnp.maximum(m, jnp.max(s, axis=-1, keepdims=True))
            alpha = jnp.exp(m - m_new)
            p = jnp.exp(s - m_new)
            return p.astype(BF16), m_new, alpha * l + jnp.sum(p, axis=-1, keepdims=True), alpha

        p_e, m_e, l_e, a_e = one(k_lo, m_e, l_e)
        p_o, m_o, l_o, a_o = one(k_hi, m_o, l_o)
        acc = acc * jnp.where(low, a_e, a_o) + _dot(p_e, v_lo) + _dot(p_o, v_hi)
        return m_e, m_o, l_e, l_o, acc

    for pair in range(ATT_HEADS // 2):
        kvh = (2 * pair) // group
        q = q_ref[:, pair * LANES:(pair + 1) * LANES]
        neg = jnp.full((tq, 1), -jnp.inf, F32)
        zero = jnp.zeros((tq, 1), F32)
        carry = (neg, neg, zero, zero, jnp.zeros((tq, LANES), F32))
        key_ok = lax.broadcasted_iota(jnp.int32, (tq, LANES), 1) >= padf
        carry = step(q, kv_block(0, LANES, kvh), carry, key_ok)

        def body(j, carry, q=q, kvh=kvh):
            start = pl.multiple_of(LANES + j * TK_ATT, LANES)
            return step(q, kv_block(start, TK_ATT, kvh), carry, None)

        m_e, m_o, l_e, l_o, acc = lax.fori_loop(0, n_main, body, carry)
        low = lax.broadcasted_iota(jnp.int32, acc.shape, 1) < HEAD_DIM
        o_ref[:, pair * LANES:(pair + 1) * LANES] = (
            acc * jnp.where(low, 1.0 / l_e, 1.0 / l_o)).astype(BF16)


def _attn(qb, kb, vb, lp, padf):
    n = qb.shape[0]
    nb = n // lp
    assert lp % TQ_ATT == 0 and (lp - LANES) % TK_ATT == 0
    nq = lp // TQ_ATT
    q_spec = pl.BlockSpec((TQ_ATT, ATT_Q), lambda b, i: (b * nq + i, 0))
    kv_spec = pl.BlockSpec((lp, 2 * ATT_KV), lambda b, i: (b, 0))
    return pl.pallas_call(
        functools.partial(_attn_body, padf=padf),
        out_shape=jax.ShapeDtypeStruct((n, ATT_Q), BF16),
        grid=(nb, nq),
        in_specs=[q_spec, kv_spec, kv_spec],
        out_specs=q_spec,
        compiler_params=pltpu.CompilerParams(
            dimension_semantics=("parallel", "arbitrary"), vmem_limit_bytes=VMEM_LIMIT),
        name="attn",
    )(qb, kb, vb)


def _merge_body(h_ref, g_ref, a_ref, b_ref, wg_ref, bm_ref, wpa_ref, wpb_ref, wo_ref, o_ref,
                *, lp, padf):
    x = h_ref[...]
    d = x.shape[1]
    zb = _rmsnorm_rows(x, g_ref[...]).astype(BF16)
    gate_a = jax.nn.sigmoid(_dot(zb, wg_ref[:, :d]) + bm_ref[:, :d])
    gate_b = jax.nn.sigmoid(_dot(zb, wg_ref[:, d:]) + bm_ref[:, d:])
    y = gate_a * _dot(a_ref[...], wpa_ref[...]) + gate_b * _dot(b_ref[...], wpb_ref[...])
    out = _dot(y.astype(BF16), wo_ref[...])
    tm = x.shape[0]
    pos = (pl.program_id(0) % (lp // tm)) * tm + lax.broadcasted_iota(jnp.int32, (tm, 1), 0)
    o_ref[...] = x + jnp.where(pos >= padf, out, 0.0)


def _merge(hp, gain, a, b, wg, bm, wpa, wpb, wo, lp, padf):
    n, d = hp.shape

    def row(c):
        return pl.BlockSpec((TM_TOK, c), lambda i: (i, 0))

    return pl.pallas_call(
        functools.partial(_merge_body, lp=lp, padf=padf),
        out_shape=jax.ShapeDtypeStruct((n, d), F32),
        grid=(n // TM_TOK,),
        in_specs=[row(d), _resident((1, d)), row(GLA_VAL), row(ATT_Q), _resident(wg.shape),
                  _resident(bm.shape), _resident(wpa.shape), _resident(wpb.shape),
                  _resident(wo.shape)],
        out_specs=row(d),
        compiler_params=pltpu.CompilerParams(
            dimension_semantics=("parallel",), vmem_limit_bytes=VMEM_LIMIT),
        name="merge",
    )(hp, gain, a, b, wg, bm, wpa, wpb, wo)


def _final_body(h_ref, g_ref, o_ref):
    o_ref[...] = _rmsnorm_rows(h_ref[...], g_ref[...])


def _final(hp, gain, nb, n_tok, lp, skip):
    d = hp.shape[1]
    assert skip % TM_FIN == 0 and n_tok % TM_FIN == 0
    per_in, per_out, off = lp // TM_FIN, n_tok // TM_FIN, skip // TM_FIN
    return pl.pallas_call(
        _final_body,
        out_shape=jax.ShapeDtypeStruct((nb * n_tok, d), F32),
        grid=(nb, per_out),
        in_specs=[pl.BlockSpec((TM_FIN, d), lambda b, i: (b * per_in + off + i, 0)),
                  _resident((1, d))],
        out_specs=pl.BlockSpec((TM_FIN, d), lambda b, i: (b * per_out + i, 0)),
        compiler_params=pltpu.CompilerParams(
            dimension_semantics=("parallel", "parallel"), vmem_limit_bytes=VMEM_LIMIT),
        name="final_norm",
    )(hp, gain)


def _rope_tables(n_tok, lp, padf):
    quarter = HEAD_DIM // 4
    pos = jnp.arange(n_tok)
    inv = ROPE_THETA ** (-jnp.arange(0, HEAD_DIM // 2, 2, dtype=F32) / (HEAD_DIM // 2))
    ang_r = (pos // GRID_W).astype(F32)[:, None] * inv
    ang_c = (pos % GRID_W).astype(F32)[:, None] * inv
    cos = jnp.concatenate([jnp.cos(ang_r)] * 2 + [jnp.cos(ang_c)] * 2, axis=-1)
    sin = jnp.concatenate([-jnp.sin(ang_r), jnp.sin(ang_r), -jnp.sin(ang_c), jnp.sin(ang_c)],
                          axis=-1)
    lead = padf + N_META
    cos = jnp.concatenate([jnp.ones((lead, HEAD_DIM), F32), cos], axis=0)
    sin = jnp.concatenate([jnp.zeros((lead, HEAD_DIM), F32), sin], axis=0)
    assert cos.shape == (lp, 4 * quarter)
    return jnp.tile(cos, (1, 2)), jnp.tile(sin, (1, 2))


def _prep_w_in(w_in):
    sizes = (GLA_KEY, GLA_KEY, GLA_VAL, GLA_VAL, GLA_RANK, GLA_RANK, ATT_Q, ATT_KV, ATT_KV)
    offs = np.concatenate([[0], np.cumsum(sizes)])
    qa, ka, va, ra, lrf, lrb, qb, kb, vb = (w_in[:, offs[i]:offs[i + 1]] for i in range(9))

    def dup(w):
        heads = [w[:, h * HEAD_DIM:(h + 1) * HEAD_DIM] for h in range(ATT_KV_HEADS)]
        return jnp.concatenate([x for h in heads for x in (h, h)], axis=-1)

    lr = jnp.concatenate([lrf, lrb, jnp.zeros((w_in.shape[0], LANES - 2 * GLA_RANK), F32)], -1)
    return jnp.concatenate([qa, ka, va, ra, qb, dup(kb), dup(vb), lr], axis=-1).astype(BF16)


def _prep_gate(w2, b2):
    w = jnp.zeros((LANES, 2 * GLA_KEY), F32)
    w = w.at[:GLA_RANK, :GLA_KEY].set(w2[0]).at[GLA_RANK:2 * GLA_RANK, GLA_KEY:].set(w2[1])
    return w.astype(BF16), b2.reshape(1, 2 * GLA_KEY)


def kernel(x, meta_tokens, norm_gains, ffn_w_gate, ffn_w_up, ffn_w_down, w_in, gla_w2, gla_b2,
           gla_gn, q_norm, k_norm, w_pa, w_pb, b_merge, w_out, final_norm):
    nb, n_tok, d = x.shape
    depth = w_in.shape[0]
    lp = -(-(n_tok + N_META) // LANES) * LANES
    padf = lp - n_tok - N_META
    assert (padf + N_META) % CHUNK == 0 and n_tok % GRID_W == 0

    meta = jnp.broadcast_to(meta_tokens[None].astype(x.dtype), (nb, N_META, d))
    hp = jnp.concatenate([jnp.zeros((nb, padf, d), x.dtype), meta, x], axis=1).reshape(nb * lp, d)
    cos_t, sin_t = _rope_tables(n_tok, lp, padf)

    g_split = w_in.shape[2] - 2 * d
    for l in range(depth):
        gains = norm_gains[l].reshape(3, 1, d)
        hp = _ffn(hp, gains[0], ffn_w_gate[l, 0].astype(BF16), ffn_w_up[l, 0].astype(BF16),
                  ffn_w_down[l, 0].astype(BF16))
        w2, b2 = _prep_gate(gla_w2[l], gla_b2[l])
        qa, ka, va, sr, gf, gb, qb, kb, vb = _proj(
            hp, gains[1], _prep_w_in(w_in[l, :, :g_split]), w2, b2,
            jnp.tile(q_norm[l], 2).reshape(1, LANES), jnp.tile(k_norm[l], 2).reshape(1, LANES),
            cos_t, sin_t, lp)
        a = _gla(qa, ka, va, gf, gb, sr, gla_gn[l].reshape(1, GLA_VAL), lp)
        b = _attn(qb, kb, vb, lp, padf)
        hp = _merge(hp, gains[1], a, b, w_in[l, :, g_split:].astype(BF16),
                    b_merge[l].reshape(1, 2 * d), w_pa[l].astype(BF16), w_pb[l].astype(BF16),
                    w_out[l].astype(BF16), lp, padf)
        hp = _ffn(hp, gains[2], ffn_w_gate[l, 1].astype(BF16), ffn_w_up[l, 1].astype(BF16),
                  ffn_w_down[l, 1].astype(BF16))
    out = _final(hp, final_norm.reshape(1, d), nb, n_tok, lp, padf + N_META)
    return out.reshape(nb, n_tok, d)
```

```python
import functools

import jax
import jax.numpy as jnp
import numpy as np
from jax import lax
from jax.experimental import pallas as pl
from jax.experimental.pallas import tpu as pltpu

F32 = jnp.float32
BF16 = jnp.bfloat16

N_META = 16
GRID_W = 64
GLA_HEADS = 4
GLA_DK = 64
GLA_DV = 128
GLA_KEY = GLA_HEADS * GLA_DK
GLA_VAL = GLA_HEADS * GLA_DV
GLA_RANK = 16
GLA_TAU = 16.0
CHUNK = 64
ATT_HEADS = 8
ATT_KV_HEADS = 2
HEAD_DIM = 64
ATT_Q = ATT_HEADS * HEAD_DIM
ATT_KV = ATT_KV_HEADS * HEAD_DIM
ROPE_THETA = 10000.0
EPS = 1e-6

LANES = 128
VMEM_LIMIT = 56 * 1024 * 1024

TM_FFN = 512
TF_FFN = 1408
TM_TOK = 384
TQ_ATT = 128
TK_ATT = 512
TM_FIN = 128


def _resident(shape):
    nd = len(shape)
    return pl.BlockSpec(shape, lambda *_: (0,) * nd, pipeline_mode=pl.Buffered(1))


def _rmsnorm_rows(x, gain):
    return x * lax.rsqrt(jnp.mean(x * x, axis=-1, keepdims=True) + EPS) * gain


def _dot(a, b):
    return jnp.dot(a, b, preferred_element_type=F32)


def _dot_nt(a, b):
    return lax.dot_general(a, b, (((1,), (1,)), ((), ())), preferred_element_type=F32)


def _dot_tn(a, b):
    return lax.dot_general(a, b, (((0,), (0,)), ((), ())), preferred_element_type=F32)


def _silu(x):
    return x * jax.nn.sigmoid(x)


def _ffn_body(x_ref, g_ref, wg_ref, wu_ref, wd_ref, o_ref):
    x = x_ref[...]
    xb = _rmsnorm_rows(x, g_ref[...]).astype(BF16)
    d_ff = wg_ref.shape[1]
    acc = jnp.zeros(x.shape, F32)
    for j in range(d_ff // TF_FFN):
        sl = slice(j * TF_FFN, (j + 1) * TF_FFN)
        gate = _dot(xb, wg_ref[:, sl])
        up = _dot(xb, wu_ref[:, sl])
        acc = acc + _dot((_silu(gate) * up).astype(BF16), wd_ref[sl, :])
    o_ref[...] = x + 0.5 * acc


def _ffn(hp, gain, wg, wu, wd):
    n, d = hp.shape
    d_ff = wg.shape[1]
    assert n % TM_FFN == 0 and d_ff % TF_FFN == 0
    row = pl.BlockSpec((TM_FFN, d), lambda i: (i, 0))
    return pl.pallas_call(
        _ffn_body,
        out_shape=jax.ShapeDtypeStruct((n, d), F32),
        grid=(n // TM_FFN,),
        in_specs=[row, _resident((1, d)), _resident((d, d_ff)), _resident((d, d_ff)),
                  _resident((d_ff, d))],
        out_specs=row,
        compiler_params=pltpu.CompilerParams(
            dimension_semantics=("parallel",), vmem_limit_bytes=VMEM_LIMIT),
        name="ffn",
    )(hp, gain, wg, wu, wd)


_C_QA = 0
_C_KA = _C_QA + GLA_KEY
_C_VA = _C_KA + GLA_KEY
_C_RA = _C_VA + GLA_VAL
_C_QB = _C_RA + GLA_VAL
_C_KB = _C_QB + ATT_Q
_C_LR = _C_KB + 2 * ATT_KV
_C_END = _C_LR + LANES
_Q_SCALE = HEAD_DIM ** -0.5 * float(np.log2(np.e))


def _proj_body(h_ref, g_ref, w_ref, wvt_ref, w2_ref, b2_ref, gq_ref, gk_ref, cos_ref, sin_ref,
               qa_ref, ka_ref, va_ref, sr_ref, gf_ref, gb_ref, qb_ref, kb_ref, vt_ref):
    zb = _rmsnorm_rows(h_ref[...], g_ref[...]).astype(BF16)

    def proj(lo, hi):
        return _dot(zb, w_ref[:, lo:hi])

    qa_ref[...] = (proj(_C_QA, _C_KA) * GLA_DK ** -0.5).astype(BF16)
    ka_ref[...] = proj(_C_KA, _C_VA).astype(BF16)
    va_ref[...] = proj(_C_VA, _C_RA).astype(BF16)
    sr_ref[...] = _silu(proj(_C_RA, _C_QB)).astype(BF16)

    lr = proj(_C_LR, _C_END).astype(BF16)
    pre = _dot(lr, w2_ref[...]) + b2_ref[...]
    lg = (jnp.minimum(pre, 0.0) - jnp.log(1.0 + jnp.exp(-jnp.abs(pre)))) * (1.0 / GLA_TAU)
    gf_ref[...] = lg[:, :GLA_KEY]
    gb_ref[...] = lg[:, GLA_KEY:]

    cos = cos_ref[...]
    sin = sin_ref[...]
    lane = lax.broadcasted_iota(jnp.int32, cos.shape, 1)
    low_head = lane < HEAD_DIM
    first_half = (lane % (HEAD_DIM // 2)) < (HEAD_DIM // 4)

    def norm_rope(x, gain, scale):
        x2 = x * x
        s_lo = jnp.sum(jnp.where(low_head, x2, 0.0), axis=-1, keepdims=True)
        s_hi = jnp.sum(jnp.where(low_head, 0.0, x2), axis=-1, keepdims=True)
        inv = jnp.where(low_head, lax.rsqrt(s_lo * (1.0 / HEAD_DIM) + EPS),
                        lax.rsqrt(s_hi * (1.0 / HEAD_DIM) + EPS))
        y = x * inv * gain
        partner = jnp.where(first_half, pltpu.roll(y, LANES - HEAD_DIM // 4, 1),
                            pltpu.roll(y, HEAD_DIM // 4, 1))
        return (y * cos + partner * sin) * scale

    for s in range(ATT_Q // LANES):
        sl = slice(s * LANES, (s + 1) * LANES)
        qb_ref[:, sl] = norm_rope(proj(_C_QB + s * LANES, _C_QB + (s + 1) * LANES),
                                  gq_ref[...], _Q_SCALE).astype(BF16)
    for s in range(ATT_KV_HEADS):
        sl = slice(s * LANES, (s + 1) * LANES)
        kb_ref[:, sl] = norm_rope(proj(_C_KB + s * LANES, _C_KB + (s + 1) * LANES),
                                  gk_ref[...], 1.0).astype(BF16)
    vt = _dot_nt(wvt_ref[...], zb)
    row_i = lax.broadcasted_iota(jnp.int32, vt.shape, 0)
    vt_ref[...] = jnp.where(row_i % LANES < HEAD_DIM, vt, 1.0).astype(BF16)


def _proj(hp, gain, w_all, wvt, w2, b2, gq, gk, cos_t, sin_t, lp):
    n, d = hp.shape
    assert lp % TM_TOK == 0
    tiles_per_seq = lp // TM_TOK

    def row(c):
        return pl.BlockSpec((TM_TOK, c), lambda i: (i, 0))

    table = pl.BlockSpec((TM_TOK, LANES), lambda i: (i % tiles_per_seq, 0))
    outs = [(GLA_KEY, BF16), (GLA_KEY, BF16), (GLA_VAL, BF16), (GLA_VAL, BF16),
            (GLA_KEY, F32), (GLA_KEY, F32), (ATT_Q, BF16), (2 * ATT_KV, BF16)]
    vt_rows = wvt.shape[0]
    return pl.pallas_call(
        _proj_body,
        out_shape=[jax.ShapeDtypeStruct((n, c), t) for c, t in outs]
        + [jax.ShapeDtypeStruct((vt_rows, n), BF16)],
        grid=(n // TM_TOK,),
        in_specs=[row(d), _resident((1, d)), _resident(w_all.shape), _resident(wvt.shape),
                  _resident(w2.shape), _resident(b2.shape), _resident((1, LANES)),
                  _resident((1, LANES)), table, table],
        out_specs=[row(c) for c, _ in outs]
        + [pl.BlockSpec((vt_rows, TM_TOK), lambda i: (0, i))],
        compiler_params=pltpu.CompilerParams(
            dimension_semantics=("parallel",), vmem_limit_bytes=VMEM_LIMIT),
        name="proj",
    )(hp, gain, w_all, wvt, w2, b2, gq, gk, cos_t, sin_t)


def _split3(x):
    a = x.astype(BF16)
    r = x - a.astype(F32)
    b = r.astype(BF16)
    c = (r - b.astype(F32)).astype(BF16)
    return a, b, c


def _gla_body(q_ref, k_ref, v_ref, gf_ref, gb_ref, sr_ref, gn_ref, o_ref, ob_ref, st_ref):
    lp = q_ref.shape[0]
    n_chunks = lp // CHUNK
    r_i = lax.broadcasted_iota(jnp.int32, (CHUNK, CHUNK), 0)
    c_i = lax.broadcasted_iota(jnp.int32, (CHUNK, CHUNK), 1)
    lane = lax.broadcasted_iota(jnp.int32, (CHUNK, LANES), 1)
    low_head = lane < GLA_DK

    def chunk(c, g_ref, tri, edge_row, final):
        rows = pl.ds(pl.multiple_of(c * CHUNK, CHUNK), CHUNK)
        g1, g2, g3 = _split3(g_ref[rows, :])
        tri_b = tri.astype(BF16)
        bc = _dot(tri_b, g1) + _dot(tri_b, g2) + _dot(tri_b, g3)
        bt = bc[edge_row:edge_row + 1, :]
        q = q_ref[rows, :].astype(F32)
        k = k_ref[rows, :].astype(F32)
        qd = q * jnp.exp(bc)
        ki = (k * jnp.exp(-bc)).astype(BF16)
        ke = (k * jnp.exp(bt - bc)).astype(BF16)
        dec = jnp.exp(bt)
        for h in range(2):
            hs = slice(h * GLA_DV, (h + 1) * GLA_DV)
            qh = jnp.where(low_head if h == 0 else ~low_head, qd, 0.0).astype(BF16)
            vh = v_ref[rows, hs]
            att = jnp.where(tri, _dot_nt(qh, ki), 0.0).astype(BF16)
            st = st_ref[h]
            o = _dot(att, vh) + _dot_nt(qh, st.astype(BF16))
            st_ref[h] = st * dec + _dot_tn(vh, ke)
            if not final:
                ob_ref[rows, hs] = o
            else:
                o = o + ob_ref[rows, hs]
                o = o * lax.rsqrt(jnp.mean(o * o, axis=-1, keepdims=True) + EPS) * gn_ref[:, hs]
                o_ref[rows, hs] = (o * sr_ref[rows, hs].astype(F32)).astype(BF16)

    st_ref[...] = jnp.zeros(st_ref.shape, F32)

    def bwd(i, carry):
        chunk(n_chunks - 1 - i, gb_ref, r_i <= c_i, 0, False)
        return carry

    lax.fori_loop(0, n_chunks, bwd, 0)

    st_ref[...] = jnp.zeros(st_ref.shape, F32)

    def fwd(i, carry):
        chunk(i, gf_ref, r_i >= c_i, CHUNK - 1, True)
        return carry

    lax.fori_loop(0, n_chunks, fwd, 0)


def _gla(qa, ka, va, gf, gb, sr, gn, lp):
    n = qa.shape[0]
    nb = n // lp
    pairs = GLA_HEADS // 2

    def blk(c):
        return pl.BlockSpec((lp, c), lambda b, p: (b, p))

    return pl.pallas_call(
        _gla_body,
        out_shape=jax.ShapeDtypeStruct((n, GLA_VAL), BF16),
        grid=(nb, pairs),
        in_specs=[blk(LANES), blk(LANES), blk(2 * GLA_DV), blk(LANES), blk(LANES),
                  blk(2 * GLA_DV), pl.BlockSpec((1, 2 * GLA_DV), lambda b, p: (0, p))],
        out_specs=blk(2 * GLA_DV),
        scratch_shapes=[pltpu.VMEM((lp, 2 * GLA_DV), F32),
                        pltpu.VMEM((2, GLA_DV, LANES), F32)],
        compiler_params=pltpu.CompilerParams(
            dimension_semantics=("parallel", "parallel"), vmem_limit_bytes=VMEM_LIMIT),
        name="gla",
    )(qa, ka, va, gf, gb, sr, gn)


def _attn_body(q_ref, k_ref, vt_ref, o_ref, s_ref, *, padf):
    tq = q_ref.shape[0]
    lp = k_ref.shape[0]
    group = ATT_HEADS // ATT_KV_HEADS
    cols = group * tq
    n_main = (lp - LANES) // TK_ATT
    low = lax.broadcasted_iota(jnp.int32, (tq, LANES), 1) < HEAD_DIM

    for kvh in range(ATT_KV_HEADS):
        hs = slice(kvh * LANES, (kvh + 1) * LANES)
        parts = []
        for s in range(group // 2):
            slab = kvh * (group // 2) + s
            q = q_ref[:, slab * LANES:(slab + 1) * LANES]
            zero = jnp.zeros_like(q)
            parts += [jnp.where(low, q, zero), jnp.where(low, zero, q)]
        qs = jnp.concatenate(parts, axis=0)

        def scores(start, size, qs=qs, hs=hs):
            return _dot_nt(k_ref[pl.ds(start, size), hs], qs)

        def absorb(s, start, size, m, acc, hs=hs):
            m_new = jnp.maximum(m, jnp.max(s, axis=0, keepdims=True))
            p = jnp.exp2(s - m_new).astype(BF16)
            pv = _dot(vt_ref[hs, pl.ds(start, size)], p)
            return m_new, acc * jnp.exp2(m - m_new) + pv

        def main_start(j):
            return pl.multiple_of(LANES + j * TK_ATT, LANES)

        key_ok = lax.broadcasted_iota(jnp.int32, (LANES, cols), 0) >= padf
        m, acc = absorb(jnp.where(key_ok, scores(0, LANES), -1e30), 0, LANES,
                        jnp.full((1, cols), -jnp.inf, F32), jnp.zeros((LANES, cols), F32))

        assert n_main % 2 == 0
        s_ref[0] = scores(main_start(0), TK_ATT)

        def pair(i, carry, last):
            m, acc = carry
            s_ref[1] = scores(main_start(2 * i + 1), TK_ATT)
            m, acc = absorb(s_ref[0], main_start(2 * i), TK_ATT, m, acc)
            if not last:
                s_ref[0] = scores(main_start(2 * i + 2), TK_ATT)
            return absorb(s_ref[1], main_start(2 * i + 1), TK_ATT, m, acc)

        m, acc = lax.fori_loop(0, n_main // 2 - 1, lambda i, c: pair(i, c, False), (m, acc))
        _, acc = pair(n_main // 2 - 1, (m, acc), True)
        out_t = acc[:HEAD_DIM] / acc[HEAD_DIM:]
        for s in range(group // 2):
            slab = kvh * (group // 2) + s
            pair_t = jnp.concatenate([out_t[:, (2 * s) * tq:(2 * s + 1) * tq],
                                      out_t[:, (2 * s + 1) * tq:(2 * s + 2) * tq]], axis=0)
            o_ref[:, slab * LANES:(slab + 1) * LANES] = pair_t.T.astype(BF16)


def _attn(qb, kb, vt, lp, padf):
    n = qb.shape[0]
    nb = n // lp
    assert lp % TQ_ATT == 0 and (lp - LANES) % TK_ATT == 0
    nq = lp // TQ_ATT
    q_spec = pl.BlockSpec((TQ_ATT, ATT_Q), lambda b, i: (b * nq + i, 0))
    return pl.pallas_call(
        functools.partial(_attn_body, padf=padf),
        out_shape=jax.ShapeDtypeStruct((n, ATT_Q), BF16),
        grid=(nb, nq),
        in_specs=[q_spec, pl.BlockSpec((lp, kb.shape[1]), lambda b, i: (b, 0)),
                  pl.BlockSpec((vt.shape[0], lp), lambda b, i: (0, b))],
        out_specs=q_spec,
        scratch_shapes=[pltpu.VMEM((2, TK_ATT, ATT_HEADS // ATT_KV_HEADS * TQ_ATT), F32)],
        compiler_params=pltpu.CompilerParams(
            dimension_semantics=("parallel", "arbitrary"), vmem_limit_bytes=VMEM_LIMIT),
        name="attn",
    )(qb, kb, vt)


def _merge_body(h_ref, g_ref, a_ref, b_ref, wg_ref, bm_ref, wpa_ref, wpb_ref, wo_ref, o_ref,
                *, lp, padf):
    x = h_ref[...]
    d = x.shape[1]
    zb = _rmsnorm_rows(x, g_ref[...]).astype(BF16)
    gate_a = jax.nn.sigmoid(_dot(zb, wg_ref[:, :d]) + bm_ref[:, :d])
    gate_b = jax.nn.sigmoid(_dot(zb, wg_ref[:, d:]) + bm_ref[:, d:])
    y = gate_a * _dot(a_ref[...], wpa_ref[...]) + gate_b * _dot(b_ref[...], wpb_ref[...])
    out = _dot(y.astype(BF16), wo_ref[...])
    tm = x.shape[0]
    pos = (pl.program_id(0) % (lp // tm)) * tm + lax.broadcasted_iota(jnp.int32, (tm, 1), 0)
    o_ref[...] = x + jnp.where(pos >= padf, out, 0.0)


def _merge(hp, gain, a, b, wg, bm, wpa, wpb, wo, lp, padf):
    n, d = hp.shape

    def row(c):
        return pl.BlockSpec((TM_TOK, c), lambda i: (i, 0))

    return pl.pallas_call(
        functools.partial(_merge_body, lp=lp, padf=padf),
        out_shape=jax.ShapeDtypeStruct((n, d), F32),
        grid=(n // TM_TOK,),
        in_specs=[row(d), _resident((1, d)), row(GLA_VAL), row(ATT_Q), _resident(wg.shape),
                  _resident(bm.shape), _resident(wpa.shape), _resident(wpb.shape),
                  _resident(wo.shape)],
        out_specs=row(d),
        compiler_params=pltpu.CompilerParams(
            dimension_semantics=("parallel",), vmem_limit_bytes=VMEM_LIMIT),
        name="merge",
    )(hp, gain, a, b, wg, bm, wpa, wpb, wo)


def _final_body(h_ref, g_ref, o_ref):
    o_ref[...] = _rmsnorm_rows(h_ref[...], g_ref[...])


def _final(hp, gain, nb, n_tok, lp, skip):
    d = hp.shape[1]
    assert skip % TM_FIN == 0 and n_tok % TM_FIN == 0
    per_in, per_out, off = lp // TM_FIN, n_tok // TM_FIN, skip // TM_FIN
    return pl.pallas_call(
        _final_body,
        out_shape=jax.ShapeDtypeStruct((nb * n_tok, d), F32),
        grid=(nb, per_out),
        in_specs=[pl.BlockSpec((TM_FIN, d), lambda b, i: (b * per_in + off + i, 0)),
                  _resident((1, d))],
        out_specs=pl.BlockSpec((TM_FIN, d), lambda b, i: (b * per_out + i, 0)),
        compiler_params=pltpu.CompilerParams(
            dimension_semantics=("parallel", "parallel"), vmem_limit_bytes=VMEM_LIMIT),
        name="final_norm",
    )(hp, gain)


def _rope_tables(n_tok, lp, padf):
    quarter = HEAD_DIM // 4
    pos = jnp.arange(n_tok)
    inv = ROPE_THETA ** (-jnp.arange(0, HEAD_DIM // 2, 2, dtype=F32) / (HEAD_DIM // 2))
    ang_r = (pos // GRID_W).astype(F32)[:, None] * inv
    ang_c = (pos % GRID_W).astype(F32)[:, None] * inv
    cos = jnp.concatenate([jnp.cos(ang_r)] * 2 + [jnp.cos(ang_c)] * 2, axis=-1)
    sin = jnp.concatenate([-jnp.sin(ang_r), jnp.sin(ang_r), -jnp.sin(ang_c), jnp.sin(ang_c)],
                          axis=-1)
    lead = padf + N_META
    cos = jnp.concatenate([jnp.ones((lead, HEAD_DIM), F32), cos], axis=0)
    sin = jnp.concatenate([jnp.zeros((lead, HEAD_DIM), F32), sin], axis=0)
    assert cos.shape == (lp, 4 * quarter)
    return jnp.tile(cos, (1, 2)), jnp.tile(sin, (1, 2))


def _prep_w_in(w_in):
    sizes = (GLA_KEY, GLA_KEY, GLA_VAL, GLA_VAL, GLA_RANK, GLA_RANK, ATT_Q, ATT_KV, ATT_KV)
    offs = np.concatenate([[0], np.cumsum(sizes)])
    qa, ka, va, ra, lrf, lrb, qb, kb, vb = (w_in[:, offs[i]:offs[i + 1]] for i in range(9))

    def spread(w, twice):
        heads = [w[:, h * HEAD_DIM:(h + 1) * HEAD_DIM] for h in range(ATT_KV_HEADS)]
        return jnp.concatenate([x for h in heads for x in (h, h if twice else jnp.zeros_like(h))],
                               axis=-1)

    lr = jnp.concatenate([lrf, lrb, jnp.zeros((w_in.shape[0], LANES - 2 * GLA_RANK), F32)], -1)
    w_all = jnp.concatenate([qa, ka, va, ra, qb, spread(kb, True), lr], axis=-1).astype(BF16)
    return w_all, spread(vb, False).T.astype(BF16)


def _prep_gate(w2, b2):
    w = jnp.zeros((LANES, 2 * GLA_KEY), F32)
    w = w.at[:GLA_RANK, :GLA_KEY].set(w2[0]).at[GLA_RANK:2 * GLA_RANK, GLA_KEY:].set(w2[1])
    return w.astype(BF16), b2.reshape(1, 2 * GLA_KEY)


def kernel(x, meta_tokens, norm_gains, ffn_w_gate, ffn_w_up, ffn_w_down, w_in, gla_w2, gla_b2,
           gla_gn, q_norm, k_norm, w_pa, w_pb, b_merge, w_out, final_norm):
    nb, n_tok, d = x.shape
    depth = w_in.shape[0]
    lp = -(-(n_tok + N_META) // LANES) * LANES
    padf = lp - n_tok - N_META
    assert (padf + N_META) % CHUNK == 0 and n_tok % GRID_W == 0

    meta = jnp.broadcast_to(meta_tokens[None].astype(x.dtype), (nb, N_META, d))
    hp = jnp.concatenate([jnp.zeros((nb, padf, d), x.dtype), meta, x], axis=1).reshape(nb * lp, d)
    cos_t, sin_t = _rope_tables(n_tok, lp, padf)

    g_split = w_in.shape[2] - 2 * d
    for l in range(depth):
        gains = norm_gains[l].reshape(3, 1, d)
        hp = _ffn(hp, gains[0], ffn_w_gate[l, 0].astype(BF16), ffn_w_up[l, 0].astype(BF16),
                  ffn_w_down[l, 0].astype(BF16))
        w2, b2 = _prep_gate(gla_w2[l], gla_b2[l])
        w_all, wvt = _prep_w_in(w_in[l, :, :g_split])
        qa, ka, va, sr, gf, gb, qb, kb, vt = _proj(
            hp, gains[1], w_all, wvt, w2, b2,
            jnp.tile(q_norm[l], 2).reshape(1, LANES), jnp.tile(k_norm[l], 2).reshape(1, LANES),
            cos_t, sin_t, lp)
        a = _gla(qa, ka, va, gf, gb, sr, gla_gn[l].reshape(1, GLA_VAL), lp)
        b = _attn(qb, kb, vt, lp, padf)
        hp = _merge(hp, gains[1], a, b, w_in[l, :, g_split:].astype(BF16),
                    b_merge[l].reshape(1, 2 * d), w_pa[l].astype(BF16), w_pb[l].astype(BF16),
                    w_out[l].astype(BF16), lp, padf)
        hp = _ffn(hp, gains[2], ffn_w_gate[l, 1].astype(BF16), ffn_w_up[l, 1].astype(BF16),
                  ffn_w_down[l, 1].astype(BF16))
    out = _final(hp, final_norm.reshape(1, d), nb, n_tok, lp, padf + N_META)
    return out.reshape(nb, n_tok, d)
```

```python
import functools

import jax
import jax.numpy as jnp
import numpy as np
from jax import lax
from jax.experimental import pallas as pl
from jax.experimental.pallas import tpu as pltpu

F32 = jnp.float32
BF16 = jnp.bfloat16

N_META = 16
GRID_W = 64
GLA_HEADS = 4
GLA_DK = 64
GLA_DV = 128
GLA_KEY = GLA_HEADS * GLA_DK
GLA_VAL = GLA_HEADS * GLA_DV
GLA_RANK = 16
GLA_TAU = 16.0
CHUNK = 64
ATT_HEADS = 8
ATT_KV_HEADS = 2
HEAD_DIM = 64
ATT_Q = ATT_HEADS * HEAD_DIM
ATT_KV = ATT_KV_HEADS * HEAD_DIM
ROPE_THETA = 10000.0
EPS = 1e-6

LANES = 128
VMEM_LIMIT = 56 * 1024 * 1024

TM_FFN = 512
TF_FFN = 1408
TM_TOK = 384
TQ_ATT = 128
TK_ATT = 512
TM_FIN = 128
GLA_ROWS = 256
GLA_LEAD = 128


def _resident(shape):
    nd = len(shape)
    return pl.BlockSpec(shape, lambda *_: (0,) * nd, pipeline_mode=pl.Buffered(1))


def _rmsnorm_rows(x, gain):
    return x * lax.rsqrt(jnp.mean(x * x, axis=-1, keepdims=True) + EPS) * gain


def _dot(a, b):
    return jnp.dot(a, b, preferred_element_type=F32)


def _dot_nt(a, b):
    return lax.dot_general(a, b, (((1,), (1,)), ((), ())), preferred_element_type=F32)


def _dot_tn(a, b):
    return lax.dot_general(a, b, (((0,), (0,)), ((), ())), preferred_element_type=F32)


def _silu(x):
    return x * jax.nn.sigmoid(x)


def _ffn_body(x_ref, g_ref, wg_ref, wu_ref, wd_ref, o_ref):
    x = x_ref[...]
    xb = _rmsnorm_rows(x, g_ref[...]).astype(BF16)
    d_ff = wg_ref.shape[1]
    acc = jnp.zeros(x.shape, F32)
    for j in range(d_ff // TF_FFN):
        sl = slice(j * TF_FFN, (j + 1) * TF_FFN)
        gate = _dot(xb, wg_ref[:, sl])
        up = _dot(xb, wu_ref[:, sl])
        acc = acc + _dot((_silu(gate) * up).astype(BF16), wd_ref[sl, :])
    o_ref[...] = x + 0.5 * acc


def _ffn(hp, gain, wg, wu, wd):
    n, d = hp.shape
    d_ff = wg.shape[1]
    assert n % TM_FFN == 0 and d_ff % TF_FFN == 0
    row = pl.BlockSpec((TM_FFN, d), lambda i: (i, 0))
    return pl.pallas_call(
        _ffn_body,
        out_shape=jax.ShapeDtypeStruct((n, d), F32),
        grid=(n // TM_FFN,),
        in_specs=[row, _resident((1, d)), _resident((d, d_ff)), _resident((d, d_ff)),
                  _resident((d_ff, d))],
        out_specs=row,
        compiler_params=pltpu.CompilerParams(
            dimension_semantics=("parallel",), vmem_limit_bytes=VMEM_LIMIT),
        name="ffn",
    )(hp, gain, wg, wu, wd)


_C_QA = 0
_C_KA = _C_QA + GLA_KEY
_C_VA = _C_KA + GLA_KEY
_C_RA = _C_VA + GLA_VAL
_C_QB = _C_RA + GLA_VAL
_C_KB = _C_QB + ATT_Q
_C_LR = _C_KB + 2 * ATT_KV
_C_END = _C_LR + LANES
_Q_SCALE = HEAD_DIM ** -0.5 * float(np.log2(np.e))


def _proj_body(h_ref, g_ref, w_ref, wvt_ref, w2_ref, b2_ref, gq_ref, gk_ref, cos_ref, sin_ref,
               qa_ref, ka_ref, va_ref, sr_ref, gf_ref, gb_ref, qb_ref, kb_ref, vt_ref):
    zb = _rmsnorm_rows(h_ref[...], g_ref[...]).astype(BF16)

    def proj(lo, hi):
        return _dot(zb, w_ref[:, lo:hi])

    qa_ref[...] = (proj(_C_QA, _C_KA) * GLA_DK ** -0.5).astype(BF16)
    ka_ref[...] = proj(_C_KA, _C_VA).astype(BF16)
    va_ref[...] = proj(_C_VA, _C_RA).astype(BF16)
    sr_ref[...] = _silu(proj(_C_RA, _C_QB)).astype(BF16)

    lr = proj(_C_LR, _C_END).astype(BF16)
    pre = _dot(lr, w2_ref[...]) + b2_ref[...]
    lg = (jnp.minimum(pre, 0.0) - jnp.log(1.0 + jnp.exp(-jnp.abs(pre)))) * (1.0 / GLA_TAU)
    gf_ref[...] = lg[:, :GLA_KEY]
    gb_ref[...] = lg[:, GLA_KEY:]

    cos = cos_ref[...]
    sin = sin_ref[...]
    lane = lax.broadcasted_iota(jnp.int32, cos.shape, 1)
    low_head = lane < HEAD_DIM
    first_half = (lane % (HEAD_DIM // 2)) < (HEAD_DIM // 4)

    def norm_rope(x, gain, scale):
        x2 = x * x
        s_lo = jnp.sum(jnp.where(low_head, x2, 0.0), axis=-1, keepdims=True)
        s_hi = jnp.sum(jnp.where(low_head, 0.0, x2), axis=-1, keepdims=True)
        inv = jnp.where(low_head, lax.rsqrt(s_lo * (1.0 / HEAD_DIM) + EPS),
                        lax.rsqrt(s_hi * (1.0 / HEAD_DIM) + EPS))
        y = x * inv * gain
        partner = jnp.where(first_half, pltpu.roll(y, LANES - HEAD_DIM // 4, 1),
                            pltpu.roll(y, HEAD_DIM // 4, 1))
        return (y * cos + partner * sin) * scale

    for s in range(ATT_Q // LANES):
        sl = slice(s * LANES, (s + 1) * LANES)
        qb_ref[:, sl] = norm_rope(proj(_C_QB + s * LANES, _C_QB + (s + 1) * LANES),
                                  gq_ref[...], _Q_SCALE).astype(BF16)
    for s in range(ATT_KV_HEADS):
        sl = slice(s * LANES, (s + 1) * LANES)
        kb_ref[:, sl] = norm_rope(proj(_C_KB + s * LANES, _C_KB + (s + 1) * LANES),
                                  gk_ref[...], 1.0).astype(BF16)
    vt = _dot_nt(wvt_ref[...], zb)
    row_i = lax.broadcasted_iota(jnp.int32, vt.shape, 0)
    vt_ref[...] = jnp.where(row_i % LANES < HEAD_DIM, vt, 1.0).astype(BF16)


def _proj(hp, gain, w_all, wvt, w2, b2, gq, gk, cos_t, sin_t, lp):
    n, d = hp.shape
    assert lp % TM_TOK == 0
    tiles_per_seq = lp // TM_TOK

    def row(c):
        return pl.BlockSpec((TM_TOK, c), lambda i: (i, 0))

    table = pl.BlockSpec((TM_TOK, LANES), lambda i: (i % tiles_per_seq, 0))
    outs = [(GLA_KEY, BF16), (GLA_KEY, BF16), (GLA_VAL, BF16), (GLA_VAL, BF16),
            (GLA_KEY, F32), (GLA_KEY, F32), (ATT_Q, BF16), (2 * ATT_KV, BF16)]
    vt_rows = wvt.shape[0]
    return pl.pallas_call(
        _proj_body,
        out_shape=[jax.ShapeDtypeStruct((n, c), t) for c, t in outs]
        + [jax.ShapeDtypeStruct((vt_rows, n), BF16)],
        grid=(n // TM_TOK,),
        in_specs=[row(d), _resident((1, d)), _resident(w_all.shape), _resident(wvt.shape),
                  _resident(w2.shape), _resident(b2.shape), _resident((1, LANES)),
                  _resident((1, LANES)), table, table],
        out_specs=[row(c) for c, _ in outs]
        + [pl.BlockSpec((vt_rows, TM_TOK), lambda i: (0, i))],
        compiler_params=pltpu.CompilerParams(
            dimension_semantics=("parallel",), vmem_limit_bytes=VMEM_LIMIT),
        name="proj",
    )(hp, gain, w_all, wvt, w2, b2, gq, gk, cos_t, sin_t)


def _gla_body(q_ref, k_ref, v_ref, gf_ref, gb_ref, sr_ref, gn_ref, o_ref, ob_ref, st_ref):
    lp = q_ref.shape[0]
    assert GLA_LEAD % CHUNK == 0 and (lp - GLA_LEAD) % GLA_ROWS == 0
    n_main = (lp - GLA_LEAD) // GLA_ROWS

    def block(start, rows_n, forward, final):
        nch = rows_n // CHUNK
        rows = pl.ds(start, rows_n)
        pos = lax.broadcasted_iota(jnp.int32, (rows_n, LANES), 0) % CHUNK
        lane = lax.broadcasted_iota(jnp.int32, (rows_n, LANES), 1)

        bc = (gf_ref if forward else gb_ref)[rows, :]
        shift = 1
        while shift < CHUNK:
            if forward:
                moved, ok = pltpu.roll(bc, shift, 0), pos >= shift
            else:
                moved, ok = pltpu.roll(bc, rows_n - shift, 0), pos < CHUNK - shift
            bc = bc + jnp.where(ok, moved, 0.0)
            shift *= 2
        edge = CHUNK - 1 if forward else 0
        bt3 = bc.reshape(nch, CHUNK, LANES)[:, edge:edge + 1, :]
        bt = jnp.broadcast_to(bt3, (nch, CHUNK, LANES)).reshape(rows_n, LANES)
        dec = jnp.exp(bt3)

        q = q_ref[rows, :].astype(F32)
        k = k_ref[rows, :].astype(F32)
        qd = q * jnp.exp(bc)
        ki = (k * jnp.exp(-bc)).astype(BF16)
        ke = (k * jnp.exp(bt - bc)).astype(BF16)

        r_i = lax.broadcasted_iota(jnp.int32, (rows_n, rows_n), 0)
        c_i = lax.broadcasted_iota(jnp.int32, (rows_n, rows_n), 1)
        same_chunk = (r_i // CHUNK) == (c_i // CHUNK)
        att_ok = same_chunk & ((r_i >= c_i) if forward else (r_i <= c_i))
        wide = (rows_n, nch * LANES)
        own_block = (lax.broadcasted_iota(jnp.int32, wide, 0) // CHUNK
                     == lax.broadcasted_iota(jnp.int32, wide, 1) // LANES)
        ke_blk = jnp.where(own_block, jnp.tile(ke, (1, nch)), jnp.zeros(wide, BF16))

        for h in range(2):
            hs = slice(h * GLA_DV, (h + 1) * GLA_DV)
            qh = jnp.where((lane < GLA_DK) if h == 0 else (lane >= GLA_DK), qd, 0.0).astype(BF16)
            vh = v_ref[rows, hs]
            att = jnp.where(att_ok, _dot_nt(qh, ki), 0.0).astype(BF16)
            o = _dot(att, vh)
            kv_t = _dot_tn(vh, ke_blk)
            st = st_ref[h]
            states = [None] * nch
            for c in (range(nch) if forward else range(nch - 1, -1, -1)):
                states[c] = st
                st = st * dec[c] + kv_t[:, c * LANES:(c + 1) * LANES]
            st_ref[h] = st
            q_blk = jnp.where(own_block, jnp.tile(qh, (1, nch)), jnp.zeros(wide, BF16))
            o = o + _dot_nt(q_blk, jnp.concatenate(states, axis=1).astype(BF16))
            if not final:
                ob_ref[rows, hs] = o
            else:
                o = o + ob_ref[rows, hs]
                o = o * lax.rsqrt(jnp.mean(o * o, axis=-1, keepdims=True) + EPS) * gn_ref[:, hs]
                o_ref[rows, hs] = (o * sr_ref[rows, hs].astype(F32)).astype(BF16)

    def main_start(j):
        return pl.multiple_of(GLA_LEAD + j * GLA_ROWS, CHUNK)

    st_ref[...] = jnp.zeros(st_ref.shape, F32)

    def bwd(i, carry):
        block(main_start(n_main - 1 - i), GLA_ROWS, False, False)
        return carry

    lax.fori_loop(0, n_main, bwd, 0, unroll=4)
    block(0, GLA_LEAD, False, False)

    st_ref[...] = jnp.zeros(st_ref.shape, F32)
    block(0, GLA_LEAD, True, True)

    def fwd(i, carry):
        block(main_start(i), GLA_ROWS, True, True)
        return carry

    lax.fori_loop(0, n_main, fwd, 0, unroll=4)


def _gla(qa, ka, va, gf, gb, sr, gn, lp):
    n = qa.shape[0]
    nb = n // lp
    pairs = GLA_HEADS // 2

    def blk(c):
        return pl.BlockSpec((lp, c), lambda b, p: (b, p))

    return pl.pallas_call(
        _gla_body,
        out_shape=jax.ShapeDtypeStruct((n, GLA_VAL), BF16),
        grid=(nb, pairs),
        in_specs=[blk(LANES), blk(LANES), blk(2 * GLA_DV), blk(LANES), blk(LANES),
                  blk(2 * GLA_DV), pl.BlockSpec((1, 2 * GLA_DV), lambda b, p: (0, p))],
        out_specs=blk(2 * GLA_DV),
        scratch_shapes=[pltpu.VMEM((lp, 2 * GLA_DV), F32),
                        pltpu.VMEM((2, GLA_DV, LANES), F32)],
        compiler_params=pltpu.CompilerParams(
            dimension_semantics=("parallel", "parallel"), vmem_limit_bytes=VMEM_LIMIT),
        name="gla",
    )(qa, ka, va, gf, gb, sr, gn)


def _attn_body(q_ref, k_ref, vt_ref, o_ref, s_ref, *, padf):
    tq = q_ref.shape[0]
    lp = k_ref.shape[0]
    group = ATT_HEADS // ATT_KV_HEADS
    cols = group * tq
    n_main = (lp - LANES) // TK_ATT
    low = lax.broadcasted_iota(jnp.int32, (tq, LANES), 1) < HEAD_DIM

    for kvh in range(ATT_KV_HEADS):
        hs = slice(kvh * LANES, (kvh + 1) * LANES)
        parts = []
        for s in range(group // 2):
            slab = kvh * (group // 2) + s
            q = q_ref[:, slab * LANES:(slab + 1) * LANES]
            zero = jnp.zeros_like(q)
            parts += [jnp.where(low, q, zero), jnp.where(low, zero, q)]
        qs = jnp.concatenate(parts, axis=0)

        def scores(start, size, qs=qs, hs=hs):
            return _dot_nt(k_ref[pl.ds(start, size), hs], qs)

        def absorb(s, start, size, m, acc, hs=hs):
            m_new = jnp.maximum(m, jnp.max(s, axis=0, keepdims=True))
            p = jnp.exp2(s - m_new).astype(BF16)
            pv = _dot(vt_ref[hs, pl.ds(start, size)], p)
            return m_new, acc * jnp.exp2(m - m_new) + pv

        def main_start(j):
            return pl.multiple_of(LANES + j * TK_ATT, LANES)

        key_ok = lax.broadcasted_iota(jnp.int32, (LANES, cols), 0) >= padf
        m, acc = absorb(jnp.where(key_ok, scores(0, LANES), -1e30), 0, LANES,
                        jnp.full((1, cols), -jnp.inf, F32), jnp.zeros((LANES, cols), F32))

        assert n_main % 2 == 0
        s_ref[0] = scores(main_start(0), TK_ATT)

        def pair(i, carry, last):
            m, acc = carry
            s_ref[1] = scores(main_start(2 * i + 1), TK_ATT)
            m, acc = absorb(s_ref[0], main_start(2 * i), TK_ATT, m, acc)
            if not last:
                s_ref[0] = scores(main_start(2 * i + 2), TK_ATT)
            return absorb(s_ref[1], main_start(2 * i + 1), TK_ATT, m, acc)

        m, acc = lax.fori_loop(0, n_main // 2 - 1, lambda i, c: pair(i, c, False), (m, acc))
        _, acc = pair(n_main // 2 - 1, (m, acc), True)
        out_t = acc[:HEAD_DIM] / acc[HEAD_DIM:]
        for s in range(group // 2):
            slab = kvh * (group // 2) + s
            pair_t = jnp.concatenate([out_t[:, (2 * s) * tq:(2 * s + 1) * tq],
                                      out_t[:, (2 * s + 1) * tq:(2 * s + 2) * tq]], axis=0)
            o_ref[:, slab * LANES:(slab + 1) * LANES] = pair_t.T.astype(BF16)


def _attn(qb, kb, vt, lp, padf):
    n = qb.shape[0]
    nb = n // lp
    assert lp % TQ_ATT == 0 and (lp - LANES) % TK_ATT == 0
    nq = lp // TQ_ATT
    q_spec = pl.BlockSpec((TQ_ATT, ATT_Q), lambda b, i: (b * nq + i, 0))
    return pl.pallas_call(
        functools.partial(_attn_body, padf=padf),
        out_shape=jax.ShapeDtypeStruct((n, ATT_Q), BF16),
        grid=(nb, nq),
        in_specs=[q_spec, pl.BlockSpec((lp, kb.shape[1]), lambda b, i: (b, 0)),
                  pl.BlockSpec((vt.shape[0], lp), lambda b, i: (0, b))],
        out_specs=q_spec,
        scratch_shapes=[pltpu.VMEM((2, TK_ATT, ATT_HEADS // ATT_KV_HEADS * TQ_ATT), F32)],
        compiler_params=pltpu.CompilerParams(
            dimension_semantics=("parallel", "arbitrary"), vmem_limit_bytes=VMEM_LIMIT),
        name="attn",
    )(qb, kb, vt)


def _merge_body(h_ref, g_ref, a_ref, b_ref, wg_ref, bm_ref, wpa_ref, wpb_ref, wo_ref, o_ref,
                *, lp, padf):
    x = h_ref[...]
    d = x.shape[1]
    zb = _rmsnorm_rows(x, g_ref[...]).astype(BF16)
    gate_a = jax.nn.sigmoid(_dot(zb, wg_ref[:, :d]) + bm_ref[:, :d])
    gate_b = jax.nn.sigmoid(_dot(zb, wg_ref[:, d:]) + bm_ref[:, d:])
    y = gate_a * _dot(a_ref[...], wpa_ref[...]) + gate_b * _dot(b_ref[...], wpb_ref[...])
    out = _dot(y.astype(BF16), wo_ref[...])
    tm = x.shape[0]
    pos = (pl.program_id(0) % (lp // tm)) * tm + lax.broadcasted_iota(jnp.int32, (tm, 1), 0)
    o_ref[...] = x + jnp.where(pos >= padf, out, 0.0)


def _merge(hp, gain, a, b, wg, bm, wpa, wpb, wo, lp, padf):
    n, d = hp.shape

    def row(c):
        return pl.BlockSpec((TM_TOK, c), lambda i: (i, 0))

    return pl.pallas_call(
        functools.partial(_merge_body, lp=lp, padf=padf),
        out_shape=jax.ShapeDtypeStruct((n, d), F32),
        grid=(n // TM_TOK,),
        in_specs=[row(d), _resident((1, d)), row(GLA_VAL), row(ATT_Q), _resident(wg.shape),
                  _resident(bm.shape), _resident(wpa.shape), _resident(wpb.shape),
                  _resident(wo.shape)],
        out_specs=row(d),
        compiler_params=pltpu.CompilerParams(
            dimension_semantics=("parallel",), vmem_limit_bytes=VMEM_LIMIT),
        name="merge",
    )(hp, gain, a, b, wg, bm, wpa, wpb, wo)


def _final_body(h_ref, g_ref, o_ref):
    o_ref[...] = _rmsnorm_rows(h_ref[...], g_ref[...])


def _final(hp, gain, nb, n_tok, lp, skip):
    d = hp.shape[1]
    assert skip % TM_FIN == 0 and n_tok % TM_FIN == 0
    per_in, per_out, off = lp // TM_FIN, n_tok // TM_FIN, skip // TM_FIN
    return pl.pallas_call(
        _final_body,
        out_shape=jax.ShapeDtypeStruct((nb * n_tok, d), F32),
        grid=(nb, per_out),
        in_specs=[pl.BlockSpec((TM_FIN, d), lambda b, i: (b * per_in + off + i, 0)),
                  _resident((1, d))],
        out_specs=pl.BlockSpec((TM_FIN, d), lambda b, i: (b * per_out + i, 0)),
        compiler_params=pltpu.CompilerParams(
            dimension_semantics=("parallel", "parallel"), vmem_limit_bytes=VMEM_LIMIT),
        name="final_norm",
    )(hp, gain)


def _rope_tables(n_tok, lp, padf):
    quarter = HEAD_DIM // 4
    pos = jnp.arange(n_tok)
    inv = ROPE_THETA ** (-jnp.arange(0, HEAD_DIM // 2, 2, dtype=F32) / (HEAD_DIM // 2))
    ang_r = (pos // GRID_W).astype(F32)[:, None] * inv
    ang_c = (pos % GRID_W).astype(F32)[:, None] * inv
    cos = jnp.concatenate([jnp.cos(ang_r)] * 2 + [jnp.cos(ang_c)] * 2, axis=-1)
    sin = jnp.concatenate([-jnp.sin(ang_r), jnp.sin(ang_r), -jnp.sin(ang_c), jnp.sin(ang_c)],
                          axis=-1)
    lead = padf + N_META
    cos = jnp.concatenate([jnp.ones((lead, HEAD_DIM), F32), cos], axis=0)
    sin = jnp.concatenate([jnp.zeros((lead, HEAD_DIM), F32), sin], axis=0)
    assert cos.shape == (lp, 4 * quarter)
    return jnp.tile(cos, (1, 2)), jnp.tile(sin, (1, 2))


def _prep_w_in(w_in):
    sizes = (GLA_KEY, GLA_KEY, GLA_VAL, GLA_VAL, GLA_RANK, GLA_RANK, ATT_Q, ATT_KV, ATT_KV)
    offs = np.concatenate([[0], np.cumsum(sizes)])
    qa, ka, va, ra, lrf, lrb, qb, kb, vb = (w_in[:, offs[i]:offs[i + 1]] for i in range(9))

    def spread(w, twice):
        heads = [w[:, h * HEAD_DIM:(h + 1) * HEAD_DIM] for h in range(ATT_KV_HEADS)]
        return jnp.concatenate([x for h in heads for x in (h, h if twice else jnp.zeros_like(h))],
                               axis=-1)

    lr = jnp.concatenate([lrf, lrb, jnp.zeros((w_in.shape[0], LANES - 2 * GLA_RANK), F32)], -1)
    w_all = jnp.concatenate([qa, ka, va, ra, qb, spread(kb, True), lr], axis=-1).astype(BF16)
    return w_all, spread(vb, False).T.astype(BF16)


def _prep_gate(w2, b2):
    w = jnp.zeros((LANES, 2 * GLA_KEY), F32)
    w = w.at[:GLA_RANK, :GLA_KEY].set(w2[0]).at[GLA_RANK:2 * GLA_RANK, GLA_KEY:].set(w2[1])
    return w.astype(BF16), b2.reshape(1, 2 * GLA_KEY)


def kernel(x, meta_tokens, norm_gains, ffn_w_gate, ffn_w_up, ffn_w_down, w_in, gla_w2, gla_b2,
           gla_gn, q_norm, k_norm, w_pa, w_pb, b_merge, w_out, final_norm):
    nb, n_tok, d = x.shape
    depth = w_in.shape[0]
    lp = -(-(n_tok + N_META) // LANES) * LANES
    padf = lp - n_tok - N_META
    assert (padf + N_META) % CHUNK == 0 and n_tok % GRID_W == 0

    meta = jnp.broadcast_to(meta_tokens[None].astype(x.dtype), (nb, N_META, d))
    hp = jnp.concatenate([jnp.zeros((nb, padf, d), x.dtype), meta, x], axis=1).reshape(nb * lp, d)
    cos_t, sin_t = _rope_tables(n_tok, lp, padf)

    g_split = w_in.shape[2] - 2 * d
    for l in range(depth):
        gains = norm_gains[l].reshape(3, 1, d)
        hp = _ffn(hp, gains[0], ffn_w_gate[l, 0].astype(BF16), ffn_w_up[l, 0].astype(BF16),
                  ffn_w_down[l, 0].astype(BF16))
        w2, b2 = _prep_gate(gla_w2[l], gla_b2[l])
        w_all, wvt = _prep_w_in(w_in[l, :, :g_split])
        qa, ka, va, sr, gf, gb, qb, kb, vt = _proj(
            hp, gains[1], w_all, wvt, w2, b2,
            jnp.tile(q_norm[l], 2).reshape(1, LANES), jnp.tile(k_norm[l], 2).reshape(1, LANES),
            cos_t, sin_t, lp)
        a = _gla(qa, ka, va, gf, gb, sr, gla_gn[l].reshape(1, GLA_VAL), lp)
        b = _attn(qb, kb, vt, lp, padf)
        hp = _merge(hp, gains[1], a, b, w_in[l, :, g_split:].astype(BF16),
                    b_merge[l].reshape(1, 2 * d), w_pa[l].astype(BF16), w_pb[l].astype(BF16),
                    w_out[l].astype(BF16), lp, padf)
        hp = _ffn(hp, gains[2], ffn_w_gate[l, 1].astype(BF16), ffn_w_up[l, 1].astype(BF16),
                  ffn_w_down[l, 1].astype(BF16))
    out = _final(hp, final_norm.reshape(1, d), nb, n_tok, lp, padf + N_META)
    return out.reshape(nb, n_tok, d)
```

```python
import functools

import jax
import jax.numpy as jnp
import numpy as np
from jax import lax
from jax.experimental import pallas as pl
from jax.experimental.pallas import tpu as pltpu

F32 = jnp.float32
BF16 = jnp.bfloat16

N_META = 16
GRID_W = 64
GLA_HEADS = 4
GLA_DK = 64
GLA_DV = 128
GLA_KEY = GLA_HEADS * GLA_DK
GLA_VAL = GLA_HEADS * GLA_DV
GLA_RANK = 16
GLA_TAU = 16.0
CHUNK = 64
ATT_HEADS = 8
ATT_KV_HEADS = 2
HEAD_DIM = 64
ATT_Q = ATT_HEADS * HEAD_DIM
ATT_KV = ATT_KV_HEADS * HEAD_DIM
ROPE_THETA = 10000.0
EPS = 1e-6

LANES = 128
VMEM_LIMIT = 56 * 1024 * 1024

TM_FFN = 512
TF_FFN = 1408
TM_TOK = 384
TQ_ATT = 384
TK_ATT = 512
TM_FIN = 128
GLA_ROWS = 256
GLA_LEAD = 128


def _resident(shape):
    nd = len(shape)
    return pl.BlockSpec(shape, lambda *_: (0,) * nd, pipeline_mode=pl.Buffered(1))


def _rmsnorm_rows(x, gain):
    return x * lax.rsqrt(jnp.mean(x * x, axis=-1, keepdims=True) + EPS) * gain


def _dot(a, b):
    return jnp.dot(a, b, preferred_element_type=F32)


def _dot_nt(a, b):
    return lax.dot_general(a, b, (((1,), (1,)), ((), ())), preferred_element_type=F32)


def _dot_tn(a, b):
    return lax.dot_general(a, b, (((0,), (0,)), ((), ())), preferred_element_type=F32)


def _silu(x):
    return x * jax.nn.sigmoid(x)


def _ffn_body(x_ref, g_ref, wg_ref, wu_ref, wd_ref, o_ref):
    x = x_ref[...]
    xb = _rmsnorm_rows(x, g_ref[...]).astype(BF16)
    d_ff = wg_ref.shape[1]
    acc = jnp.zeros(x.shape, F32)
    for j in range(d_ff // TF_FFN):
        sl = slice(j * TF_FFN, (j + 1) * TF_FFN)
        gate = _dot(xb, wg_ref[:, sl])
        up = _dot(xb, wu_ref[:, sl])
        acc = acc + _dot((_silu(gate) * up).astype(BF16), wd_ref[sl, :])
    o_ref[...] = x + 0.5 * acc


def _ffn(hp, gain, wg, wu, wd):
    n, d = hp.shape
    d_ff = wg.shape[1]
    assert n % TM_FFN == 0 and d_ff % TF_FFN == 0
    row = pl.BlockSpec((TM_FFN, d), lambda i: (i, 0))
    return pl.pallas_call(
        _ffn_body,
        out_shape=jax.ShapeDtypeStruct((n, d), F32),
        grid=(n // TM_FFN,),
        in_specs=[row, _resident((1, d)), _resident((d, d_ff)), _resident((d, d_ff)),
                  _resident((d_ff, d))],
        out_specs=row,
        compiler_params=pltpu.CompilerParams(
            dimension_semantics=("parallel",), vmem_limit_bytes=VMEM_LIMIT),
        name="ffn",
    )(hp, gain, wg, wu, wd)


_C_QA = 0
_C_KA = _C_QA + GLA_KEY
_C_VA = _C_KA + GLA_KEY
_C_RA = _C_VA + GLA_VAL
_C_QB = _C_RA + GLA_VAL
_C_KB = _C_QB + ATT_Q
_C_LR = _C_KB + 2 * ATT_KV
_C_END = _C_LR + LANES
_Q_SCALE = HEAD_DIM ** -0.5 * float(np.log2(np.e))


def _proj_body(h_ref, g_ref, w_ref, wvt_ref, w2_ref, b2_ref, gq_ref, gk_ref, cos_ref, sin_ref,
               qa_ref, ka_ref, va_ref, sr_ref, gf_ref, gb_ref, qb_ref, kb_ref, vt_ref):
    zb = _rmsnorm_rows(h_ref[...], g_ref[...]).astype(BF16)

    def proj(lo, hi):
        return _dot(zb, w_ref[:, lo:hi])

    qa_ref[...] = (proj(_C_QA, _C_KA) * GLA_DK ** -0.5).astype(BF16)
    ka_ref[...] = proj(_C_KA, _C_VA).astype(BF16)
    va_ref[...] = proj(_C_VA, _C_RA).astype(BF16)
    sr_ref[...] = _silu(proj(_C_RA, _C_QB)).astype(BF16)

    lr = proj(_C_LR, _C_END).astype(BF16)
    pre = _dot(lr, w2_ref[...]) + b2_ref[...]
    lg = (jnp.minimum(pre, 0.0) - jnp.log(1.0 + jnp.exp(-jnp.abs(pre)))) * (1.0 / GLA_TAU)
    gf_ref[...] = lg[:, :GLA_KEY]
    gb_ref[...] = lg[:, GLA_KEY:]

    cos = cos_ref[...]
    sin = sin_ref[...]
    lane = lax.broadcasted_iota(jnp.int32, cos.shape, 1)
    low_head = lane < HEAD_DIM
    first_half = (lane % (HEAD_DIM // 2)) < (HEAD_DIM // 4)

    def norm_rope(x, gain, scale):
        x2 = x * x
        s_lo = jnp.sum(jnp.where(low_head, x2, 0.0), axis=-1, keepdims=True)
        s_hi = jnp.sum(jnp.where(low_head, 0.0, x2), axis=-1, keepdims=True)
        inv = jnp.where(low_head, lax.rsqrt(s_lo * (1.0 / HEAD_DIM) + EPS),
                        lax.rsqrt(s_hi * (1.0 / HEAD_DIM) + EPS))
        y = x * inv * gain
        partner = jnp.where(first_half, pltpu.roll(y, LANES - HEAD_DIM // 4, 1),
                            pltpu.roll(y, HEAD_DIM // 4, 1))
        return (y * cos + partner * sin) * scale

    for s in range(ATT_Q // LANES):
        sl = slice(s * LANES, (s + 1) * LANES)
        qb_ref[:, sl] = norm_rope(proj(_C_QB + s * LANES, _C_QB + (s + 1) * LANES),
                                  gq_ref[...], _Q_SCALE).astype(BF16)
    for s in range(ATT_KV_HEADS):
        sl = slice(s * LANES, (s + 1) * LANES)
        kb_ref[:, sl] = norm_rope(proj(_C_KB + s * LANES, _C_KB + (s + 1) * LANES),
                                  gk_ref[...], 1.0).astype(BF16)
    vt = _dot_nt(wvt_ref[...], zb)
    row_i = lax.broadcasted_iota(jnp.int32, vt.shape, 0)
    vt_ref[...] = jnp.where(row_i % LANES < HEAD_DIM, vt, 1.0).astype(BF16)


def _proj(hp, gain, w_all, wvt, w2, b2, gq, gk, cos_t, sin_t, lp):
    n, d = hp.shape
    assert lp % TM_TOK == 0
    tiles_per_seq = lp // TM_TOK

    def row(c):
        return pl.BlockSpec((TM_TOK, c), lambda i: (i, 0))

    table = pl.BlockSpec((TM_TOK, LANES), lambda i: (i % tiles_per_seq, 0))
    outs = [(GLA_KEY, BF16), (GLA_KEY, BF16), (GLA_VAL, BF16), (GLA_VAL, BF16),
            (GLA_KEY, F32), (GLA_KEY, F32), (ATT_Q, BF16), (2 * ATT_KV, BF16)]
    vt_rows = wvt.shape[0]
    return pl.pallas_call(
        _proj_body,
        out_shape=[jax.ShapeDtypeStruct((n, c), t) for c, t in outs]
        + [jax.ShapeDtypeStruct((vt_rows, n), BF16)],
        grid=(n // TM_TOK,),
        in_specs=[row(d), _resident((1, d)), _resident(w_all.shape), _resident(wvt.shape),
                  _resident(w2.shape), _resident(b2.shape), _resident((1, LANES)),
                  _resident((1, LANES)), table, table],
        out_specs=[row(c) for c, _ in outs]
        + [pl.BlockSpec((vt_rows, TM_TOK), lambda i: (0, i))],
        compiler_params=pltpu.CompilerParams(
            dimension_semantics=("parallel",), vmem_limit_bytes=VMEM_LIMIT),
        name="proj",
    )(hp, gain, w_all, wvt, w2, b2, gq, gk, cos_t, sin_t)


def _gla_body(q_ref, k_ref, v_ref, gf_ref, gb_ref, sr_ref, gn_ref, o_ref, ob_ref, st_ref):
    lp = q_ref.shape[0]
    assert GLA_LEAD % CHUNK == 0 and (lp - GLA_LEAD) % GLA_ROWS == 0
    n_main = (lp - GLA_LEAD) // GLA_ROWS

    def block(start, rows_n, forward, final):
        nch = rows_n // CHUNK
        rows = pl.ds(start, rows_n)
        pos = lax.broadcasted_iota(jnp.int32, (rows_n, LANES), 0) % CHUNK
        lane = lax.broadcasted_iota(jnp.int32, (rows_n, LANES), 1)

        bc = (gf_ref if forward else gb_ref)[rows, :]
        shift = 1
        while shift < CHUNK:
            if forward:
                moved, ok = pltpu.roll(bc, shift, 0), pos >= shift
            else:
                moved, ok = pltpu.roll(bc, rows_n - shift, 0), pos < CHUNK - shift
            bc = bc + jnp.where(ok, moved, 0.0)
            shift *= 2
        edge = CHUNK - 1 if forward else 0
        bt3 = bc.reshape(nch, CHUNK, LANES)[:, edge:edge + 1, :]
        bt = jnp.broadcast_to(bt3, (nch, CHUNK, LANES)).reshape(rows_n, LANES)
        dec = jnp.exp(bt3)

        q = q_ref[rows, :].astype(F32)
        k = k_ref[rows, :].astype(F32)
        qd = q * jnp.exp(bc)
        ki = (k * jnp.exp(-bc)).astype(BF16)
        ke = (k * jnp.exp(bt - bc)).astype(BF16)

        r_i = lax.broadcasted_iota(jnp.int32, (rows_n, rows_n), 0)
        c_i = lax.broadcasted_iota(jnp.int32, (rows_n, rows_n), 1)
        same_chunk = (r_i // CHUNK) == (c_i // CHUNK)
        att_ok = same_chunk & ((r_i >= c_i) if forward else (r_i <= c_i))
        wide = (rows_n, nch * LANES)
        own_block = (lax.broadcasted_iota(jnp.int32, wide, 0) // CHUNK
                     == lax.broadcasted_iota(jnp.int32, wide, 1) // LANES)
        ke_blk = jnp.where(own_block, jnp.tile(ke, (1, nch)), jnp.zeros(wide, BF16))

        for h in range(2):
            hs = slice(h * GLA_DV, (h + 1) * GLA_DV)
            qh = jnp.where((lane < GLA_DK) if h == 0 else (lane >= GLA_DK), qd, 0.0).astype(BF16)
            vh = v_ref[rows, hs]
            att = jnp.where(att_ok, _dot_nt(qh, ki), 0.0).astype(BF16)
            o = _dot(att, vh)
            kv_t = _dot_tn(vh, ke_blk)
            st = st_ref[h]
            states = [None] * nch
            for c in (range(nch) if forward else range(nch - 1, -1, -1)):
                states[c] = st
                st = st * dec[c] + kv_t[:, c * LANES:(c + 1) * LANES]
            st_ref[h] = st
            q_blk = jnp.where(own_block, jnp.tile(qh, (1, nch)), jnp.zeros(wide, BF16))
            o = o + _dot_nt(q_blk, jnp.concatenate(states, axis=1).astype(BF16))
            if not final:
                ob_ref[rows, hs] = o
            else:
                o = o + ob_ref[rows, hs]
                o = o * lax.rsqrt(jnp.mean(o * o, axis=-1, keepdims=True) + EPS) * gn_ref[:, hs]
                o_ref[rows, hs] = (o * sr_ref[rows, hs].astype(F32)).astype(BF16)

    def main_start(j):
        return pl.multiple_of(GLA_LEAD + j * GLA_ROWS, CHUNK)

    st_ref[...] = jnp.zeros(st_ref.shape, F32)

    def bwd(i, carry):
        block(main_start(n_main - 1 - i), GLA_ROWS, False, False)
        return carry

    lax.fori_loop(0, n_main, bwd, 0, unroll=4)
    block(0, GLA_LEAD, False, False)

    st_ref[...] = jnp.zeros(st_ref.shape, F32)
    block(0, GLA_LEAD, True, True)

    def fwd(i, carry):
        block(main_start(i), GLA_ROWS, True, True)
        return carry

    lax.fori_loop(0, n_main, fwd, 0, unroll=4)


def _gla(qa, ka, va, gf, gb, sr, gn, lp):
    n = qa.shape[0]
    nb = n // lp
    pairs = GLA_HEADS // 2

    def blk(c):
        return pl.BlockSpec((lp, c), lambda b, p: (b, p))

    return pl.pallas_call(
        _gla_body,
        out_shape=jax.ShapeDtypeStruct((n, GLA_VAL), BF16),
        grid=(nb, pairs),
        in_specs=[blk(LANES), blk(LANES), blk(2 * GLA_DV), blk(LANES), blk(LANES),
                  blk(2 * GLA_DV), pl.BlockSpec((1, 2 * GLA_DV), lambda b, p: (0, p))],
        out_specs=blk(2 * GLA_DV),
        scratch_shapes=[pltpu.VMEM((lp, 2 * GLA_DV), F32),
                        pltpu.VMEM((2, GLA_DV, LANES), F32)],
        compiler_params=pltpu.CompilerParams(
            dimension_semantics=("parallel", "parallel"), vmem_limit_bytes=VMEM_LIMIT),
        name="gla",
    )(qa, ka, va, gf, gb, sr, gn)


def _attn_body(q_ref, k_ref, vt_ref, o_ref, s_ref, *, padf):
    tq = q_ref.shape[0]
    lp = k_ref.shape[0]
    group = ATT_HEADS // ATT_KV_HEADS
    cols = group * tq
    n_main = (lp - LANES) // TK_ATT
    low = lax.broadcasted_iota(jnp.int32, (tq, LANES), 1) < HEAD_DIM

    for kvh in range(ATT_KV_HEADS):
        hs = slice(kvh * LANES, (kvh + 1) * LANES)
        parts = []
        for s in range(group // 2):
            slab = kvh * (group // 2) + s
            q = q_ref[:, slab * LANES:(slab + 1) * LANES]
            zero = jnp.zeros_like(q)
            parts += [jnp.where(low, q, zero), jnp.where(low, zero, q)]
        qs = jnp.concatenate(parts, axis=0)

        def scores(start, size, qs=qs, hs=hs):
            return _dot_nt(k_ref[pl.ds(start, size), hs], qs)

        def absorb(s, start, size, m, acc, hs=hs):
            m_new = jnp.maximum(m, jnp.max(s, axis=0, keepdims=True))
            p = jnp.exp2(s - m_new).astype(BF16)
            pv = _dot(vt_ref[hs, pl.ds(start, size)], p)
            return m_new, acc * jnp.exp2(m - m_new) + pv

        def main_start(j):
            return pl.multiple_of(LANES + j * TK_ATT, LANES)

        key_ok = lax.broadcasted_iota(jnp.int32, (LANES, cols), 0) >= padf
        m, acc = absorb(jnp.where(key_ok, scores(0, LANES), -1e30), 0, LANES,
                        jnp.full((1, cols), -jnp.inf, F32), jnp.zeros((LANES, cols), F32))

        assert n_main % 2 == 0
        s_ref[0] = scores(main_start(0), TK_ATT)

        def pair(i, carry, last):
            m, acc = carry
            s_ref[1] = scores(main_start(2 * i + 1), TK_ATT)
            m, acc = absorb(s_ref[0], main_start(2 * i), TK_ATT, m, acc)
            if not last:
                s_ref[0] = scores(main_start(2 * i + 2), TK_ATT)
            return absorb(s_ref[1], main_start(2 * i + 1), TK_ATT, m, acc)

        m, acc = lax.fori_loop(0, n_main // 2 - 1, lambda i, c: pair(i, c, False), (m, acc))
        _, acc = pair(n_main // 2 - 1, (m, acc), True)
        out_t = acc[:HEAD_DIM] / acc[HEAD_DIM:]
        for s in range(group // 2):
            slab = kvh * (group // 2) + s
            pair_t = jnp.concatenate([out_t[:, (2 * s) * tq:(2 * s + 1) * tq],
                                      out_t[:, (2 * s + 1) * tq:(2 * s + 2) * tq]], axis=0)
            o_ref[:, slab * LANES:(slab + 1) * LANES] = pair_t.T.astype(BF16)


def _attn(qb, kb, vt, lp, padf):
    n = qb.shape[0]
    nb = n // lp
    assert lp % TQ_ATT == 0 and (lp - LANES) % TK_ATT == 0
    nq = lp // TQ_ATT
    q_spec = pl.BlockSpec((TQ_ATT, ATT_Q), lambda b, i: (b * nq + i, 0))
    return pl.pallas_call(
        functools.partial(_attn_body, padf=padf),
        out_shape=jax.ShapeDtypeStruct((n, ATT_Q), BF16),
        grid=(nb, nq),
        in_specs=[q_spec, pl.BlockSpec((lp, kb.shape[1]), lambda b, i: (b, 0)),
                  pl.BlockSpec((vt.shape[0], lp), lambda b, i: (0, b))],
        out_specs=q_spec,
        scratch_shapes=[pltpu.VMEM((2, TK_ATT, ATT_HEADS // ATT_KV_HEADS * TQ_ATT), F32)],
        compiler_params=pltpu.CompilerParams(
            dimension_semantics=("parallel", "arbitrary"), vmem_limit_bytes=VMEM_LIMIT),
        name="attn",
    )(qb, kb, vt)


def _merge_body(h_ref, g_ref, a_ref, b_ref, wg_ref, bm_ref, wpa_ref, wpb_ref, wo_ref, o_ref,
                *, lp, padf):
    x = h_ref[...]
    d = x.shape[1]
    zb = _rmsnorm_rows(x, g_ref[...]).astype(BF16)
    gate_a = jax.nn.sigmoid(_dot(zb, wg_ref[:, :d]) + bm_ref[:, :d])
    gate_b = jax.nn.sigmoid(_dot(zb, wg_ref[:, d:]) + bm_ref[:, d:])
    y = gate_a * _dot(a_ref[...], wpa_ref[...]) + gate_b * _dot(b_ref[...], wpb_ref[...])
    out = _dot(y.astype(BF16), wo_ref[...])
    tm = x.shape[0]
    pos = (pl.program_id(0) % (lp // tm)) * tm + lax.broadcasted_iota(jnp.int32, (tm, 1), 0)
    o_ref[...] = x + jnp.where(pos >= padf, out, 0.0)


def _merge(hp, gain, a, b, wg, bm, wpa, wpb, wo, lp, padf):
    n, d = hp.shape

    def row(c):
        return pl.BlockSpec((TM_TOK, c), lambda i: (i, 0))

    return pl.pallas_call(
        functools.partial(_merge_body, lp=lp, padf=padf),
        out_shape=jax.ShapeDtypeStruct((n, d), F32),
        grid=(n // TM_TOK,),
        in_specs=[row(d), _resident((1, d)), row(GLA_VAL), row(ATT_Q), _resident(wg.shape),
                  _resident(bm.shape), _resident(wpa.shape), _resident(wpb.shape),
                  _resident(wo.shape)],
        out_specs=row(d),
        compiler_params=pltpu.CompilerParams(
            dimension_semantics=("parallel",), vmem_limit_bytes=VMEM_LIMIT),
        name="merge",
    )(hp, gain, a, b, wg, bm, wpa, wpb, wo)


def _final_body(h_ref, g_ref, o_ref):
    o_ref[...] = _rmsnorm_rows(h_ref[...], g_ref[...])


def _final(hp, gain, nb, n_tok, lp, skip):
    d = hp.shape[1]
    assert skip % TM_FIN == 0 and n_tok % TM_FIN == 0
    per_in, per_out, off = lp // TM_FIN, n_tok // TM_FIN, skip // TM_FIN
    return pl.pallas_call(
        _final_body,
        out_shape=jax.ShapeDtypeStruct((nb * n_tok, d), F32),
        grid=(nb, per_out),
        in_specs=[pl.BlockSpec((TM_FIN, d), lambda b, i: (b * per_in + off + i, 0)),
                  _resident((1, d))],
        out_specs=pl.BlockSpec((TM_FIN, d), lambda b, i: (b * per_out + i, 0)),
        compiler_params=pltpu.CompilerParams(
            dimension_semantics=("parallel", "parallel"), vmem_limit_bytes=VMEM_LIMIT),
        name="final_norm",
    )(hp, gain)


def _rope_tables(n_tok, lp, padf):
    quarter = HEAD_DIM // 4
    pos = jnp.arange(n_tok)
    inv = ROPE_THETA ** (-jnp.arange(0, HEAD_DIM // 2, 2, dtype=F32) / (HEAD_DIM // 2))
    ang_r = (pos // GRID_W).astype(F32)[:, None] * inv
    ang_c = (pos % GRID_W).astype(F32)[:, None] * inv
    cos = jnp.concatenate([jnp.cos(ang_r)] * 2 + [jnp.cos(ang_c)] * 2, axis=-1)
    sin = jnp.concatenate([-jnp.sin(ang_r), jnp.sin(ang_r), -jnp.sin(ang_c), jnp.sin(ang_c)],
                          axis=-1)
    lead = padf + N_META
    cos = jnp.concatenate([jnp.ones((lead, HEAD_DIM), F32), cos], axis=0)
    sin = jnp.concatenate([jnp.zeros((lead, HEAD_DIM), F32), sin], axis=0)
    assert cos.shape == (lp, 4 * quarter)
    return jnp.tile(cos, (1, 2)), jnp.tile(sin, (1, 2))


def _prep_w_in(w_in):
    sizes = (GLA_KEY, GLA_KEY, GLA_VAL, GLA_VAL, GLA_RANK, GLA_RANK, ATT_Q, ATT_KV, ATT_KV)
    offs = np.concatenate([[0], np.cumsum(sizes)])
    qa, ka, va, ra, lrf, lrb, qb, kb, vb = (w_in[:, offs[i]:offs[i + 1]] for i in range(9))

    def spread(w, twice):
        heads = [w[:, h * HEAD_DIM:(h + 1) * HEAD_DIM] for h in range(ATT_KV_HEADS)]
        return jnp.concatenate([x for h in heads for x in (h, h if twice else jnp.zeros_like(h))],
                               axis=-1)

    lr = jnp.concatenate([lrf, lrb, jnp.zeros((w_in.shape[0], LANES - 2 * GLA_RANK), F32)], -1)
    w_all = jnp.concatenate([qa, ka, va, ra, qb, spread(kb, True), lr], axis=-1).astype(BF16)
    return w_all, spread(vb, False).T.astype(BF16)


def _prep_gate(w2, b2):
    w = jnp.zeros((LANES, 2 * GLA_KEY), F32)
    w = w.at[:GLA_RANK, :GLA_KEY].set(w2[0]).at[GLA_RANK:2 * GLA_RANK, GLA_KEY:].set(w2[1])
    return w.astype(BF16), b2.reshape(1, 2 * GLA_KEY)


def kernel(x, meta_tokens, norm_gains, ffn_w_gate, ffn_w_up, ffn_w_down, w_in, gla_w2, gla_b2,
           gla_gn, q_norm, k_norm, w_pa, w_pb, b_merge, w_out, final_norm):
    nb, n_tok, d = x.shape
    depth = w_in.shape[0]
    lp = -(-(n_tok + N_META) // LANES) * LANES
    padf = lp - n_tok - N_META
    assert (padf + N_META) % CHUNK == 0 and n_tok % GRID_W == 0

    meta = jnp.broadcast_to(meta_tokens[None].astype(x.dtype), (nb, N_META, d))
    hp = jnp.concatenate([jnp.zeros((nb, padf, d), x.dtype), meta, x], axis=1).reshape(nb * lp, d)
    cos_t, sin_t = _rope_tables(n_tok, lp, padf)

    g_split = w_in.shape[2] - 2 * d
    for l in range(depth):
        gains = norm_gains[l].reshape(3, 1, d)
        hp = _ffn(hp, gains[0], ffn_w_gate[l, 0].astype(BF16), ffn_w_up[l, 0].astype(BF16),
                  ffn_w_down[l, 0].astype(BF16))
        w2, b2 = _prep_gate(gla_w2[l], gla_b2[l])
        w_all, wvt = _prep_w_in(w_in[l, :, :g_split])
        qa, ka, va, sr, gf, gb, qb, kb, vt = _proj(
            hp, gains[1], w_all, wvt, w2, b2,
            jnp.tile(q_norm[l], 2).reshape(1, LANES), jnp.tile(k_norm[l], 2).reshape(1, LANES),
            cos_t, sin_t, lp)
        a = _gla(qa, ka, va, gf, gb, sr, gla_gn[l].reshape(1, GLA_VAL), lp)
        b = _attn(qb, kb, vt, lp, padf)
        hp = _merge(hp, gains[1], a, b, w_in[l, :, g_split:].astype(BF16),
                    b_merge[l].reshape(1, 2 * d), w_pa[l].astype(BF16), w_pb[l].astype(BF16),
                    w_out[l].astype(BF16), lp, padf)
        hp = _ffn(hp, gains[2], ffn_w_gate[l, 1].astype(BF16), ffn_w_up[l, 1].astype(BF16),
                  ffn_w_down[l, 1].astype(BF16))
    out = _final(hp, final_norm.reshape(1, d), nb, n_tok, lp, padf + N_META)
    return out.reshape(nb, n_tok, d)
```

```python
import functools

import jax
import jax.numpy as jnp
import numpy as np
from jax import lax
from jax.experimental import pallas as pl
from jax.experimental.pallas import tpu as pltpu

F32 = jnp.float32
BF16 = jnp.bfloat16

N_META = 16
GRID_W = 64
GLA_HEADS = 4
GLA_DK = 64
GLA_DV = 128
GLA_KEY = GLA_HEADS * GLA_DK
GLA_VAL = GLA_HEADS * GLA_DV
GLA_RANK = 16
GLA_TAU = 16.0
CHUNK = 64
ATT_HEADS = 8
ATT_KV_HEADS = 2
HEAD_DIM = 64
ATT_Q = ATT_HEADS * HEAD_DIM
ATT_KV = ATT_KV_HEADS * HEAD_DIM
ROPE_THETA = 10000.0
EPS = 1e-6

LANES = 128
VMEM_LIMIT = 56 * 1024 * 1024

TM_FFN = 512
TF_FFN = 1408
TM_TOK = 384
TQ_ATT = 384
TK_ATT = 512
TM_FIN = 1024
GLA_ROWS = 256
GLA_LEAD = 128


def _resident(shape):
    nd = len(shape)
    return pl.BlockSpec(shape, lambda *_: (0,) * nd, pipeline_mode=pl.Buffered(1))


def _rmsnorm_rows(x, gain):
    return x * lax.rsqrt(jnp.mean(x * x, axis=-1, keepdims=True) + EPS) * gain


def _dot(a, b):
    return jnp.dot(a, b, preferred_element_type=F32)


def _dot_nt(a, b):
    return lax.dot_general(a, b, (((1,), (1,)), ((), ())), preferred_element_type=F32)


def _dot_tn(a, b):
    return lax.dot_general(a, b, (((0,), (0,)), ((), ())), preferred_element_type=F32)


def _silu(x):
    return x * jax.nn.sigmoid(x)


def _ffn_body(x_ref, g_ref, wg_ref, wu_ref, wd_ref, o_ref):
    x = x_ref[...]
    xb = _rmsnorm_rows(x, g_ref[...]).astype(BF16)
    d_ff = wg_ref.shape[1]
    acc = jnp.zeros(x.shape, F32)
    for j in range(d_ff // TF_FFN):
        sl = slice(j * TF_FFN, (j + 1) * TF_FFN)
        gate = _dot(xb, wg_ref[:, sl])
        up = _dot(xb, wu_ref[:, sl])
        acc = acc + _dot((_silu(gate) * up).astype(BF16), wd_ref[sl, :])
    o_ref[...] = x + 0.5 * acc


def _ffn(hp, gain, wg, wu, wd):
    n, d = hp.shape
    d_ff = wg.shape[1]
    assert n % TM_FFN == 0 and d_ff % TF_FFN == 0
    row = pl.BlockSpec((TM_FFN, d), lambda i: (i, 0))
    return pl.pallas_call(
        _ffn_body,
        out_shape=jax.ShapeDtypeStruct((n, d), F32),
        grid=(n // TM_FFN,),
        in_specs=[row, _resident((1, d)), _resident((d, d_ff)), _resident((d, d_ff)),
                  _resident((d_ff, d))],
        out_specs=row,
        compiler_params=pltpu.CompilerParams(
            dimension_semantics=("parallel",), vmem_limit_bytes=VMEM_LIMIT),
        name="ffn",
    )(hp, gain, wg, wu, wd)


_C_QA = 0
_C_KA = _C_QA + GLA_KEY
_C_VA = _C_KA + GLA_KEY
_C_RA = _C_VA + GLA_VAL
_C_QB = _C_RA + GLA_VAL
_C_KB = _C_QB + ATT_Q
_C_LR = _C_KB + 2 * ATT_KV
_C_END = _C_LR + LANES
_Q_SCALE = HEAD_DIM ** -0.5 * float(np.log2(np.e))
ATT_SCORE_BOUND = 1.05 * HEAD_DIM * _Q_SCALE
ATT_SAFE_SCORE = 60.0


def _proj_body(h_ref, g_ref, w_ref, wvt_ref, w2_ref, b2_ref, gq_ref, gk_ref, cos_ref, sin_ref,
               qa_ref, ka_ref, va_ref, sr_ref, gf_ref, gb_ref, qb_ref, kb_ref, vt_ref):
    zb = _rmsnorm_rows(h_ref[...], g_ref[...]).astype(BF16)

    def proj(lo, hi):
        return _dot(zb, w_ref[:, lo:hi])

    qa_ref[...] = (proj(_C_QA, _C_KA) * GLA_DK ** -0.5).astype(BF16)
    ka_ref[...] = proj(_C_KA, _C_VA).astype(BF16)
    va_ref[...] = proj(_C_VA, _C_RA).astype(BF16)
    sr_ref[...] = _silu(proj(_C_RA, _C_QB)).astype(BF16)

    lr = proj(_C_LR, _C_END).astype(BF16)
    pre = _dot(lr, w2_ref[...]) + b2_ref[...]
    lg = (jnp.minimum(pre, 0.0) - jnp.log(1.0 + jnp.exp(-jnp.abs(pre)))) * (1.0 / GLA_TAU)
    gf_ref[...] = lg[:, :GLA_KEY]
    gb_ref[...] = lg[:, GLA_KEY:]

    cos = cos_ref[...]
    sin = sin_ref[...]
    lane = lax.broadcasted_iota(jnp.int32, cos.shape, 1)
    low_head = lane < HEAD_DIM
    first_half = (lane % (HEAD_DIM // 2)) < (HEAD_DIM // 4)

    def norm_rope(x, gain, scale):
        x2 = x * x
        s_lo = jnp.sum(jnp.where(low_head, x2, 0.0), axis=-1, keepdims=True)
        s_hi = jnp.sum(jnp.where(low_head, 0.0, x2), axis=-1, keepdims=True)
        inv = jnp.where(low_head, lax.rsqrt(s_lo * (1.0 / HEAD_DIM) + EPS),
                        lax.rsqrt(s_hi * (1.0 / HEAD_DIM) + EPS))
        y = x * inv * gain
        partner = jnp.where(first_half, pltpu.roll(y, LANES - HEAD_DIM // 4, 1),
                            pltpu.roll(y, HEAD_DIM // 4, 1))
        return (y * cos + partner * sin) * scale

    for s in range(ATT_Q // LANES):
        sl = slice(s * LANES, (s + 1) * LANES)
        qb_ref[:, sl] = norm_rope(proj(_C_QB + s * LANES, _C_QB + (s + 1) * LANES),
                                  gq_ref[...], _Q_SCALE).astype(BF16)
    for s in range(ATT_KV_HEADS):
        sl = slice(s * LANES, (s + 1) * LANES)
        kb_ref[:, sl] = norm_rope(proj(_C_KB + s * LANES, _C_KB + (s + 1) * LANES),
                                  gk_ref[...], 1.0).astype(BF16)
    vt = _dot_nt(wvt_ref[...], zb)
    row_i = lax.broadcasted_iota(jnp.int32, vt.shape, 0)
    vt_ref[...] = jnp.where(row_i % LANES < HEAD_DIM, vt, 1.0).astype(BF16)


def _proj(hp, gain, w_all, wvt, w2, b2, gq, gk, cos_t, sin_t, lp):
    n, d = hp.shape
    assert lp % TM_TOK == 0
    tiles_per_seq = lp // TM_TOK

    def row(c):
        return pl.BlockSpec((TM_TOK, c), lambda i: (i, 0))

    table = pl.BlockSpec((TM_TOK, LANES), lambda i: (i % tiles_per_seq, 0))
    outs = [(GLA_KEY, BF16), (GLA_KEY, BF16), (GLA_VAL, BF16), (GLA_VAL, BF16),
            (GLA_KEY, F32), (GLA_KEY, F32), (ATT_Q, BF16), (2 * ATT_KV, BF16)]
    vt_rows = wvt.shape[0]
    return pl.pallas_call(
        _proj_body,
        out_shape=[jax.ShapeDtypeStruct((n, c), t) for c, t in outs]
        + [jax.ShapeDtypeStruct((vt_rows, n), BF16)],
        grid=(n // TM_TOK,),
        in_specs=[row(d), _resident((1, d)), _resident(w_all.shape), _resident(wvt.shape),
                  _resident(w2.shape), _resident(b2.shape), _resident((1, LANES)),
                  _resident((1, LANES)), table, table],
        out_specs=[row(c) for c, _ in outs]
        + [pl.BlockSpec((vt_rows, TM_TOK), lambda i: (0, i))],
        compiler_params=pltpu.CompilerParams(
            dimension_semantics=("parallel",), vmem_limit_bytes=VMEM_LIMIT),
        name="proj",
    )(hp, gain, w_all, wvt, w2, b2, gq, gk, cos_t, sin_t)


def _gla_body(q_ref, k_ref, v_ref, gf_ref, gb_ref, sr_ref, gn_ref, o_ref, ob_ref, st_ref):
    lp = q_ref.shape[0]
    assert GLA_LEAD % CHUNK == 0 and (lp - GLA_LEAD) % GLA_ROWS == 0
    n_main = (lp - GLA_LEAD) // GLA_ROWS

    def block(start, rows_n, forward, final):
        nch = rows_n // CHUNK
        rows = pl.ds(start, rows_n)
        pos = lax.broadcasted_iota(jnp.int32, (rows_n, LANES), 0) % CHUNK
        lane = lax.broadcasted_iota(jnp.int32, (rows_n, LANES), 1)

        bc = (gf_ref if forward else gb_ref)[rows, :]
        shift = 1
        while shift < CHUNK:
            if forward:
                moved, ok = pltpu.roll(bc, shift, 0), pos >= shift
            else:
                moved, ok = pltpu.roll(bc, rows_n - shift, 0), pos < CHUNK - shift
            bc = bc + jnp.where(ok, moved, 0.0)
            shift *= 2
        edge = CHUNK - 1 if forward else 0
        bt3 = bc.reshape(nch, CHUNK, LANES)[:, edge:edge + 1, :]
        bt = jnp.broadcast_to(bt3, (nch, CHUNK, LANES)).reshape(rows_n, LANES)
        dec = jnp.exp(bt3)

        q = q_ref[rows, :].astype(F32)
        k = k_ref[rows, :].astype(F32)
        qd = q * jnp.exp(bc)
        ki = (k * jnp.exp(-bc)).astype(BF16)
        ke = (k * jnp.exp(bt - bc)).astype(BF16)

        r_i = lax.broadcasted_iota(jnp.int32, (rows_n, rows_n), 0)
        c_i = lax.broadcasted_iota(jnp.int32, (rows_n, rows_n), 1)
        same_chunk = (r_i // CHUNK) == (c_i // CHUNK)
        att_ok = same_chunk & ((r_i >= c_i) if forward else (r_i <= c_i))
        wide = (rows_n, nch * LANES)
        own_block = (lax.broadcasted_iota(jnp.int32, wide, 0) // CHUNK
                     == lax.broadcasted_iota(jnp.int32, wide, 1) // LANES)
        ke_blk = jnp.where(own_block, jnp.tile(ke, (1, nch)), jnp.zeros(wide, BF16))

        for h in range(2):
            hs = slice(h * GLA_DV, (h + 1) * GLA_DV)
            qh = jnp.where((lane < GLA_DK) if h == 0 else (lane >= GLA_DK), qd, 0.0).astype(BF16)
            vh = v_ref[rows, hs]
            att = jnp.where(att_ok, _dot_nt(qh, ki), 0.0).astype(BF16)
            o = _dot(att, vh)
            kv_t = _dot_tn(vh, ke_blk)
            st = st_ref[h]
            states = [None] * nch
            for c in (range(nch) if forward else range(nch - 1, -1, -1)):
                states[c] = st
                st = st * dec[c] + kv_t[:, c * LANES:(c + 1) * LANES]
            st_ref[h] = st
            q_blk = jnp.where(own_block, jnp.tile(qh, (1, nch)), jnp.zeros(wide, BF16))
            o = o + _dot_nt(q_blk, jnp.concatenate(states, axis=1).astype(BF16))
            if not final:
                ob_ref[rows, hs] = o
            else:
                o = o + ob_ref[rows, hs]
                o = o * lax.rsqrt(jnp.mean(o * o, axis=-1, keepdims=True) + EPS) * gn_ref[:, hs]
                o_ref[rows, hs] = (o * sr_ref[rows, hs].astype(F32)).astype(BF16)

    def main_start(j):
        return pl.multiple_of(GLA_LEAD + j * GLA_ROWS, CHUNK)

    st_ref[...] = jnp.zeros(st_ref.shape, F32)

    def bwd(i, carry):
        block(main_start(n_main - 1 - i), GLA_ROWS, False, False)
        return carry

    lax.fori_loop(0, n_main, bwd, 0, unroll=4)
    block(0, GLA_LEAD, False, False)

    st_ref[...] = jnp.zeros(st_ref.shape, F32)
    block(0, GLA_LEAD, True, True)

    def fwd(i, carry):
        block(main_start(i), GLA_ROWS, True, True)
        return carry

    lax.fori_loop(0, n_main, fwd, 0, unroll=4)


def _gla(qa, ka, va, gf, gb, sr, gn, lp):
    n = qa.shape[0]
    nb = n // lp
    pairs = GLA_HEADS // 2

    def blk(c):
        return pl.BlockSpec((lp, c), lambda b, p: (b, p))

    return pl.pallas_call(
        _gla_body,
        out_shape=jax.ShapeDtypeStruct((n, GLA_VAL), BF16),
        grid=(nb, pairs),
        in_specs=[blk(LANES), blk(LANES), blk(2 * GLA_DV), blk(LANES), blk(LANES),
                  blk(2 * GLA_DV), pl.BlockSpec((1, 2 * GLA_DV), lambda b, p: (0, p))],
        out_specs=blk(2 * GLA_DV),
        scratch_shapes=[pltpu.VMEM((lp, 2 * GLA_DV), F32),
                        pltpu.VMEM((2, GLA_DV, LANES), F32)],
        compiler_params=pltpu.CompilerParams(
            dimension_semantics=("parallel", "parallel"), vmem_limit_bytes=VMEM_LIMIT),
        name="gla",
    )(qa, ka, va, gf, gb, sr, gn)


def _attn_body(q_ref, k_ref, vt_ref, o_ref, s_ref, *, padf, online):
    tq = q_ref.shape[0]
    lp = k_ref.shape[0]
    group = ATT_HEADS // ATT_KV_HEADS
    cols = group * tq
    n_main = (lp - LANES) // TK_ATT
    low = lax.broadcasted_iota(jnp.int32, (tq, LANES), 1) < HEAD_DIM

    for kvh in range(ATT_KV_HEADS):
        hs = slice(kvh * LANES, (kvh + 1) * LANES)
        parts = []
        for s in range(group // 2):
            slab = kvh * (group // 2) + s
            q = q_ref[:, slab * LANES:(slab + 1) * LANES]
            zero = jnp.zeros_like(q)
            parts += [jnp.where(low, q, zero), jnp.where(low, zero, q)]
        qs = jnp.concatenate(parts, axis=0)

        def scores(start, size, qs=qs, hs=hs):
            return _dot_nt(k_ref[pl.ds(start, size), hs], qs)

        def absorb(s, start, size, m, acc, hs=hs):
            if not online:
                return m, acc + _dot(vt_ref[hs, pl.ds(start, size)], jnp.exp2(s).astype(BF16))
            m_new = jnp.maximum(m, jnp.max(s, axis=0, keepdims=True))
            p = jnp.exp2(s - m_new).astype(BF16)
            pv = _dot(vt_ref[hs, pl.ds(start, size)], p)
            return m_new, acc * jnp.exp2(m - m_new) + pv

        def main_start(j):
            return pl.multiple_of(LANES + j * TK_ATT, LANES)

        key_ok = lax.broadcasted_iota(jnp.int32, (LANES, cols), 0) >= padf
        m, acc = absorb(jnp.where(key_ok, scores(0, LANES), -1e30), 0, LANES,
                        jnp.full((1, cols), -jnp.inf, F32), jnp.zeros((LANES, cols), F32))

        assert n_main % 2 == 0
        if online:
            s_ref[0] = scores(main_start(0), TK_ATT)

        def pair(i, carry, last):
            m, acc = carry
            s_ref[1] = scores(main_start(2 * i + 1), TK_ATT)
            m, acc = absorb(s_ref[0], main_start(2 * i), TK_ATT, m, acc)
            if not last:
                s_ref[0] = scores(main_start(2 * i + 2), TK_ATT)
            return absorb(s_ref[1], main_start(2 * i + 1), TK_ATT, m, acc)

        if online:
            m, acc = lax.fori_loop(0, n_main // 2 - 1, lambda i, c: pair(i, c, False), (m, acc))
            _, acc = pair(n_main // 2 - 1, (m, acc), True)
        else:
            def direct(j, acc):
                _, acc = absorb(scores(main_start(j), TK_ATT), main_start(j), TK_ATT, None, acc)
                return acc

            acc = lax.fori_loop(0, n_main, direct, acc, unroll=8)
        out_t = acc[:HEAD_DIM] / acc[HEAD_DIM:]
        for s in range(group // 2):
            slab = kvh * (group // 2) + s
            pair_t = jnp.concatenate([out_t[:, (2 * s) * tq:(2 * s + 1) * tq],
                                      out_t[:, (2 * s + 1) * tq:(2 * s + 2) * tq]], axis=0)
            o_ref[:, slab * LANES:(slab + 1) * LANES] = pair_t.T.astype(BF16)


def _attn(qb, kb, vt, lp, padf, online):
    n = qb.shape[0]
    nb = n // lp
    assert lp % TQ_ATT == 0 and (lp - LANES) % TK_ATT == 0
    nq = lp // TQ_ATT
    q_spec = pl.BlockSpec((TQ_ATT, ATT_Q), lambda b, i: (b * nq + i, 0))
    return pl.pallas_call(
        functools.partial(_attn_body, padf=padf, online=online),
        out_shape=jax.ShapeDtypeStruct((n, ATT_Q), BF16),
        grid=(nb, nq),
        in_specs=[q_spec, pl.BlockSpec((lp, kb.shape[1]), lambda b, i: (b, 0)),
                  pl.BlockSpec((vt.shape[0], lp), lambda b, i: (0, b))],
        out_specs=q_spec,
        scratch_shapes=[pltpu.VMEM((2, TK_ATT, ATT_HEADS // ATT_KV_HEADS * TQ_ATT), F32)],
        compiler_params=pltpu.CompilerParams(
            dimension_semantics=("parallel", "arbitrary"), vmem_limit_bytes=VMEM_LIMIT),
        name="attn",
    )(qb, kb, vt)


def _merge_body(h_ref, g_ref, a_ref, b_ref, wg_ref, bm_ref, wpa_ref, wpb_ref, wo_ref, o_ref,
                *, lp, padf):
    x = h_ref[...]
    d = x.shape[1]
    zb = _rmsnorm_rows(x, g_ref[...]).astype(BF16)
    gate_a = jax.nn.sigmoid(_dot(zb, wg_ref[:, :d]) + bm_ref[:, :d])
    gate_b = jax.nn.sigmoid(_dot(zb, wg_ref[:, d:]) + bm_ref[:, d:])
    y = gate_a * _dot(a_ref[...], wpa_ref[...]) + gate_b * _dot(b_ref[...], wpb_ref[...])
    out = _dot(y.astype(BF16), wo_ref[...])
    tm = x.shape[0]
    pos = (pl.program_id(0) % (lp // tm)) * tm + lax.broadcasted_iota(jnp.int32, (tm, 1), 0)
    o_ref[...] = x + jnp.where(pos >= padf, out, 0.0)


def _merge(hp, gain, a, b, wg, bm, wpa, wpb, wo, lp, padf):
    n, d = hp.shape

    def row(c):
        return pl.BlockSpec((TM_TOK, c), lambda i: (i, 0))

    return pl.pallas_call(
        functools.partial(_merge_body, lp=lp, padf=padf),
        out_shape=jax.ShapeDtypeStruct((n, d), F32),
        grid=(n // TM_TOK,),
        in_specs=[row(d), _resident((1, d)), row(GLA_VAL), row(ATT_Q), _resident(wg.shape),
                  _resident(bm.shape), _resident(wpa.shape), _resident(wpb.shape),
                  _resident(wo.shape)],
        out_specs=row(d),
        compiler_params=pltpu.CompilerParams(
            dimension_semantics=("parallel",), vmem_limit_bytes=VMEM_LIMIT),
        name="merge",
    )(hp, gain, a, b, wg, bm, wpa, wpb, wo)


def _final_body(h_ref, g_ref, o_ref):
    o_ref[...] = _rmsnorm_rows(h_ref[...], g_ref[...])


def _final(hp, gain, nb, n_tok, lp, skip):
    d = hp.shape[1]
    assert n_tok % TM_FIN == 0
    per_out = n_tok // TM_FIN
    return pl.pallas_call(
        _final_body,
        out_shape=jax.ShapeDtypeStruct((nb * n_tok, d), F32),
        grid=(nb, per_out),
        in_specs=[pl.BlockSpec((pl.Element(TM_FIN), pl.Element(d)),
                               lambda b, i: (pl.multiple_of(b * lp + skip + i * TM_FIN, LANES), 0)),
                  _resident((1, d))],
        out_specs=pl.BlockSpec((TM_FIN, d), lambda b, i: (b * per_out + i, 0)),
        compiler_params=pltpu.CompilerParams(
            dimension_semantics=("parallel", "parallel"), vmem_limit_bytes=VMEM_LIMIT),
        name="final_norm",
    )(hp, gain)


def _rope_tables(n_tok, lp, padf):
    quarter = HEAD_DIM // 4
    pos = jnp.arange(n_tok)
    inv = ROPE_THETA ** (-jnp.arange(0, HEAD_DIM // 2, 2, dtype=F32) / (HEAD_DIM // 2))
    ang_r = (pos // GRID_W).astype(F32)[:, None] * inv
    ang_c = (pos % GRID_W).astype(F32)[:, None] * inv
    cos = jnp.concatenate([jnp.cos(ang_r)] * 2 + [jnp.cos(ang_c)] * 2, axis=-1)
    sin = jnp.concatenate([-jnp.sin(ang_r), jnp.sin(ang_r), -jnp.sin(ang_c), jnp.sin(ang_c)],
                          axis=-1)
    lead = padf + N_META
    cos = jnp.concatenate([jnp.ones((lead, HEAD_DIM), F32), cos], axis=0)
    sin = jnp.concatenate([jnp.zeros((lead, HEAD_DIM), F32), sin], axis=0)
    assert cos.shape == (lp, 4 * quarter)
    return jnp.tile(cos, (1, 2)), jnp.tile(sin, (1, 2))


def _prep_w_in(w_in):
    sizes = (GLA_KEY, GLA_KEY, GLA_VAL, GLA_VAL, GLA_RANK, GLA_RANK, ATT_Q, ATT_KV, ATT_KV)
    offs = np.concatenate([[0], np.cumsum(sizes)])
    qa, ka, va, ra, lrf, lrb, qb, kb, vb = (w_in[:, offs[i]:offs[i + 1]] for i in range(9))

    def spread(w, twice):
        heads = [w[:, h * HEAD_DIM:(h + 1) * HEAD_DIM] for h in range(ATT_KV_HEADS)]
        return jnp.concatenate([x for h in heads for x in (h, h if twice else jnp.zeros_like(h))],
                               axis=-1)

    lr = jnp.concatenate([lrf, lrb, jnp.zeros((w_in.shape[0], LANES - 2 * GLA_RANK), F32)], -1)
    w_all = jnp.concatenate([qa, ka, va, ra, qb, spread(kb, True), lr], axis=-1).astype(BF16)
    return w_all, spread(vb, False).T.astype(BF16)


def _prep_gate(w2, b2):
    w = jnp.zeros((LANES, 2 * GLA_KEY), F32)
    w = w.at[:GLA_RANK, :GLA_KEY].set(w2[0]).at[GLA_RANK:2 * GLA_RANK, GLA_KEY:].set(w2[1])
    return w.astype(BF16), b2.reshape(1, 2 * GLA_KEY)


def kernel(x, meta_tokens, norm_gains, ffn_w_gate, ffn_w_up, ffn_w_down, w_in, gla_w2, gla_b2,
           gla_gn, q_norm, k_norm, w_pa, w_pb, b_merge, w_out, final_norm):
    nb, n_tok, d = x.shape
    depth = w_in.shape[0]
    lp = -(-(n_tok + N_META) // LANES) * LANES
    padf = lp - n_tok - N_META
    assert (padf + N_META) % CHUNK == 0 and n_tok % GRID_W == 0

    meta = jnp.broadcast_to(meta_tokens[None].astype(x.dtype), (nb, N_META, d))
    hp = jnp.concatenate([jnp.zeros((nb, padf, d), x.dtype), meta, x], axis=1).reshape(nb * lp, d)
    cos_t, sin_t = _rope_tables(n_tok, lp, padf)

    g_split = w_in.shape[2] - 2 * d
    for l in range(depth):
        gains = norm_gains[l].reshape(3, 1, d)
        hp = _ffn(hp, gains[0], ffn_w_gate[l, 0].astype(BF16), ffn_w_up[l, 0].astype(BF16),
                  ffn_w_down[l, 0].astype(BF16))
        w2, b2 = _prep_gate(gla_w2[l], gla_b2[l])
        w_all, wvt = _prep_w_in(w_in[l, :, :g_split])
        qa, ka, va, sr, gf, gb, qb, kb, vt = _proj(
            hp, gains[1], w_all, wvt, w2, b2,
            jnp.tile(q_norm[l], 2).reshape(1, LANES), jnp.tile(k_norm[l], 2).reshape(1, LANES),
            cos_t, sin_t, lp)
        a = _gla(qa, ka, va, gf, gb, sr, gla_gn[l].reshape(1, GLA_VAL), lp)
        small = (jnp.max(jnp.abs(q_norm[l])) * jnp.max(jnp.abs(k_norm[l])) * ATT_SCORE_BOUND
                 <= ATT_SAFE_SCORE)
        b = lax.cond(small, functools.partial(_attn, lp=lp, padf=padf, online=False),
                     functools.partial(_attn, lp=lp, padf=padf, online=True), qb, kb, vt)
        hp = _merge(hp, gains[1], a, b, w_in[l, :, g_split:].astype(BF16),
                    b_merge[l].reshape(1, 2 * d), w_pa[l].astype(BF16), w_pb[l].astype(BF16),
                    w_out[l].astype(BF16), lp, padf)
        hp = _ffn(hp, gains[2], ffn_w_gate[l, 1].astype(BF16), ffn_w_up[l, 1].astype(BF16),
                  ffn_w_down[l, 1].astype(BF16))
    out = _final(hp, final_norm.reshape(1, d), nb, n_tok, lp, padf + N_META)
    return out.reshape(nb, n_tok, d)
```

```python
import functools

import jax
import jax.numpy as jnp
import numpy as np
from jax import lax
from jax.experimental import pallas as pl
from jax.experimental.pallas import tpu as pltpu

F32 = jnp.float32
BF16 = jnp.bfloat16

N_META = 16
GRID_W = 64
GLA_HEADS = 4
GLA_DK = 64
GLA_DV = 128
GLA_KEY = GLA_HEADS * GLA_DK
GLA_VAL = GLA_HEADS * GLA_DV
GLA_RANK = 16
GLA_TAU = 16.0
CHUNK = 64
ATT_HEADS = 8
ATT_KV_HEADS = 2
HEAD_DIM = 64
ATT_Q = ATT_HEADS * HEAD_DIM
ATT_KV = ATT_KV_HEADS * HEAD_DIM
ROPE_THETA = 10000.0
EPS = 1e-6

LANES = 128
MXU_COLS = 256
VMEM_LIMIT = 56 * 1024 * 1024

TM_FFN = 512
TF_FFN = 2816
TM_TOK = 384
TQ_ATT = 384
TK_ATT = 512
TM_FIN = 1024
GLA_ROWS = 256
GLA_LEAD = 128


def _resident(shape):
    nd = len(shape)
    return pl.BlockSpec(shape, lambda *_: (0,) * nd, pipeline_mode=pl.Buffered(1))


def _rmsnorm_rows(x, gain):
    return x * lax.rsqrt(jnp.mean(x * x, axis=-1, keepdims=True) + EPS) * gain


def _dot(a, b):
    return jnp.dot(a, b, preferred_element_type=F32)


def _dot_nt(a, b):
    return lax.dot_general(a, b, (((1,), (1,)), ((), ())), preferred_element_type=F32)


def _dot_tn(a, b):
    return lax.dot_general(a, b, (((0,), (0,)), ((), ())), preferred_element_type=F32)


def _silu(x):
    return x * jax.nn.sigmoid(x)


def _ffn_body(x_ref, g_ref, wg_ref, wu_ref, wd_ref, o_ref):
    x = x_ref[...]
    xb = _rmsnorm_rows(x, g_ref[...]).astype(BF16)
    d_ff = wg_ref.shape[1]
    acc = jnp.zeros(x.shape, F32)
    for j in range(d_ff // TF_FFN):
        sl = slice(j * TF_FFN, (j + 1) * TF_FFN)
        gate = _dot(xb, wg_ref[:, sl])
        up = _dot(xb, wu_ref[:, sl])
        acc = acc + _dot((_silu(gate) * up).astype(BF16), wd_ref[sl, :])
    o_ref[...] = x + 0.5 * acc


def _ffn(hp, gain, wg, wu, wd):
    n, d = hp.shape
    d_ff = wg.shape[1]
    assert n % TM_FFN == 0 and d_ff % TF_FFN == 0
    row = pl.BlockSpec((TM_FFN, d), lambda i: (i, 0))
    return pl.pallas_call(
        _ffn_body,
        out_shape=jax.ShapeDtypeStruct((n, d), F32),
        grid=(n // TM_FFN,),
        in_specs=[row, _resident((1, d)), _resident((d, d_ff)), _resident((d, d_ff)),
                  _resident((d_ff, d))],
        out_specs=row,
        compiler_params=pltpu.CompilerParams(
            dimension_semantics=("parallel",), vmem_limit_bytes=VMEM_LIMIT),
        name="ffn",
    )(hp, gain, wg, wu, wd)


_C_QA = 0
_C_KA = _C_QA + GLA_KEY
_C_VA = _C_KA + GLA_KEY
_C_RA = _C_VA + GLA_VAL
_C_QB = _C_RA + GLA_VAL
_C_KB = _C_QB + ATT_Q
_C_LR = _C_KB + 2 * ATT_KV
_C_END = _C_LR + LANES
_Q_SCALE = HEAD_DIM ** -0.5 * float(np.log2(np.e))
ATT_SCORE_BOUND = 1.05 * HEAD_DIM * _Q_SCALE
ATT_SAFE_SCORE = 60.0


def _proj_body(h_ref, g_ref, w_ref, wvt_ref, w2_ref, b2_ref, gq_ref, gk_ref, cos_ref, sin_ref,
               qa_ref, ka_ref, va_ref, sr_ref, gf_ref, gb_ref, qb_ref, kb_ref, vt_ref):
    zb = _rmsnorm_rows(h_ref[...], g_ref[...]).astype(BF16)

    def proj(lo, hi):
        return _dot(zb, w_ref[:, lo:hi])

    qa_ref[...] = (proj(_C_QA, _C_KA) * GLA_DK ** -0.5).astype(BF16)
    ka_ref[...] = proj(_C_KA, _C_VA).astype(BF16)
    va_ref[...] = proj(_C_VA, _C_RA).astype(BF16)
    sr_ref[...] = _silu(proj(_C_RA, _C_QB)).astype(BF16)

    lr = proj(_C_LR, _C_END).astype(BF16)
    pre = _dot(lr, w2_ref[...]) + b2_ref[...]
    lg = (jnp.minimum(pre, 0.0) - jnp.log(1.0 + jnp.exp(-jnp.abs(pre)))) * (1.0 / GLA_TAU)
    gf_ref[...] = lg[:, :GLA_KEY]
    gb_ref[...] = lg[:, GLA_KEY:]

    cos = cos_ref[...]
    sin = sin_ref[...]
    lane = lax.broadcasted_iota(jnp.int32, cos.shape, 1)
    low_head = lane < HEAD_DIM
    first_half = (lane % (HEAD_DIM // 2)) < (HEAD_DIM // 4)

    def norm_rope(x, gain, scale):
        x2 = x * x
        s_lo = jnp.sum(jnp.where(low_head, x2, 0.0), axis=-1, keepdims=True)
        s_hi = jnp.sum(jnp.where(low_head, 0.0, x2), axis=-1, keepdims=True)
        inv = jnp.where(low_head, lax.rsqrt(s_lo * (1.0 / HEAD_DIM) + EPS),
                        lax.rsqrt(s_hi * (1.0 / HEAD_DIM) + EPS))
        y = x * inv * gain
        partner = jnp.where(first_half, pltpu.roll(y, LANES - HEAD_DIM // 4, 1),
                            pltpu.roll(y, HEAD_DIM // 4, 1))
        return (y * cos + partner * sin) * scale

    for base, width, out_ref, gain_ref, scale in ((_C_QB, ATT_Q, qb_ref, gq_ref, _Q_SCALE),
                                                  (_C_KB, 2 * ATT_KV, kb_ref, gk_ref, 1.0)):
        for c in range(width // MXU_COLS):
            x = proj(base + c * MXU_COLS, base + (c + 1) * MXU_COLS)
            for s in range(MXU_COLS // LANES):
                sl = slice(c * MXU_COLS + s * LANES, c * MXU_COLS + (s + 1) * LANES)
                out_ref[:, sl] = norm_rope(x[:, s * LANES:(s + 1) * LANES], gain_ref[...],
                                           scale).astype(BF16)
    vt = _dot_nt(wvt_ref[...], zb)
    row_i = lax.broadcasted_iota(jnp.int32, vt.shape, 0)
    vt_ref[...] = jnp.where(row_i % LANES < HEAD_DIM, vt, 1.0).astype(BF16)


def _proj(hp, gain, w_all, wvt, w2, b2, gq, gk, cos_t, sin_t, lp):
    n, d = hp.shape
    assert lp % TM_TOK == 0
    tiles_per_seq = lp // TM_TOK

    def row(c):
        return pl.BlockSpec((TM_TOK, c), lambda i: (i, 0))

    table = pl.BlockSpec((TM_TOK, LANES), lambda i: (i % tiles_per_seq, 0))
    outs = [(GLA_KEY, BF16), (GLA_KEY, BF16), (GLA_VAL, BF16), (GLA_VAL, BF16),
            (GLA_KEY, F32), (GLA_KEY, F32), (ATT_Q, BF16), (2 * ATT_KV, BF16)]
    vt_rows = wvt.shape[0]
    return pl.pallas_call(
        _proj_body,
        out_shape=[jax.ShapeDtypeStruct((n, c), t) for c, t in outs]
        + [jax.ShapeDtypeStruct((vt_rows, n), BF16)],
        grid=(n // TM_TOK,),
        in_specs=[row(d), _resident((1, d)), _resident(w_all.shape), _resident(wvt.shape),
                  _resident(w2.shape), _resident(b2.shape), _resident((1, LANES)),
                  _resident((1, LANES)), table, table],
        out_specs=[row(c) for c, _ in outs]
        + [pl.BlockSpec((vt_rows, TM_TOK), lambda i: (0, i))],
        compiler_params=pltpu.CompilerParams(
            dimension_semantics=("parallel",), vmem_limit_bytes=VMEM_LIMIT),
        name="proj",
    )(hp, gain, w_all, wvt, w2, b2, gq, gk, cos_t, sin_t)


def _gla_body(q_ref, k_ref, v_ref, gf_ref, gb_ref, sr_ref, gn_ref, o_ref, ob_ref, st_ref):
    lp = q_ref.shape[0]
    assert GLA_LEAD % CHUNK == 0 and (lp - GLA_LEAD) % GLA_ROWS == 0
    n_main = (lp - GLA_LEAD) // GLA_ROWS

    def block(start, rows_n, forward, final):
        nch = rows_n // CHUNK
        rows = pl.ds(start, rows_n)
        pos = lax.broadcasted_iota(jnp.int32, (rows_n, LANES), 0) % CHUNK
        lane = lax.broadcasted_iota(jnp.int32, (rows_n, LANES), 1)

        bc = (gf_ref if forward else gb_ref)[rows, :]
        shift = 1
        while shift < CHUNK:
            if forward:
                moved, ok = pltpu.roll(bc, shift, 0), pos >= shift
            else:
                moved, ok = pltpu.roll(bc, rows_n - shift, 0), pos < CHUNK - shift
            bc = bc + jnp.where(ok, moved, 0.0)
            shift *= 2
        edge = CHUNK - 1 if forward else 0
        bt3 = bc.reshape(nch, CHUNK, LANES)[:, edge:edge + 1, :]
        bt = jnp.broadcast_to(bt3, (nch, CHUNK, LANES)).reshape(rows_n, LANES)
        dec = jnp.exp(bt3)

        q = q_ref[rows, :].astype(F32)
        k = k_ref[rows, :].astype(F32)
        qd = q * jnp.exp(bc)
        ki = (k * jnp.exp(-bc)).astype(BF16)
        ke = (k * jnp.exp(bt - bc)).astype(BF16)

        r_i = lax.broadcasted_iota(jnp.int32, (rows_n, rows_n), 0)
        c_i = lax.broadcasted_iota(jnp.int32, (rows_n, rows_n), 1)
        same_chunk = (r_i // CHUNK) == (c_i // CHUNK)
        att_ok = same_chunk & ((r_i >= c_i) if forward else (r_i <= c_i))
        wide = (rows_n, nch * LANES)
        own_block = (lax.broadcasted_iota(jnp.int32, wide, 0) // CHUNK
                     == lax.broadcasted_iota(jnp.int32, wide, 1) // LANES)
        ke_blk = jnp.where(own_block, jnp.tile(ke, (1, nch)), jnp.zeros(wide, BF16))

        for h in range(2):
            hs = slice(h * GLA_DV, (h + 1) * GLA_DV)
            qh = jnp.where((lane < GLA_DK) if h == 0 else (lane >= GLA_DK), qd, 0.0).astype(BF16)
            vh = v_ref[rows, hs]
            att = jnp.where(att_ok, _dot_nt(qh, ki), 0.0).astype(BF16)
            o = _dot(att, vh)
            kv_t = _dot_tn(vh, ke_blk)
            st = st_ref[h]
            states = [None] * nch
            for c in (range(nch) if forward else range(nch - 1, -1, -1)):
                states[c] = st
                st = st * dec[c] + kv_t[:, c * LANES:(c + 1) * LANES]
            st_ref[h] = st
            q_blk = jnp.where(own_block, jnp.tile(qh, (1, nch)), jnp.zeros(wide, BF16))
            o = o + _dot_nt(q_blk, jnp.concatenate(states, axis=1).astype(BF16))
            if not final:
                ob_ref[rows, hs] = o
            else:
                o = o + ob_ref[rows, hs]
                o = o * lax.rsqrt(jnp.mean(o * o, axis=-1, keepdims=True) + EPS) * gn_ref[:, hs]
                o_ref[rows, hs] = (o * sr_ref[rows, hs].astype(F32)).astype(BF16)

    def main_start(j):
        return pl.multiple_of(GLA_LEAD + j * GLA_ROWS, CHUNK)

    st_ref[...] = jnp.zeros(st_ref.shape, F32)

    def bwd(i, carry):
        block(main_start(n_main - 1 - i), GLA_ROWS, False, False)
        return carry

    lax.fori_loop(0, n_main, bwd, 0, unroll=8)
    block(0, GLA_LEAD, False, False)

    st_ref[...] = jnp.zeros(st_ref.shape, F32)
    block(0, GLA_LEAD, True, True)

    def fwd(i, carry):
        block(main_start(i), GLA_ROWS, True, True)
        return carry

    lax.fori_loop(0, n_main, fwd, 0, unroll=8)


def _gla(qa, ka, va, gf, gb, sr, gn, lp):
    n = qa.shape[0]
    nb = n // lp
    pairs = GLA_HEADS // 2

    def blk(c):
        return pl.BlockSpec((lp, c), lambda b, p: (b, p))

    return pl.pallas_call(
        _gla_body,
        out_shape=jax.ShapeDtypeStruct((n, GLA_VAL), BF16),
        grid=(nb, pairs),
        in_specs=[blk(LANES), blk(LANES), blk(2 * GLA_DV), blk(LANES), blk(LANES),
                  blk(2 * GLA_DV), pl.BlockSpec((1, 2 * GLA_DV), lambda b, p: (0, p))],
        out_specs=blk(2 * GLA_DV),
        scratch_shapes=[pltpu.VMEM((lp, 2 * GLA_DV), F32),
                        pltpu.VMEM((2, GLA_DV, LANES), F32)],
        compiler_params=pltpu.CompilerParams(
            dimension_semantics=("parallel", "parallel"), vmem_limit_bytes=VMEM_LIMIT),
        name="gla",
    )(qa, ka, va, gf, gb, sr, gn)


def _attn_body(q_ref, k_ref, vt_ref, o_ref, s_ref, *, padf, online):
    tq = q_ref.shape[0]
    lp = k_ref.shape[0]
    group = ATT_HEADS // ATT_KV_HEADS
    cols = group * tq
    n_main = (lp - LANES) // TK_ATT
    low = lax.broadcasted_iota(jnp.int32, (tq, LANES), 1) < HEAD_DIM

    for kvh in range(ATT_KV_HEADS):
        hs = slice(kvh * LANES, (kvh + 1) * LANES)
        parts = []
        for s in range(group // 2):
            slab = kvh * (group // 2) + s
            q = q_ref[:, slab * LANES:(slab + 1) * LANES]
            zero = jnp.zeros_like(q)
            parts += [jnp.where(low, q, zero), jnp.where(low, zero, q)]
        qs = jnp.concatenate(parts, axis=0)

        def scores(start, size, qs=qs, hs=hs):
            return _dot_nt(k_ref[pl.ds(start, size), hs], qs)

        def absorb(s, start, size, m, acc, hs=hs):
            if not online:
                return m, acc + _dot(vt_ref[hs, pl.ds(start, size)], jnp.exp2(s).astype(BF16))
            m_new = jnp.maximum(m, jnp.max(s, axis=0, keepdims=True))
            p = jnp.exp2(s - m_new).astype(BF16)
            pv = _dot(vt_ref[hs, pl.ds(start, size)], p)
            return m_new, acc * jnp.exp2(m - m_new) + pv

        def main_start(j):
            return pl.multiple_of(LANES + j * TK_ATT, LANES)

        key_ok = lax.broadcasted_iota(jnp.int32, (LANES, cols), 0) >= padf
        m, acc = absorb(jnp.where(key_ok, scores(0, LANES), -1e30), 0, LANES,
                        jnp.full((1, cols), -jnp.inf, F32), jnp.zeros((LANES, cols), F32))

        assert n_main % 2 == 0
        if online:
            s_ref[0] = scores(main_start(0), TK_ATT)

        def pair(i, carry, last):
            m, acc = carry
            s_ref[1] = scores(main_start(2 * i + 1), TK_ATT)
            m, acc = absorb(s_ref[0], main_start(2 * i), TK_ATT, m, acc)
            if not last:
                s_ref[0] = scores(main_start(2 * i + 2), TK_ATT)
            return absorb(s_ref[1], main_start(2 * i + 1), TK_ATT, m, acc)

        if online:
            m, acc = lax.fori_loop(0, n_main // 2 - 1, lambda i, c: pair(i, c, False), (m, acc))
            _, acc = pair(n_main // 2 - 1, (m, acc), True)
        else:
            def direct(j, acc):
                _, acc = absorb(scores(main_start(j), TK_ATT), main_start(j), TK_ATT, None, acc)
                return acc

            acc = lax.fori_loop(0, n_main, direct, acc, unroll=8)
        out_t = acc[:HEAD_DIM] / acc[HEAD_DIM:]
        for s in range(group // 2):
            slab = kvh * (group // 2) + s
            pair_t = jnp.concatenate([out_t[:, (2 * s) * tq:(2 * s + 1) * tq],
                                      out_t[:, (2 * s + 1) * tq:(2 * s + 2) * tq]], axis=0)
            o_ref[:, slab * LANES:(slab + 1) * LANES] = pair_t.T.astype(BF16)


def _attn(qb, kb, vt, lp, padf, online):
    n = qb.shape[0]
    nb = n // lp
    assert lp % TQ_ATT == 0 and (lp - LANES) % TK_ATT == 0
    nq = lp // TQ_ATT
    q_spec = pl.BlockSpec((TQ_ATT, ATT_Q), lambda b, i: (b * nq + i, 0))
    return pl.pallas_call(
        functools.partial(_attn_body, padf=padf, online=online),
        out_shape=jax.ShapeDtypeStruct((n, ATT_Q), BF16),
        grid=(nb, nq),
        in_specs=[q_spec, pl.BlockSpec((lp, kb.shape[1]), lambda b, i: (b, 0)),
                  pl.BlockSpec((vt.shape[0], lp), lambda b, i: (0, b))],
        out_specs=q_spec,
        scratch_shapes=[pltpu.VMEM((2, TK_ATT, ATT_HEADS // ATT_KV_HEADS * TQ_ATT), F32)],
        compiler_params=pltpu.CompilerParams(
            dimension_semantics=("parallel", "arbitrary"), vmem_limit_bytes=VMEM_LIMIT),
        name="attn",
    )(qb, kb, vt)


def _merge_body(h_ref, g_ref, a_ref, b_ref, wg_ref, bm_ref, wpa_ref, wpb_ref, wo_ref, o_ref,
                *, lp, padf):
    x = h_ref[...]
    d = x.shape[1]
    zb = _rmsnorm_rows(x, g_ref[...]).astype(BF16)
    gate_a = jax.nn.sigmoid(_dot(zb, wg_ref[:, :d]) + bm_ref[:, :d])
    gate_b = jax.nn.sigmoid(_dot(zb, wg_ref[:, d:]) + bm_ref[:, d:])
    y = gate_a * _dot(a_ref[...], wpa_ref[...]) + gate_b * _dot(b_ref[...], wpb_ref[...])
    out = _dot(y.astype(BF16), wo_ref[...])
    tm = x.shape[0]
    pos = (pl.program_id(0) % (lp // tm)) * tm + lax.broadcasted_iota(jnp.int32, (tm, 1), 0)
    o_ref[...] = x + jnp.where(pos >= padf, out, 0.0)


def _merge(hp, gain, a, b, wg, bm, wpa, wpb, wo, lp, padf):
    n, d = hp.shape

    def row(c):
        return pl.BlockSpec((TM_TOK, c), lambda i: (i, 0))

    return pl.pallas_call(
        functools.partial(_merge_body, lp=lp, padf=padf),
        out_shape=jax.ShapeDtypeStruct((n, d), F32),
        grid=(n // TM_TOK,),
        in_specs=[row(d), _resident((1, d)), row(GLA_VAL), row(ATT_Q), _resident(wg.shape),
                  _resident(bm.shape), _resident(wpa.shape), _resident(wpb.shape),
                  _resident(wo.shape)],
        out_specs=row(d),
        compiler_params=pltpu.CompilerParams(
            dimension_semantics=("parallel",), vmem_limit_bytes=VMEM_LIMIT),
        name="merge",
    )(hp, gain, a, b, wg, bm, wpa, wpb, wo)


def _final_body(h_ref, g_ref, o_ref):
    o_ref[...] = _rmsnorm_rows(h_ref[...], g_ref[...])


def _final(hp, gain, nb, n_tok, lp, skip):
    d = hp.shape[1]
    assert n_tok % TM_FIN == 0
    per_out = n_tok // TM_FIN
    return pl.pallas_call(
        _final_body,
        out_shape=jax.ShapeDtypeStruct((nb * n_tok, d), F32),
        grid=(nb, per_out),
        in_specs=[pl.BlockSpec((pl.Element(TM_FIN), pl.Element(d)),
                               lambda b, i: (pl.multiple_of(b * lp + skip + i * TM_FIN, LANES), 0)),
                  _resident((1, d))],
        out_specs=pl.BlockSpec((TM_FIN, d), lambda b, i: (b * per_out + i, 0)),
        compiler_params=pltpu.CompilerParams(
            dimension_semantics=("parallel", "parallel"), vmem_limit_bytes=VMEM_LIMIT),
        name="final_norm",
    )(hp, gain)


def _rope_tables(n_tok, lp, padf):
    quarter = HEAD_DIM // 4
    pos = jnp.arange(n_tok)
    inv = ROPE_THETA ** (-jnp.arange(0, HEAD_DIM // 2, 2, dtype=F32) / (HEAD_DIM // 2))
    ang_r = (pos // GRID_W).astype(F32)[:, None] * inv
    ang_c = (pos % GRID_W).astype(F32)[:, None] * inv
    cos = jnp.concatenate([jnp.cos(ang_r)] * 2 + [jnp.cos(ang_c)] * 2, axis=-1)
    sin = jnp.concatenate([-jnp.sin(ang_r), jnp.sin(ang_r), -jnp.sin(ang_c), jnp.sin(ang_c)],
                          axis=-1)
    lead = padf + N_META
    cos = jnp.concatenate([jnp.ones((lead, HEAD_DIM), F32), cos], axis=0)
    sin = jnp.concatenate([jnp.zeros((lead, HEAD_DIM), F32), sin], axis=0)
    assert cos.shape == (lp, 4 * quarter)
    return jnp.tile(cos, (1, 2)), jnp.tile(sin, (1, 2))


def _prep_w_in(w_in):
    sizes = (GLA_KEY, GLA_KEY, GLA_VAL, GLA_VAL, GLA_RANK, GLA_RANK, ATT_Q, ATT_KV, ATT_KV)
    offs = np.concatenate([[0], np.cumsum(sizes)])
    qa, ka, va, ra, lrf, lrb, qb, kb, vb = (w_in[:, offs[i]:offs[i + 1]] for i in range(9))

    def spread(w, twice):
        heads = [w[:, h * HEAD_DIM:(h + 1) * HEAD_DIM] for h in range(ATT_KV_HEADS)]
        return jnp.concatenate([x for h in heads for x in (h, h if twice else jnp.zeros_like(h))],
                               axis=-1)

    lr = jnp.concatenate([lrf, lrb, jnp.zeros((w_in.shape[0], LANES - 2 * GLA_RANK), F32)], -1)
    w_all = jnp.concatenate([qa, ka, va, ra, qb, spread(kb, True), lr], axis=-1).astype(BF16)
    return w_all, spread(vb, False).T.astype(BF16)


def _prep_gate(w2, b2):
    w = jnp.zeros((LANES, 2 * GLA_KEY), F32)
    w = w.at[:GLA_RANK, :GLA_KEY].set(w2[0]).at[GLA_RANK:2 * GLA_RANK, GLA_KEY:].set(w2[1])
    return w.astype(BF16), b2.reshape(1, 2 * GLA_KEY)


def kernel(x, meta_tokens, norm_gains, ffn_w_gate, ffn_w_up, ffn_w_down, w_in, gla_w2, gla_b2,
           gla_gn, q_norm, k_norm, w_pa, w_pb, b_merge, w_out, final_norm):
    nb, n_tok, d = x.shape
    depth = w_in.shape[0]
    lp = -(-(n_tok + N_META) // LANES) * LANES
    padf = lp - n_tok - N_META
    assert (padf + N_META) % CHUNK == 0 and n_tok % GRID_W == 0

    meta = jnp.broadcast_to(meta_tokens[None].astype(x.dtype), (nb, N_META, d))
    hp = jnp.concatenate([jnp.zeros((nb, padf, d), x.dtype), meta, x], axis=1).reshape(nb * lp, d)
    cos_t, sin_t = _rope_tables(n_tok, lp, padf)

    g_split = w_in.shape[2] - 2 * d
    for l in range(depth):
        gains = norm_gains[l].reshape(3, 1, d)
        hp = _ffn(hp, gains[0], ffn_w_gate[l, 0].astype(BF16), ffn_w_up[l, 0].astype(BF16),
                  ffn_w_down[l, 0].astype(BF16))
        w2, b2 = _prep_gate(gla_w2[l], gla_b2[l])
        w_all, wvt = _prep_w_in(w_in[l, :, :g_split])
        qa, ka, va, sr, gf, gb, qb, kb, vt = _proj(
            hp, gains[1], w_all, wvt, w2, b2,
            jnp.tile(q_norm[l], 2).reshape(1, LANES), jnp.tile(k_norm[l], 2).reshape(1, LANES),
            cos_t, sin_t, lp)
        a = _gla(qa, ka, va, gf, gb, sr, gla_gn[l].reshape(1, GLA_VAL), lp)
        small = (jnp.max(jnp.abs(q_norm[l])) * jnp.max(jnp.abs(k_norm[l])) * ATT_SCORE_BOUND
                 <= ATT_SAFE_SCORE)
        b = lax.cond(small, functools.partial(_attn, lp=lp, padf=padf, online=False),
                     functools.partial(_attn, lp=lp, padf=padf, online=True), qb, kb, vt)
        hp = _merge(hp, gains[1], a, b, w_in[l, :, g_split:].astype(BF16),
                    b_merge[l].reshape(1, 2 * d), w_pa[l].astype(BF16), w_pb[l].astype(BF16),
                    w_out[l].astype(BF16), lp, padf)
        hp = _ffn(hp, gains[2], ffn_w_gate[l, 1].astype(BF16), ffn_w_up[l, 1].astype(BF16),
                  ffn_w_down[l, 1].astype(BF16))
    out = _final(hp, final_norm.reshape(1, d), nb, n_tok, lp, padf + N_META)
    return out.reshape(nb, n_tok, d)
```

```python
import functools

import jax
import jax.numpy as jnp
import numpy as np
from jax import lax
from jax.experimental import pallas as pl
from jax.experimental.pallas import tpu as pltpu

F32 = jnp.float32
BF16 = jnp.bfloat16

N_META = 16
GRID_W = 64
GLA_HEADS = 4
GLA_DK = 64
GLA_DV = 128
GLA_KEY = GLA_HEADS * GLA_DK
GLA_VAL = GLA_HEADS * GLA_DV
GLA_RANK = 16
GLA_TAU = 16.0
CHUNK = 64
ATT_HEADS = 8
ATT_KV_HEADS = 2
HEAD_DIM = 64
ATT_Q = ATT_HEADS * HEAD_DIM
ATT_KV = ATT_KV_HEADS * HEAD_DIM
ROPE_THETA = 10000.0
EPS = 1e-6

LANES = 128
MXU_COLS = 256
VMEM_LIMIT = 56 * 1024 * 1024

TM_FFN = 512
TF_FFN = 2816
TM_TOK = 384
TM_PROJ = 384
TQ_ATT = 384
TK_ATT = 512
TM_FIN = 1024
GLA_ROWS = 256
GLA_LEAD = 128


def _resident(shape):
    nd = len(shape)
    return pl.BlockSpec(shape, lambda *_: (0,) * nd, pipeline_mode=pl.Buffered(1))


def _rmsnorm_rows(x, gain):
    return x * lax.rsqrt(jnp.mean(x * x, axis=-1, keepdims=True) + EPS) * gain


def _dot(a, b):
    return jnp.dot(a, b, preferred_element_type=F32)


def _dot_nt(a, b):
    return lax.dot_general(a, b, (((1,), (1,)), ((), ())), preferred_element_type=F32)


def _dot_tn(a, b):
    return lax.dot_general(a, b, (((0,), (0,)), ((), ())), preferred_element_type=F32)


def _silu(x):
    return x * jax.nn.sigmoid(x)


def _ffn_body(x_ref, g_ref, wg_ref, wu_ref, wd_ref, o_ref):
    x = x_ref[...]
    xb = _rmsnorm_rows(x, g_ref[...]).astype(BF16)
    d_ff = wg_ref.shape[1]
    acc = jnp.zeros(x.shape, F32)
    for j in range(d_ff // TF_FFN):
        sl = slice(j * TF_FFN, (j + 1) * TF_FFN)
        gate = _dot(xb, wg_ref[:, sl])
        up = _dot(xb, wu_ref[:, sl])
        acc = acc + _dot((_silu(gate) * up).astype(BF16), wd_ref[sl, :])
    o_ref[...] = x + 0.5 * acc


def _ffn(hp, gain, wg, wu, wd):
    n, d = hp.shape
    d_ff = wg.shape[1]
    assert n % TM_FFN == 0 and d_ff % TF_FFN == 0
    row = pl.BlockSpec((TM_FFN, d), lambda i: (i, 0))
    return pl.pallas_call(
        _ffn_body,
        out_shape=jax.ShapeDtypeStruct((n, d), F32),
        grid=(n // TM_FFN,),
        in_specs=[row, _resident((1, d)), _resident((d, d_ff)), _resident((d, d_ff)),
                  _resident((d_ff, d))],
        out_specs=row,
        compiler_params=pltpu.CompilerParams(
            dimension_semantics=("parallel",), vmem_limit_bytes=VMEM_LIMIT),
        name="ffn",
    )(hp, gain, wg, wu, wd)


_C_QA = 0
_C_KA = _C_QA + GLA_KEY
_C_VA = _C_KA + GLA_KEY
_C_RA = _C_VA + GLA_VAL
_C_QB = _C_RA + GLA_VAL
_C_KB = _C_QB + ATT_Q
_C_LR = _C_KB + 2 * ATT_KV
_C_END = _C_LR + LANES
_Q_SCALE = HEAD_DIM ** -0.5 * float(np.log2(np.e))
ATT_SCORE_BOUND = 1.05 * HEAD_DIM * _Q_SCALE
ATT_SAFE_SCORE = 60.0


def _proj_body(h_ref, g_ref, w_ref, wvt_ref, w2_ref, b2_ref, gq_ref, gk_ref, cos_ref, sin_ref,
               qa_ref, ka_ref, va_ref, sr_ref, gf_ref, gb_ref, qb_ref, kb_ref, vt_ref):
    zb = _rmsnorm_rows(h_ref[...], g_ref[...]).astype(BF16)

    def proj(lo, hi):
        return _dot(zb, w_ref[:, lo:hi])

    cos = cos_ref[...]
    sin = sin_ref[...]
    lane = lax.broadcasted_iota(jnp.int32, cos.shape, 1)
    low_head = lane < HEAD_DIM
    first_half = (lane % (HEAD_DIM // 2)) < (HEAD_DIM // 4)

    def norm_rope(x, gain, scale):
        x2 = x * x
        s_lo = jnp.sum(jnp.where(low_head, x2, 0.0), axis=-1, keepdims=True)
        s_hi = jnp.sum(jnp.where(low_head, 0.0, x2), axis=-1, keepdims=True)
        inv = jnp.where(low_head, lax.rsqrt(s_lo * (1.0 / HEAD_DIM) + EPS),
                        lax.rsqrt(s_hi * (1.0 / HEAD_DIM) + EPS))
        y = x * inv * gain
        partner = jnp.where(first_half, pltpu.roll(y, LANES - HEAD_DIM // 4, 1),
                            pltpu.roll(y, HEAD_DIM // 4, 1))
        return (y * cos + partner * sin) * scale

    for base, width, out_ref, gain_ref, scale in ((_C_QB, ATT_Q, qb_ref, gq_ref, _Q_SCALE),
                                                  (_C_KB, 2 * ATT_KV, kb_ref, gk_ref, 1.0)):
        for c in range(width // MXU_COLS):
            x = proj(base + c * MXU_COLS, base + (c + 1) * MXU_COLS)
            for s in range(MXU_COLS // LANES):
                sl = slice(c * MXU_COLS + s * LANES, c * MXU_COLS + (s + 1) * LANES)
                out_ref[:, sl] = norm_rope(x[:, s * LANES:(s + 1) * LANES], gain_ref[...],
                                           scale).astype(BF16)
    lr = proj(_C_LR, _C_END).astype(BF16)
    pre = _dot(lr, w2_ref[...]) + b2_ref[...]
    lg = (jnp.minimum(pre, 0.0) - jnp.log(1.0 + jnp.exp(-jnp.abs(pre)))) * (1.0 / GLA_TAU)
    gf_ref[...] = lg[:, :GLA_KEY]
    gb_ref[...] = lg[:, GLA_KEY:]
    qa_ref[...] = (proj(_C_QA, _C_KA) * GLA_DK ** -0.5).astype(BF16)
    ka_ref[...] = proj(_C_KA, _C_VA).astype(BF16)
    va_ref[...] = proj(_C_VA, _C_RA).astype(BF16)
    sr_ref[...] = _silu(proj(_C_RA, _C_QB)).astype(BF16)
    vt = _dot_nt(wvt_ref[...], zb)
    row_i = lax.broadcasted_iota(jnp.int32, vt.shape, 0)
    vt_ref[...] = jnp.where(row_i % LANES < HEAD_DIM, vt, 1.0).astype(BF16)


def _proj(hp, gain, w_all, wvt, w2, b2, gq, gk, cos_t, sin_t, lp):
    n, d = hp.shape
    assert lp % TM_PROJ == 0
    tiles_per_seq = lp // TM_PROJ

    def row(c):
        return pl.BlockSpec((TM_PROJ, c), lambda i: (i, 0))

    table = pl.BlockSpec((TM_PROJ, LANES), lambda i: (i % tiles_per_seq, 0))
    outs = [(GLA_KEY, BF16), (GLA_KEY, BF16), (GLA_VAL, BF16), (GLA_VAL, BF16),
            (GLA_KEY, F32), (GLA_KEY, F32), (ATT_Q, BF16), (2 * ATT_KV, BF16)]
    vt_rows = wvt.shape[0]
    return pl.pallas_call(
        _proj_body,
        out_shape=[jax.ShapeDtypeStruct((n, c), t) for c, t in outs]
        + [jax.ShapeDtypeStruct((vt_rows, n), BF16)],
        grid=(n // TM_PROJ,),
        in_specs=[row(d), _resident((1, d)), _resident(w_all.shape), _resident(wvt.shape),
                  _resident(w2.shape), _resident(b2.shape), _resident((1, LANES)),
                  _resident((1, LANES)), table, table],
        out_specs=[row(c) for c, _ in outs]
        + [pl.BlockSpec((vt_rows, TM_PROJ), lambda i: (0, i))],
        compiler_params=pltpu.CompilerParams(
            dimension_semantics=("parallel",), vmem_limit_bytes=VMEM_LIMIT),
        name="proj",
    )(hp, gain, w_all, wvt, w2, b2, gq, gk, cos_t, sin_t)


def _gla_body(q_ref, k_ref, v_ref, gf_ref, gb_ref, sr_ref, gn_ref, o_ref, ob_ref, st_ref):
    lp = q_ref.shape[0]
    assert GLA_LEAD % CHUNK == 0 and (lp - GLA_LEAD) % GLA_ROWS == 0
    n_main = (lp - GLA_LEAD) // GLA_ROWS

    def block(start, rows_n, forward, final):
        nch = rows_n // CHUNK
        rows = pl.ds(start, rows_n)
        pos = lax.broadcasted_iota(jnp.int32, (rows_n, LANES), 0) % CHUNK
        lane = lax.broadcasted_iota(jnp.int32, (rows_n, LANES), 1)

        bc = (gf_ref if forward else gb_ref)[rows, :]
        shift = 1
        while shift < CHUNK:
            if forward:
                moved, ok = pltpu.roll(bc, shift, 0), pos >= shift
            else:
                moved, ok = pltpu.roll(bc, rows_n - shift, 0), pos < CHUNK - shift
            bc = bc + jnp.where(ok, moved, 0.0)
            shift *= 2
        edge = CHUNK - 1 if forward else 0
        bt3 = bc.reshape(nch, CHUNK, LANES)[:, edge:edge + 1, :]
        bt = jnp.broadcast_to(bt3, (nch, CHUNK, LANES)).reshape(rows_n, LANES)
        dec = jnp.exp(bt3)

        q = q_ref[rows, :].astype(F32)
        k = k_ref[rows, :].astype(F32)
        qd = q * jnp.exp(bc)
        ki = (k * jnp.exp(-bc)).astype(BF16)
        ke = (k * jnp.exp(bt - bc)).astype(BF16)

        r_i = lax.broadcasted_iota(jnp.int32, (rows_n, rows_n), 0)
        c_i = lax.broadcasted_iota(jnp.int32, (rows_n, rows_n), 1)
        same_chunk = (r_i // CHUNK) == (c_i // CHUNK)
        att_ok = same_chunk & ((r_i >= c_i) if forward else (r_i <= c_i))
        wide = (rows_n, nch * LANES)
        own_block = (lax.broadcasted_iota(jnp.int32, wide, 0) // CHUNK
                     == lax.broadcasted_iota(jnp.int32, wide, 1) // LANES)
        ke_blk = jnp.where(own_block, jnp.tile(ke, (1, nch)), jnp.zeros(wide, BF16))

        entering = []
        for h in range(2):
            hs = slice(h * GLA_DV, (h + 1) * GLA_DV)
            kv_t = _dot_tn(v_ref[rows, hs], ke_blk)
            st = st_ref[h]
            states = [None] * nch
            for c in (range(nch) if forward else range(nch - 1, -1, -1)):
                states[c] = st
                st = st * dec[c] + kv_t[:, c * LANES:(c + 1) * LANES]
            st_ref[h] = st
            entering.append(jnp.concatenate(states, axis=1).astype(BF16))

        for h in range(2):
            hs = slice(h * GLA_DV, (h + 1) * GLA_DV)
            qh = jnp.where((lane < GLA_DK) if h == 0 else (lane >= GLA_DK), qd, 0.0).astype(BF16)
            att = jnp.where(att_ok, _dot_nt(qh, ki), 0.0).astype(BF16)
            q_blk = jnp.where(own_block, jnp.tile(qh, (1, nch)), jnp.zeros(wide, BF16))
            o = _dot(att, v_ref[rows, hs]) + _dot_nt(q_blk, entering[h])
            if not final:
                ob_ref[rows, hs] = o
            else:
                o = o + ob_ref[rows, hs]
                o = o * lax.rsqrt(jnp.mean(o * o, axis=-1, keepdims=True) + EPS) * gn_ref[:, hs]
                o_ref[rows, hs] = (o * sr_ref[rows, hs].astype(F32)).astype(BF16)

    def main_start(j):
        return pl.multiple_of(GLA_LEAD + j * GLA_ROWS, CHUNK)

    st_ref[...] = jnp.zeros(st_ref.shape, F32)

    def bwd(i, carry):
        block(main_start(n_main - 1 - i), GLA_ROWS, False, False)
        return carry

    lax.fori_loop(0, n_main, bwd, 0, unroll=16)
    block(0, GLA_LEAD, False, False)

    st_ref[...] = jnp.zeros(st_ref.shape, F32)
    block(0, GLA_LEAD, True, True)

    def fwd(i, carry):
        block(main_start(i), GLA_ROWS, True, True)
        return carry

    lax.fori_loop(0, n_main, fwd, 0, unroll=16)


def _gla(qa, ka, va, gf, gb, sr, gn, lp):
    n = qa.shape[0]
    nb = n // lp
    pairs = GLA_HEADS // 2

    def blk(c):
        return pl.BlockSpec((lp, c), lambda b, p: (b, p))

    return pl.pallas_call(
        _gla_body,
        out_shape=jax.ShapeDtypeStruct((n, GLA_VAL), BF16),
        grid=(nb, pairs),
        in_specs=[blk(LANES), blk(LANES), blk(2 * GLA_DV), blk(LANES), blk(LANES),
                  blk(2 * GLA_DV), pl.BlockSpec((1, 2 * GLA_DV), lambda b, p: (0, p))],
        out_specs=blk(2 * GLA_DV),
        scratch_shapes=[pltpu.VMEM((lp, 2 * GLA_DV), F32),
                        pltpu.VMEM((2, GLA_DV, LANES), F32)],
        compiler_params=pltpu.CompilerParams(
            dimension_semantics=("parallel", "parallel"), vmem_limit_bytes=VMEM_LIMIT),
        name="gla",
    )(qa, ka, va, gf, gb, sr, gn)


def _attn_body(q_ref, k_ref, vt_ref, o_ref, s_ref, *, padf, online):
    tq = q_ref.shape[0]
    lp = k_ref.shape[0]
    group = ATT_HEADS // ATT_KV_HEADS
    cols = group * tq
    n_main = (lp - LANES) // TK_ATT
    low = lax.broadcasted_iota(jnp.int32, (tq, LANES), 1) < HEAD_DIM

    for kvh in range(ATT_KV_HEADS):
        hs = slice(kvh * LANES, (kvh + 1) * LANES)
        parts = []
        for s in range(group // 2):
            slab = kvh * (group // 2) + s
            q = q_ref[:, slab * LANES:(slab + 1) * LANES]
            zero = jnp.zeros_like(q)
            parts += [jnp.where(low, q, zero), jnp.where(low, zero, q)]
        qs = jnp.concatenate(parts, axis=0)

        def scores(start, size, qs=qs, hs=hs):
            return _dot_nt(k_ref[pl.ds(start, size), hs], qs)

        def absorb(s, start, size, m, acc, hs=hs):
            if not online:
                return m, acc + _dot(vt_ref[hs, pl.ds(start, size)], jnp.exp2(s).astype(BF16))
            m_new = jnp.maximum(m, jnp.max(s, axis=0, keepdims=True))
            p = jnp.exp2(s - m_new).astype(BF16)
            pv = _dot(vt_ref[hs, pl.ds(start, size)], p)
            return m_new, acc * jnp.exp2(m - m_new) + pv

        def main_start(j):
            return pl.multiple_of(LANES + j * TK_ATT, LANES)

        key_ok = lax.broadcasted_iota(jnp.int32, (LANES, cols), 0) >= padf
        m, acc = absorb(jnp.where(key_ok, scores(0, LANES), -1e30), 0, LANES,
                        jnp.full((1, cols), -jnp.inf, F32), jnp.zeros((LANES, cols), F32))

        assert n_main % 2 == 0
        if online:
            s_ref[0] = scores(main_start(0), TK_ATT)

        def pair(i, carry, last):
            m, acc = carry
            s_ref[1] = scores(main_start(2 * i + 1), TK_ATT)
            m, acc = absorb(s_ref[0], main_start(2 * i), TK_ATT, m, acc)
            if not last:
                s_ref[0] = scores(main_start(2 * i + 2), TK_ATT)
            return absorb(s_ref[1], main_start(2 * i + 1), TK_ATT, m, acc)

        if online:
            m, acc = lax.fori_loop(0, n_main // 2 - 1, lambda i, c: pair(i, c, False), (m, acc))
            _, acc = pair(n_main // 2 - 1, (m, acc), True)
        else:
            def direct(j, acc):
                _, acc = absorb(scores(main_start(j), TK_ATT), main_start(j), TK_ATT, None, acc)
                return acc

            acc = lax.fori_loop(0, n_main, direct, acc, unroll=8)
        out_t = acc[:HEAD_DIM] / acc[HEAD_DIM:]
        for s in range(group // 2):
            slab = kvh * (group // 2) + s
            pair_t = jnp.concatenate([out_t[:, (2 * s) * tq:(2 * s + 1) * tq],
                                      out_t[:, (2 * s + 1) * tq:(2 * s + 2) * tq]], axis=0)
            o_ref[:, slab * LANES:(slab + 1) * LANES] = pair_t.T.astype(BF16)


def _attn(qb, kb, vt, lp, padf, online):
    n = qb.shape[0]
    nb = n // lp
    assert lp % TQ_ATT == 0 and (lp - LANES) % TK_ATT == 0
    nq = lp // TQ_ATT
    q_spec = pl.BlockSpec((TQ_ATT, ATT_Q), lambda b, i: (b * nq + i, 0))
    return pl.pallas_call(
        functools.partial(_attn_body, padf=padf, online=online),
        out_shape=jax.ShapeDtypeStruct((n, ATT_Q), BF16),
        grid=(nb, nq),
        in_specs=[q_spec, pl.BlockSpec((lp, kb.shape[1]), lambda b, i: (b, 0)),
                  pl.BlockSpec((vt.shape[0], lp), lambda b, i: (0, b))],
        out_specs=q_spec,
        scratch_shapes=[pltpu.VMEM((2, TK_ATT, ATT_HEADS // ATT_KV_HEADS * TQ_ATT), F32)],
        compiler_params=pltpu.CompilerParams(
            dimension_semantics=("parallel", "arbitrary"), vmem_limit_bytes=VMEM_LIMIT),
        name="attn",
    )(qb, kb, vt)


def _merge_body(h_ref, g_ref, a_ref, b_ref, wg_ref, bm_ref, wpa_ref, wpb_ref, wo_ref, o_ref,
                *, lp, padf):
    x = h_ref[...]
    d = x.shape[1]
    zb = _rmsnorm_rows(x, g_ref[...]).astype(BF16)
    gate_a = jax.nn.sigmoid(_dot(zb, wg_ref[:, :d]) + bm_ref[:, :d])
    gate_b = jax.nn.sigmoid(_dot(zb, wg_ref[:, d:]) + bm_ref[:, d:])
    y = gate_a * _dot(a_ref[...], wpa_ref[...]) + gate_b * _dot(b_ref[...], wpb_ref[...])
    out = _dot(y.astype(BF16), wo_ref[...])
    tm = x.shape[0]
    pos = (pl.program_id(0) % (lp // tm)) * tm + lax.broadcasted_iota(jnp.int32, (tm, 1), 0)
    o_ref[...] = x + jnp.where(pos >= padf, out, 0.0)


def _merge(hp, gain, a, b, wg, bm, wpa, wpb, wo, lp, padf):
    n, d = hp.shape

    def row(c):
        return pl.BlockSpec((TM_TOK, c), lambda i: (i, 0))

    return pl.pallas_call(
        functools.partial(_merge_body, lp=lp, padf=padf),
        out_shape=jax.ShapeDtypeStruct((n, d), F32),
        grid=(n // TM_TOK,),
        in_specs=[row(d), _resident((1, d)), row(GLA_VAL), row(ATT_Q), _resident(wg.shape),
                  _resident(bm.shape), _resident(wpa.shape), _resident(wpb.shape),
                  _resident(wo.shape)],
        out_specs=row(d),
        compiler_params=pltpu.CompilerParams(
            dimension_semantics=("parallel",), vmem_limit_bytes=VMEM_LIMIT),
        name="merge",
    )(hp, gain, a, b, wg, bm, wpa, wpb, wo)


def _final_body(h_ref, g_ref, o_ref):
    o_ref[...] = _rmsnorm_rows(h_ref[...], g_ref[...])


def _final(hp, gain, nb, n_tok, lp, skip):
    d = hp.shape[1]
    assert n_tok % TM_FIN == 0
    per_out = n_tok // TM_FIN
    return pl.pallas_call(
        _final_body,
        out_shape=jax.ShapeDtypeStruct((nb * n_tok, d), F32),
        grid=(nb, per_out),
        in_specs=[pl.BlockSpec((pl.Element(TM_FIN), pl.Element(d)),
                               lambda b, i: (pl.multiple_of(b * lp + skip + i * TM_FIN, LANES), 0)),
                  _resident((1, d))],
        out_specs=pl.BlockSpec((TM_FIN, d), lambda b, i: (b * per_out + i, 0)),
        compiler_params=pltpu.CompilerParams(
            dimension_semantics=("parallel", "parallel"), vmem_limit_bytes=VMEM_LIMIT),
        name="final_norm",
    )(hp, gain)


def _rope_tables(n_tok, lp, padf):
    quarter = HEAD_DIM // 4
    pos = jnp.arange(n_tok)
    inv = ROPE_THETA ** (-jnp.arange(0, HEAD_DIM // 2, 2, dtype=F32) / (HEAD_DIM // 2))
    ang_r = (pos // GRID_W).astype(F32)[:, None] * inv
    ang_c = (pos % GRID_W).astype(F32)[:, None] * inv
    cos = jnp.concatenate([jnp.cos(ang_r)] * 2 + [jnp.cos(ang_c)] * 2, axis=-1)
    sin = jnp.concatenate([-jnp.sin(ang_r), jnp.sin(ang_r), -jnp.sin(ang_c), jnp.sin(ang_c)],
                          axis=-1)
    lead = padf + N_META
    cos = jnp.concatenate([jnp.ones((lead, HEAD_DIM), F32), cos], axis=0)
    sin = jnp.concatenate([jnp.zeros((lead, HEAD_DIM), F32), sin], axis=0)
    assert cos.shape == (lp, 4 * quarter)
    return jnp.tile(cos, (1, 2)), jnp.tile(sin, (1, 2))


def _prep_w_in(w_in):
    sizes = (GLA_KEY, GLA_KEY, GLA_VAL, GLA_VAL, GLA_RANK, GLA_RANK, ATT_Q, ATT_KV, ATT_KV)
    offs = np.concatenate([[0], np.cumsum(sizes)])
    qa, ka, va, ra, lrf, lrb, qb, kb, vb = (w_in[:, offs[i]:offs[i + 1]] for i in range(9))

    def spread(w, twice):
        heads = [w[:, h * HEAD_DIM:(h + 1) * HEAD_DIM] for h in range(ATT_KV_HEADS)]
        return jnp.concatenate([x for h in heads for x in (h, h if twice else jnp.zeros_like(h))],
                               axis=-1)

    lr = jnp.concatenate([lrf, lrb, jnp.zeros((w_in.shape[0], LANES - 2 * GLA_RANK), F32)], -1)
    w_all = jnp.concatenate([qa, ka, va, ra, qb, spread(kb, True), lr], axis=-1).astype(BF16)
    return w_all, spread(vb, False).T.astype(BF16)


def _prep_gate(w2, b2):
    w = jnp.zeros((LANES, 2 * GLA_KEY), F32)
    w = w.at[:GLA_RANK, :GLA_KEY].set(w2[0]).at[GLA_RANK:2 * GLA_RANK, GLA_KEY:].set(w2[1])
    return w.astype(BF16), b2.reshape(1, 2 * GLA_KEY)


def kernel(x, meta_tokens, norm_gains, ffn_w_gate, ffn_w_up, ffn_w_down, w_in, gla_w2, gla_b2,
           gla_gn, q_norm, k_norm, w_pa, w_pb, b_merge, w_out, final_norm):
    nb, n_tok, d = x.shape
    depth = w_in.shape[0]
    lp = -(-(n_tok + N_META) // LANES) * LANES
    padf = lp - n_tok - N_META
    assert (padf + N_META) % CHUNK == 0 and n_tok % GRID_W == 0

    meta = jnp.broadcast_to(meta_tokens[None].astype(x.dtype), (nb, N_META, d))
    hp = jnp.concatenate([jnp.zeros((nb, padf, d), x.dtype), meta, x], axis=1).reshape(nb * lp, d)
    cos_t, sin_t = _rope_tables(n_tok, lp, padf)

    g_split = w_in.shape[2] - 2 * d
    for l in range(depth):
        gains = norm_gains[l].reshape(3, 1, d)
        hp = _ffn(hp, gains[0], ffn_w_gate[l, 0].astype(BF16), ffn_w_up[l, 0].astype(BF16),
                  ffn_w_down[l, 0].astype(BF16))
        w2, b2 = _prep_gate(gla_w2[l], gla_b2[l])
        w_all, wvt = _prep_w_in(w_in[l, :, :g_split])
        qa, ka, va, sr, gf, gb, qb, kb, vt = _proj(
            hp, gains[1], w_all, wvt, w2, b2,
            jnp.tile(q_norm[l], 2).reshape(1, LANES), jnp.tile(k_norm[l], 2).reshape(1, LANES),
            cos_t, sin_t, lp)
        a = _gla(qa, ka, va, gf, gb, sr, gla_gn[l].reshape(1, GLA_VAL), lp)
        small = (jnp.max(jnp.abs(q_norm[l])) * jnp.max(jnp.abs(k_norm[l])) * ATT_SCORE_BOUND
                 <= ATT_SAFE_SCORE)
        b = lax.cond(small, functools.partial(_attn, lp=lp, padf=padf, online=False),
                     functools.partial(_attn, lp=lp, padf=padf, online=True), qb, kb, vt)
        hp = _merge(hp, gains[1], a, b, w_in[l, :, g_split:].astype(BF16),
                    b_merge[l].reshape(1, 2 * d), w_pa[l].astype(BF16), w_pb[l].astype(BF16),
                    w_out[l].astype(BF16), lp, padf)
        hp = _ffn(hp, gains[2], ffn_w_gate[l, 1].astype(BF16), ffn_w_up[l, 1].astype(BF16),
                  ffn_w_down[l, 1].astype(BF16))
    out = _final(hp, final_norm.reshape(1, d), nb, n_tok, lp, padf + N_META)
    return out.reshape(nb, n_tok, d)
```

```python
import functools

import jax
import jax.numpy as jnp
import numpy as np
from jax import lax
from jax.experimental import pallas as pl
from jax.experimental.pallas import tpu as pltpu

F32 = jnp.float32
BF16 = jnp.bfloat16

N_META = 16
GRID_W = 64
GLA_HEADS = 4
GLA_DK = 64
GLA_DV = 128
GLA_KEY = GLA_HEADS * GLA_DK
GLA_VAL = GLA_HEADS * GLA_DV
GLA_RANK = 16
GLA_TAU = 16.0
CHUNK = 64
ATT_HEADS = 8
ATT_KV_HEADS = 2
HEAD_DIM = 64
ATT_Q = ATT_HEADS * HEAD_DIM
ATT_KV = ATT_KV_HEADS * HEAD_DIM
ROPE_THETA = 10000.0
EPS = 1e-6

LANES = 128
MXU_COLS = 256
VMEM_LIMIT = 56 * 1024 * 1024

TM_FFN = 512
TF_FFN = 2816
TM_TOK = 384
TM_PROJ = 384
TQ_ATT = 384
TK_ATT = 512
TM_FIN = 1024
GLA_ROWS = 256
GLA_LEAD = 128


def _resident(shape, lead=()):
    tail = tuple(shape[len(lead):])
    return pl.BlockSpec((None,) * len(lead) + tail, lambda *_: tuple(lead) + (0,) * len(tail),
                        pipeline_mode=pl.Buffered(1))


def _rmsnorm_rows(x, gain):
    return x * lax.rsqrt(jnp.mean(x * x, axis=-1, keepdims=True) + EPS) * gain


def _dot(a, b):
    return jnp.dot(a, b, preferred_element_type=F32)


def _dot_nt(a, b):
    return lax.dot_general(a, b, (((1,), (1,)), ((), ())), preferred_element_type=F32)


def _dot_tn(a, b):
    return lax.dot_general(a, b, (((0,), (0,)), ((), ())), preferred_element_type=F32)


def _silu(x):
    return x * jax.nn.sigmoid(x)


def _ffn_body(x_ref, g_ref, wg_ref, wu_ref, wd_ref, o_ref):
    x = x_ref[...]
    xb = _rmsnorm_rows(x, g_ref[...]).astype(BF16)
    d_ff = wg_ref.shape[1]
    acc = jnp.zeros(x.shape, F32)
    for j in range(d_ff // TF_FFN):
        sl = slice(j * TF_FFN, (j + 1) * TF_FFN)
        gate = _dot(xb, wg_ref[:, sl])
        up = _dot(xb, wu_ref[:, sl])
        acc = acc + _dot((_silu(gate) * up).astype(BF16), wd_ref[sl, :])
    o_ref[...] = x + 0.5 * acc


def _ffn(hp, gain, wg, wu, wd, which):
    n, d = hp.shape
    d_ff = wg.shape[-1]
    assert n % TM_FFN == 0 and d_ff % TF_FFN == 0
    row = pl.BlockSpec((TM_FFN, d), lambda i: (i, 0))
    return pl.pallas_call(
        _ffn_body,
        out_shape=jax.ShapeDtypeStruct((n, d), F32),
        grid=(n // TM_FFN,),
        in_specs=[row, _resident((1, d)), _resident(wg.shape, which), _resident(wu.shape, which),
                  _resident(wd.shape, which)],
        out_specs=row,
        compiler_params=pltpu.CompilerParams(
            dimension_semantics=("parallel",), vmem_limit_bytes=VMEM_LIMIT),
        name="ffn",
    )(hp, gain, wg, wu, wd)


_C_QA = 0
_C_KA = _C_QA + GLA_KEY
_C_VA = _C_KA + GLA_KEY
_C_RA = _C_VA + GLA_VAL
_C_QB = _C_RA + GLA_VAL
_C_KB = _C_QB + ATT_Q
_C_LR = _C_KB + 2 * ATT_KV
_C_END = _C_LR + LANES
_Q_SCALE = HEAD_DIM ** -0.5 * float(np.log2(np.e))
ATT_SCORE_BOUND = 1.05 * HEAD_DIM * _Q_SCALE
ATT_SAFE_SCORE = 60.0


def _proj_body(h_ref, g_ref, w_ref, wvt_ref, w2_ref, b2_ref, gq_ref, gk_ref, cos_ref, sin_ref,
               qa_ref, ka_ref, va_ref, sr_ref, gf_ref, gb_ref, qb_ref, kb_ref, vt_ref):
    zb = _rmsnorm_rows(h_ref[...], g_ref[...]).astype(BF16)

    def proj(lo, hi):
        return _dot(zb, w_ref[:, lo:hi])

    cos = cos_ref[...]
    sin = sin_ref[...]
    lane = lax.broadcasted_iota(jnp.int32, cos.shape, 1)
    low_head = lane < HEAD_DIM
    first_half = (lane % (HEAD_DIM // 2)) < (HEAD_DIM // 4)

    def norm_rope(x, gain, scale):
        x2 = x * x
        s_lo = jnp.sum(jnp.where(low_head, x2, 0.0), axis=-1, keepdims=True)
        s_hi = jnp.sum(jnp.where(low_head, 0.0, x2), axis=-1, keepdims=True)
        inv = jnp.where(low_head, lax.rsqrt(s_lo * (1.0 / HEAD_DIM) + EPS),
                        lax.rsqrt(s_hi * (1.0 / HEAD_DIM) + EPS))
        y = x * inv * gain
        partner = jnp.where(first_half, pltpu.roll(y, LANES - HEAD_DIM // 4, 1),
                            pltpu.roll(y, HEAD_DIM // 4, 1))
        return (y * cos + partner * sin) * scale

    for base, width, out_ref, gain_ref, scale in ((_C_QB, ATT_Q, qb_ref, gq_ref, _Q_SCALE),
                                                  (_C_KB, 2 * ATT_KV, kb_ref, gk_ref, 1.0)):
        for c in range(width // MXU_COLS):
            x = proj(base + c * MXU_COLS, base + (c + 1) * MXU_COLS)
            for s in range(MXU_COLS // LANES):
                sl = slice(c * MXU_COLS + s * LANES, c * MXU_COLS + (s + 1) * LANES)
                out_ref[:, sl] = norm_rope(x[:, s * LANES:(s + 1) * LANES], gain_ref[...],
                                           scale).astype(BF16)
    lr = proj(_C_LR, _C_END).astype(BF16)
    pre = _dot(lr, w2_ref[...]) + b2_ref[...]
    lg = (jnp.minimum(pre, 0.0) - jnp.log(1.0 + jnp.exp(-jnp.abs(pre)))) * (1.0 / GLA_TAU)
    gf_ref[...] = lg[:, :GLA_KEY]
    gb_ref[...] = lg[:, GLA_KEY:]
    qa_ref[...] = (proj(_C_QA, _C_KA) * GLA_DK ** -0.5).astype(BF16)
    ka_ref[...] = proj(_C_KA, _C_VA).astype(BF16)
    va_ref[...] = proj(_C_VA, _C_RA).astype(BF16)
    sr_ref[...] = _silu(proj(_C_RA, _C_QB)).astype(BF16)
    vt = _dot_nt(wvt_ref[...], zb)
    row_i = lax.broadcasted_iota(jnp.int32, vt.shape, 0)
    vt_ref[...] = jnp.where(row_i % LANES < HEAD_DIM, vt, 1.0).astype(BF16)


def _proj(hp, gain, w_all, wvt, w2, b2, gq, gk, cos_t, sin_t, lp, layer):
    n, d = hp.shape
    assert lp % TM_PROJ == 0
    tiles_per_seq = lp // TM_PROJ

    def row(c):
        return pl.BlockSpec((TM_PROJ, c), lambda i: (i, 0))

    table = pl.BlockSpec((TM_PROJ, LANES), lambda i: (i % tiles_per_seq, 0))
    outs = [(GLA_KEY, BF16), (GLA_KEY, BF16), (GLA_VAL, BF16), (GLA_VAL, BF16),
            (GLA_KEY, F32), (GLA_KEY, F32), (ATT_Q, BF16), (2 * ATT_KV, BF16)]
    vt_rows = wvt.shape[1]
    return pl.pallas_call(
        _proj_body,
        out_shape=[jax.ShapeDtypeStruct((n, c), t) for c, t in outs]
        + [jax.ShapeDtypeStruct((vt_rows, n), BF16)],
        grid=(n // TM_PROJ,),
        in_specs=[row(d), _resident((1, d)), _resident(w_all.shape, (layer,)),
                  _resident(wvt.shape, (layer,)), _resident(w2.shape, (layer,)),
                  _resident(b2.shape, (layer,)), _resident((1, LANES)),
                  _resident((1, LANES)), table, table],
        out_specs=[row(c) for c, _ in outs]
        + [pl.BlockSpec((vt_rows, TM_PROJ), lambda i: (0, i))],
        compiler_params=pltpu.CompilerParams(
            dimension_semantics=("parallel",), vmem_limit_bytes=VMEM_LIMIT),
        name="proj",
    )(hp, gain, w_all, wvt, w2, b2, gq, gk, cos_t, sin_t)


def _gla_body(q_ref, k_ref, v_ref, gf_ref, gb_ref, sr_ref, gn_ref, o_ref, ob_ref, st_ref):
    lp = q_ref.shape[0]
    assert GLA_LEAD % CHUNK == 0 and (lp - GLA_LEAD) % GLA_ROWS == 0
    n_main = (lp - GLA_LEAD) // GLA_ROWS

    def block(start, rows_n, forward, final):
        nch = rows_n // CHUNK
        rows = pl.ds(start, rows_n)
        pos = lax.broadcasted_iota(jnp.int32, (rows_n, LANES), 0) % CHUNK
        lane = lax.broadcasted_iota(jnp.int32, (rows_n, LANES), 1)

        bc = (gf_ref if forward else gb_ref)[rows, :]
        shift = 1
        while shift < CHUNK:
            if forward:
                moved, ok = pltpu.roll(bc, shift, 0), pos >= shift
            else:
                moved, ok = pltpu.roll(bc, rows_n - shift, 0), pos < CHUNK - shift
            bc = bc + jnp.where(ok, moved, 0.0)
            shift *= 2
        edge = CHUNK - 1 if forward else 0
        bt3 = bc.reshape(nch, CHUNK, LANES)[:, edge:edge + 1, :]
        bt = jnp.broadcast_to(bt3, (nch, CHUNK, LANES)).reshape(rows_n, LANES)
        dec = jnp.exp(bt3)

        q = q_ref[rows, :].astype(F32)
        k = k_ref[rows, :].astype(F32)
        qd = q * jnp.exp(bc)
        ki = (k * jnp.exp(-bc)).astype(BF16)
        ke = (k * jnp.exp(bt - bc)).astype(BF16)

        r_i = lax.broadcasted_iota(jnp.int32, (rows_n, rows_n), 0)
        c_i = lax.broadcasted_iota(jnp.int32, (rows_n, rows_n), 1)
        same_chunk = (r_i // CHUNK) == (c_i // CHUNK)
        att_ok = same_chunk & ((r_i >= c_i) if forward else (r_i <= c_i))
        wide = (rows_n, nch * LANES)
        own_block = (lax.broadcasted_iota(jnp.int32, wide, 0) // CHUNK
                     == lax.broadcasted_iota(jnp.int32, wide, 1) // LANES)
        ke_blk = jnp.where(own_block, jnp.tile(ke, (1, nch)), jnp.zeros(wide, BF16))

        entering = []
        for h in range(2):
            hs = slice(h * GLA_DV, (h + 1) * GLA_DV)
            kv_t = _dot_tn(v_ref[rows, hs], ke_blk)
            st = st_ref[h]
            states = [None] * nch
            for c in (range(nch) if forward else range(nch - 1, -1, -1)):
                states[c] = st
                st = st * dec[c] + kv_t[:, c * LANES:(c + 1) * LANES]
            st_ref[h] = st
            entering.append(jnp.concatenate(states, axis=1).astype(BF16))

        for h in range(2):
            hs = slice(h * GLA_DV, (h + 1) * GLA_DV)
            qh = jnp.where((lane < GLA_DK) if h == 0 else (lane >= GLA_DK), qd, 0.0).astype(BF16)
            att = jnp.where(att_ok, _dot_nt(qh, ki), 0.0).astype(BF16)
            q_blk = jnp.where(own_block, jnp.tile(qh, (1, nch)), jnp.zeros(wide, BF16))
            o = _dot(att, v_ref[rows, hs]) + _dot_nt(q_blk, entering[h])
            if not final:
                ob_ref[rows, hs] = o
            else:
                o = o + ob_ref[rows, hs]
                o = o * lax.rsqrt(jnp.mean(o * o, axis=-1, keepdims=True) + EPS) * gn_ref[:, hs]
                o_ref[rows, hs] = (o * sr_ref[rows, hs].astype(F32)).astype(BF16)

    def main_start(j):
        return pl.multiple_of(GLA_LEAD + j * GLA_ROWS, CHUNK)

    st_ref[...] = jnp.zeros(st_ref.shape, F32)

    def bwd(i, carry):
        block(main_start(n_main - 1 - i), GLA_ROWS, False, False)
        return carry

    lax.fori_loop(0, n_main, bwd, 0, unroll=16)
    block(0, GLA_LEAD, False, False)

    st_ref[...] = jnp.zeros(st_ref.shape, F32)
    block(0, GLA_LEAD, True, True)

    def fwd(i, carry):
        block(main_start(i), GLA_ROWS, True, True)
        return carry

    lax.fori_loop(0, n_main, fwd, 0, unroll=16)


def _gla(qa, ka, va, gf, gb, sr, gn, lp):
    n = qa.shape[0]
    nb = n // lp
    pairs = GLA_HEADS // 2

    def blk(c):
        return pl.BlockSpec((lp, c), lambda b, p: (b, p))

    return pl.pallas_call(
        _gla_body,
        out_shape=jax.ShapeDtypeStruct((n, GLA_VAL), BF16),
        grid=(nb, pairs),
        in_specs=[blk(LANES), blk(LANES), blk(2 * GLA_DV), blk(LANES), blk(LANES),
                  blk(2 * GLA_DV), pl.BlockSpec((1, 2 * GLA_DV), lambda b, p: (0, p))],
        out_specs=blk(2 * GLA_DV),
        scratch_shapes=[pltpu.VMEM((lp, 2 * GLA_DV), F32),
                        pltpu.VMEM((2, GLA_DV, LANES), F32)],
        compiler_params=pltpu.CompilerParams(
            dimension_semantics=("parallel", "parallel"), vmem_limit_bytes=VMEM_LIMIT),
        name="gla",
    )(qa, ka, va, gf, gb, sr, gn)


def _attn_body(q_ref, k_ref, vt_ref, o_ref, *s_ref, padf, online):
    tq = q_ref.shape[0]
    lp = k_ref.shape[0]
    group = ATT_HEADS // ATT_KV_HEADS
    cols = group * tq
    n_main = (lp - LANES) // TK_ATT
    low = lax.broadcasted_iota(jnp.int32, (tq, LANES), 1) < HEAD_DIM
    if online:
        (s_ref,) = s_ref

    for kvh in range(ATT_KV_HEADS):
        hs = slice(kvh * LANES, (kvh + 1) * LANES)
        parts = []
        for s in range(group // 2):
            slab = kvh * (group // 2) + s
            q = q_ref[:, slab * LANES:(slab + 1) * LANES]
            zero = jnp.zeros_like(q)
            parts += [jnp.where(low, q, zero), jnp.where(low, zero, q)]
        qs = jnp.concatenate(parts, axis=0)

        def scores(start, size, qs=qs, hs=hs):
            return _dot_nt(k_ref[pl.ds(start, size), hs], qs)

        def absorb(s, start, size, m, acc, hs=hs):
            if not online:
                return m, acc + _dot(vt_ref[hs, pl.ds(start, size)], jnp.exp2(s).astype(BF16))
            m_new = jnp.maximum(m, jnp.max(s, axis=0, keepdims=True))
            p = jnp.exp2(s - m_new).astype(BF16)
            pv = _dot(vt_ref[hs, pl.ds(start, size)], p)
            return m_new, acc * jnp.exp2(m - m_new) + pv

        def main_start(j):
            return pl.multiple_of(LANES + j * TK_ATT, LANES)

        key_ok = lax.broadcasted_iota(jnp.int32, (LANES, cols), 0) >= padf
        m, acc = absorb(jnp.where(key_ok, scores(0, LANES), -1e30), 0, LANES,
                        jnp.full((1, cols), -jnp.inf, F32), jnp.zeros((LANES, cols), F32))

        assert n_main % 2 == 0
        if online:
            s_ref[0] = scores(main_start(0), TK_ATT)

        def pair(i, carry, last):
            m, acc = carry
            s_ref[1] = scores(main_start(2 * i + 1), TK_ATT)
            m, acc = absorb(s_ref[0], main_start(2 * i), TK_ATT, m, acc)
            if not last:
                s_ref[0] = scores(main_start(2 * i + 2), TK_ATT)
            return absorb(s_ref[1], main_start(2 * i + 1), TK_ATT, m, acc)

        if online:
            m, acc = lax.fori_loop(0, n_main // 2 - 1, lambda i, c: pair(i, c, False), (m, acc))
            _, acc = pair(n_main // 2 - 1, (m, acc), True)
        else:
            def direct(j, acc):
                _, acc = absorb(scores(main_start(j), TK_ATT), main_start(j), TK_ATT, None, acc)
                return acc

            acc = lax.fori_loop(0, n_main, direct, acc, unroll=8)
        out_t = acc[:HEAD_DIM] / acc[HEAD_DIM:]
        for s in range(group // 2):
            slab = kvh * (group // 2) + s
            pair_t = jnp.concatenate([out_t[:, (2 * s) * tq:(2 * s + 1) * tq],
                                      out_t[:, (2 * s + 1) * tq:(2 * s + 2) * tq]], axis=0)
            o_ref[:, slab * LANES:(slab + 1) * LANES] = pair_t.T.astype(BF16)


def _attn(qb, kb, vt, lp, padf, online):
    n = qb.shape[0]
    nb = n // lp
    assert lp % TQ_ATT == 0 and (lp - LANES) % TK_ATT == 0
    nq = lp // TQ_ATT
    q_spec = pl.BlockSpec((TQ_ATT, ATT_Q), lambda b, i: (b * nq + i, 0))
    return pl.pallas_call(
        functools.partial(_attn_body, padf=padf, online=online),
        out_shape=jax.ShapeDtypeStruct((n, ATT_Q), BF16),
        grid=(nb, nq),
        in_specs=[q_spec, pl.BlockSpec((lp, kb.shape[1]), lambda b, i: (b, 0)),
                  pl.BlockSpec((vt.shape[0], lp), lambda b, i: (0, b))],
        out_specs=q_spec,
        scratch_shapes=([pltpu.VMEM((2, TK_ATT, ATT_HEADS // ATT_KV_HEADS * TQ_ATT), F32)]
                        if online else []),
        compiler_params=pltpu.CompilerParams(
            dimension_semantics=("parallel", "arbitrary"), vmem_limit_bytes=VMEM_LIMIT),
        name="attn",
    )(qb, kb, vt)


def _merge_body(h_ref, g_ref, a_ref, b_ref, wg_ref, bm_ref, wpa_ref, wpb_ref, wo_ref, o_ref,
                *, lp, padf):
    x = h_ref[...]
    d = x.shape[1]
    pa = _dot(a_ref[...], wpa_ref[...])
    pb = _dot(b_ref[...], wpb_ref[...])
    zb = _rmsnorm_rows(x, g_ref[...]).astype(BF16)
    gate_a = jax.nn.sigmoid(_dot(zb, wg_ref[:, :d]) + bm_ref[:, :d])
    gate_b = jax.nn.sigmoid(_dot(zb, wg_ref[:, d:]) + bm_ref[:, d:])
    out = _dot((gate_a * pa + gate_b * pb).astype(BF16), wo_ref[...])
    tm = x.shape[0]
    pos = (pl.program_id(0) % (lp // tm)) * tm + lax.broadcasted_iota(jnp.int32, (tm, 1), 0)
    o_ref[...] = x + jnp.where(pos >= padf, out, 0.0)


def _merge(hp, gain, a, b, wg, bm, wpa, wpb, wo, lp, padf, layer):
    n, d = hp.shape
    lead = (layer,)

    def row(c):
        return pl.BlockSpec((TM_TOK, c), lambda i: (i, 0))

    return pl.pallas_call(
        functools.partial(_merge_body, lp=lp, padf=padf),
        out_shape=jax.ShapeDtypeStruct((n, d), F32),
        grid=(n // TM_TOK,),
        in_specs=[row(d), _resident((1, d)), row(GLA_VAL), row(ATT_Q), _resident(wg.shape, lead),
                  _resident(bm.shape, lead), _resident(wpa.shape, lead),
                  _resident(wpb.shape, lead), _resident(wo.shape, lead)],
        out_specs=row(d),
        compiler_params=pltpu.CompilerParams(
            dimension_semantics=("parallel",), vmem_limit_bytes=VMEM_LIMIT),
        name="merge",
    )(hp, gain, a, b, wg, bm, wpa, wpb, wo)


def _final_body(h_ref, g_ref, o_ref):
    o_ref[...] = _rmsnorm_rows(h_ref[...], g_ref[...])


def _final(hp, gain, nb, n_tok, lp, skip):
    d = hp.shape[1]
    assert n_tok % TM_FIN == 0
    per_out = n_tok // TM_FIN
    return pl.pallas_call(
        _final_body,
        out_shape=jax.ShapeDtypeStruct((nb * n_tok, d), F32),
        grid=(nb, per_out),
        in_specs=[pl.BlockSpec((pl.Element(TM_FIN), pl.Element(d)),
                               lambda b, i: (pl.multiple_of(b * lp + skip + i * TM_FIN, LANES), 0)),
                  _resident((1, d))],
        out_specs=pl.BlockSpec((TM_FIN, d), lambda b, i: (b * per_out + i, 0)),
        compiler_params=pltpu.CompilerParams(
            dimension_semantics=("parallel", "parallel"), vmem_limit_bytes=VMEM_LIMIT),
        name="final_norm",
    )(hp, gain)


def _rope_tables(n_tok, lp, padf):
    quarter = HEAD_DIM // 4
    pos = jnp.arange(n_tok)
    inv = ROPE_THETA ** (-jnp.arange(0, HEAD_DIM // 2, 2, dtype=F32) / (HEAD_DIM // 2))
    ang_r = (pos // GRID_W).astype(F32)[:, None] * inv
    ang_c = (pos % GRID_W).astype(F32)[:, None] * inv
    cos = jnp.concatenate([jnp.cos(ang_r)] * 2 + [jnp.cos(ang_c)] * 2, axis=-1)
    sin = jnp.concatenate([-jnp.sin(ang_r), jnp.sin(ang_r), -jnp.sin(ang_c), jnp.sin(ang_c)],
                          axis=-1)
    lead = padf + N_META
    cos = jnp.concatenate([jnp.ones((lead, HEAD_DIM), F32), cos], axis=0)
    sin = jnp.concatenate([jnp.zeros((lead, HEAD_DIM), F32), sin], axis=0)
    assert cos.shape == (lp, 4 * quarter)
    return jnp.tile(cos, (1, 2)), jnp.tile(sin, (1, 2))


def _prep_w_in(w_in):
    sizes = (GLA_KEY, GLA_KEY, GLA_VAL, GLA_VAL, GLA_RANK, GLA_RANK, ATT_Q, ATT_KV, ATT_KV)
    offs = np.concatenate([[0], np.cumsum(sizes)])
    qa, ka, va, ra, lrf, lrb, qb, kb, vb = (w_in[..., offs[i]:offs[i + 1]] for i in range(9))

    def spread(w, twice):
        heads = [w[..., h * HEAD_DIM:(h + 1) * HEAD_DIM] for h in range(ATT_KV_HEADS)]
        return jnp.concatenate([x for h in heads for x in (h, h if twice else jnp.zeros_like(h))],
                               axis=-1)

    lr = jnp.concatenate([lrf, lrb, jnp.zeros(lrf.shape[:-1] + (LANES - 2 * GLA_RANK,), F32)], -1)
    w_all = jnp.concatenate([qa, ka, va, ra, qb, spread(kb, True), lr], axis=-1).astype(BF16)
    return w_all, jnp.swapaxes(spread(vb, False), -1, -2).astype(BF16)


def _prep_gate(w2, b2):
    depth = w2.shape[0]
    w = jnp.zeros((depth, LANES, 2 * GLA_KEY), F32)
    w = w.at[:, :GLA_RANK, :GLA_KEY].set(w2[:, 0]).at[:, GLA_RANK:2 * GLA_RANK, GLA_KEY:].set(w2[:, 1])
    return w.astype(BF16), b2.reshape(depth, 1, 2 * GLA_KEY)


def kernel(x, meta_tokens, norm_gains, ffn_w_gate, ffn_w_up, ffn_w_down, w_in, gla_w2, gla_b2,
           gla_gn, q_norm, k_norm, w_pa, w_pb, b_merge, w_out, final_norm):
    nb, n_tok, d = x.shape
    depth = w_in.shape[0]
    lp = -(-(n_tok + N_META) // LANES) * LANES
    padf = lp - n_tok - N_META
    assert (padf + N_META) % CHUNK == 0 and n_tok % GRID_W == 0

    meta = jnp.broadcast_to(meta_tokens[None].astype(x.dtype), (nb, N_META, d))
    hp = jnp.concatenate([jnp.zeros((nb, padf, d), x.dtype), meta, x], axis=1).reshape(nb * lp, d)
    cos_t, sin_t = _rope_tables(n_tok, lp, padf)

    g_split = w_in.shape[2] - 2 * d
    ffn_w = tuple(w.astype(BF16) for w in (ffn_w_gate, ffn_w_up, ffn_w_down))
    w_all, wvt = _prep_w_in(w_in[:, :, :g_split])
    w2, b2 = _prep_gate(gla_w2, gla_b2)
    merge_w = (w_in[:, :, g_split:].astype(BF16), b_merge.reshape(depth, 1, 2 * d),
               w_pa.astype(BF16), w_pb.astype(BF16), w_out.astype(BF16))
    for l in range(depth):
        gains = norm_gains[l].reshape(3, 1, d)
        hp = _ffn(hp, gains[0], *ffn_w, (l, 0))
        qa, ka, va, sr, gf, gb, qb, kb, vt = _proj(
            hp, gains[1], w_all, wvt, w2, b2,
            jnp.tile(q_norm[l], 2).reshape(1, LANES), jnp.tile(k_norm[l], 2).reshape(1, LANES),
            cos_t, sin_t, lp, l)
        a = _gla(qa, ka, va, gf, gb, sr, gla_gn[l].reshape(1, GLA_VAL), lp)
        small = (jnp.max(jnp.abs(q_norm[l])) * jnp.max(jnp.abs(k_norm[l])) * ATT_SCORE_BOUND
                 <= ATT_SAFE_SCORE)
        b = lax.cond(small, functools.partial(_attn, lp=lp, padf=padf, online=False),
                     functools.partial(_attn, lp=lp, padf=padf, online=True), qb, kb, vt)
        hp = _merge(hp, gains[1], a, b, *merge_w, lp, padf, l)
        hp = _ffn(hp, gains[2], *ffn_w, (l, 1))
    out = _final(hp, final_norm.reshape(1, d), nb, n_tok, lp, padf + N_META)
    return out.reshape(nb, n_tok, d)
```

```python
import functools

import jax
import jax.numpy as jnp
import numpy as np
from jax import lax
from jax.experimental import pallas as pl
from jax.experimental.pallas import tpu as pltpu

F32 = jnp.float32
BF16 = jnp.bfloat16

N_META = 16
GRID_W = 64
GLA_HEADS = 4
GLA_DK = 64
GLA_DV = 128
GLA_KEY = GLA_HEADS * GLA_DK
GLA_VAL = GLA_HEADS * GLA_DV
GLA_RANK = 16
GLA_TAU = 16.0
CHUNK = 64
ATT_HEADS = 8
ATT_KV_HEADS = 2
HEAD_DIM = 64
ATT_Q = ATT_HEADS * HEAD_DIM
ATT_KV = ATT_KV_HEADS * HEAD_DIM
ROPE_THETA = 10000.0
EPS = 1e-6

LANES = 128
MXU_COLS = 256
VMEM_LIMIT = 56 * 1024 * 1024

TM_FFN = 768
TS_FFN = 384
TM_TOK = 1056
TS_MERGE = 352
TM_PROJ = 1408
TS_PROJ = 384
TQ_ATT = 384
TK_ATT = 512
TM_FIN = 1024
GLA_ROWS = 256
GLA_LEAD = 128


def _resident(shape, lead=()):
    tail = tuple(shape[len(lead):])
    return pl.BlockSpec((None,) * len(lead) + tail, lambda *_: tuple(lead) + (0,) * len(tail),
                        pipeline_mode=pl.Buffered(1))


def _rmsnorm_rows(x, gain):
    return x * lax.rsqrt(jnp.mean(x * x, axis=-1, keepdims=True) + EPS) * gain


def _dot(a, b):
    return jnp.dot(a, b, preferred_element_type=F32)


def _dot_nt(a, b):
    return lax.dot_general(a, b, (((1,), (1,)), ((), ())), preferred_element_type=F32)


def _dot_tn(a, b):
    return lax.dot_general(a, b, (((0,), (0,)), ((), ())), preferred_element_type=F32)


def _silu(x):
    return x * jax.nn.sigmoid(x)


def _ffn_body(x_ref, g_ref, wg_ref, wu_ref, wd_ref, o_ref):
    for r in range(x_ref.shape[0] // TS_FFN):
        rows = slice(r * TS_FFN, (r + 1) * TS_FFN)
        x = x_ref[rows, :]
        xb = _rmsnorm_rows(x, g_ref[...]).astype(BF16)
        gate = _dot(xb, wg_ref[...])
        up = _dot(xb, wu_ref[...])
        o_ref[rows, :] = x + 0.5 * _dot((_silu(gate) * up).astype(BF16), wd_ref[...])


def _ffn(hp, gain, wg, wu, wd, which):
    n, d = hp.shape
    d_ff = wg.shape[-1]
    assert n % TM_FFN == 0 and TM_FFN % TS_FFN == 0 and d_ff % MXU_COLS == 0
    row = pl.BlockSpec((TM_FFN, d), lambda i: (i, 0))
    return pl.pallas_call(
        _ffn_body,
        out_shape=jax.ShapeDtypeStruct((n, d), F32),
        grid=(n // TM_FFN,),
        in_specs=[row, _resident((1, d)), _resident(wg.shape, which), _resident(wu.shape, which),
                  _resident(wd.shape, which)],
        out_specs=row,
        compiler_params=pltpu.CompilerParams(
            dimension_semantics=("parallel",), vmem_limit_bytes=VMEM_LIMIT),
        name="ffn",
    )(hp, gain, wg, wu, wd)


_C_QA = 0
_C_KA = _C_QA + GLA_KEY
_C_VA = _C_KA + GLA_KEY
_C_RA = _C_VA + GLA_VAL
_C_QB = _C_RA + GLA_VAL
_C_KB = _C_QB + ATT_Q
_C_LR = _C_KB + 2 * ATT_KV
_C_END = _C_LR + LANES
_Q_SCALE = HEAD_DIM ** -0.5 * float(np.log2(np.e))
ATT_SCORE_BOUND = 1.05 * HEAD_DIM * _Q_SCALE
ATT_SAFE_SCORE = 60.0


def _proj_body(h_ref, g_ref, w_ref, wvt_ref, w2_ref, b2_ref, gq_ref, gk_ref, cos_ref, sin_ref,
               qa_ref, ka_ref, va_ref, sr_ref, gf_ref, gb_ref, qb_ref, kb_ref, vt_ref):
    tm = h_ref.shape[0]
    for lo_row in range(0, tm, TS_PROJ):
        _proj_rows(slice(lo_row, min(lo_row + TS_PROJ, tm)), h_ref, g_ref, w_ref, wvt_ref, w2_ref,
                   b2_ref, gq_ref, gk_ref, cos_ref, sin_ref, qa_ref, ka_ref, va_ref, sr_ref,
                   gf_ref, gb_ref, qb_ref, kb_ref, vt_ref)


def _proj_rows(rows, h_ref, g_ref, w_ref, wvt_ref, w2_ref, b2_ref, gq_ref, gk_ref, cos_ref,
               sin_ref, qa_ref, ka_ref, va_ref, sr_ref, gf_ref, gb_ref, qb_ref, kb_ref, vt_ref):
    zb = _rmsnorm_rows(h_ref[rows, :], g_ref[...]).astype(BF16)

    def proj(lo, hi):
        return _dot(zb, w_ref[:, lo:hi])

    cos = cos_ref[rows, :]
    sin = sin_ref[rows, :]
    lane = lax.broadcasted_iota(jnp.int32, cos.shape, 1)
    low_head = lane < HEAD_DIM
    first_half = (lane % (HEAD_DIM // 2)) < (HEAD_DIM // 4)

    def norm_rope(x, gain, scale):
        x2 = x * x
        s_lo = jnp.sum(jnp.where(low_head, x2, 0.0), axis=-1, keepdims=True)
        s_hi = jnp.sum(jnp.where(low_head, 0.0, x2), axis=-1, keepdims=True)
        inv = jnp.where(low_head, lax.rsqrt(s_lo * (1.0 / HEAD_DIM) + EPS),
                        lax.rsqrt(s_hi * (1.0 / HEAD_DIM) + EPS))
        y = x * inv * gain
        partner = jnp.where(first_half, pltpu.roll(y, LANES - HEAD_DIM // 4, 1),
                            pltpu.roll(y, HEAD_DIM // 4, 1))
        return (y * cos + partner * sin) * scale

    for base, width, out_ref, gain_ref, scale in ((_C_QB, ATT_Q, qb_ref, gq_ref, _Q_SCALE),
                                                  (_C_KB, 2 * ATT_KV, kb_ref, gk_ref, 1.0)):
        for c in range(width // MXU_COLS):
            x = proj(base + c * MXU_COLS, base + (c + 1) * MXU_COLS)
            for s in range(MXU_COLS // LANES):
                sl = slice(c * MXU_COLS + s * LANES, c * MXU_COLS + (s + 1) * LANES)
                out_ref[rows, sl] = norm_rope(x[:, s * LANES:(s + 1) * LANES], gain_ref[...],
                                              scale).astype(BF16)
    lr = proj(_C_LR, _C_END).astype(BF16)
    pre = _dot(lr, w2_ref[...]) + b2_ref[...]
    lg = (jnp.minimum(pre, 0.0) - jnp.log(1.0 + jnp.exp(-jnp.abs(pre)))) * (1.0 / GLA_TAU)
    gf_ref[rows, :] = lg[:, :GLA_KEY]
    gb_ref[rows, :] = lg[:, GLA_KEY:]
    qa_ref[rows, :] = (proj(_C_QA, _C_KA) * GLA_DK ** -0.5).astype(BF16)
    ka_ref[rows, :] = proj(_C_KA, _C_VA).astype(BF16)
    va_ref[rows, :] = proj(_C_VA, _C_RA).astype(BF16)
    sr_ref[rows, :] = _silu(proj(_C_RA, _C_QB)).astype(BF16)
    vt = _dot_nt(wvt_ref[...], zb)
    row_i = lax.broadcasted_iota(jnp.int32, vt.shape, 0)
    vt_ref[:, rows] = jnp.where(row_i % LANES < HEAD_DIM, vt, 1.0).astype(BF16)


def _proj(hp, gain, w_all, wvt, w2, b2, gq, gk, cos_t, sin_t, lp, layer):
    n, d = hp.shape
    assert lp % TM_PROJ == 0
    tiles_per_seq = lp // TM_PROJ

    def row(c):
        return pl.BlockSpec((TM_PROJ, c), lambda i: (i, 0))

    table = pl.BlockSpec((TM_PROJ, LANES), lambda i: (i % tiles_per_seq, 0))
    outs = [(GLA_KEY, BF16), (GLA_KEY, BF16), (GLA_VAL, BF16), (GLA_VAL, BF16),
            (GLA_KEY, F32), (GLA_KEY, F32), (ATT_Q, BF16), (2 * ATT_KV, BF16)]
    vt_rows = wvt.shape[1]
    return pl.pallas_call(
        _proj_body,
        out_shape=[jax.ShapeDtypeStruct((n, c), t) for c, t in outs]
        + [jax.ShapeDtypeStruct((vt_rows, n), BF16)],
        grid=(n // TM_PROJ,),
        in_specs=[row(d), _resident((1, d)), _resident(w_all.shape, (layer,)),
                  _resident(wvt.shape, (layer,)), _resident(w2.shape, (layer,)),
                  _resident(b2.shape, (layer,)), _resident((1, LANES)),
                  _resident((1, LANES)), table, table],
        out_specs=[row(c) for c, _ in outs]
        + [pl.BlockSpec((vt_rows, TM_PROJ), lambda i: (0, i))],
        compiler_params=pltpu.CompilerParams(
            dimension_semantics=("parallel",), vmem_limit_bytes=VMEM_LIMIT),
        name="proj",
    )(hp, gain, w_all, wvt, w2, b2, gq, gk, cos_t, sin_t)


def _gla_body(q_ref, k_ref, v_ref, gf_ref, gb_ref, sr_ref, gn_ref, o_ref, ob_ref, st_ref):
    lp = q_ref.shape[0]
    assert GLA_LEAD % CHUNK == 0 and (lp - GLA_LEAD) % GLA_ROWS == 0
    n_main = (lp - GLA_LEAD) // GLA_ROWS

    def block(start, rows_n, forward, final):
        nch = rows_n // CHUNK
        rows = pl.ds(start, rows_n)
        pos = lax.broadcasted_iota(jnp.int32, (rows_n, LANES), 0) % CHUNK
        lane = lax.broadcasted_iota(jnp.int32, (rows_n, LANES), 1)

        bc = (gf_ref if forward else gb_ref)[rows, :]
        shift = 1
        while shift < CHUNK:
            if forward:
                moved, ok = pltpu.roll(bc, shift, 0), pos >= shift
            else:
                moved, ok = pltpu.roll(bc, rows_n - shift, 0), pos < CHUNK - shift
            bc = bc + jnp.where(ok, moved, 0.0)
            shift *= 2
        edge = CHUNK - 1 if forward else 0
        bt3 = bc.reshape(nch, CHUNK, LANES)[:, edge:edge + 1, :]
        bt = jnp.broadcast_to(bt3, (nch, CHUNK, LANES)).reshape(rows_n, LANES)
        dec = jnp.exp(bt3)

        q = q_ref[rows, :].astype(F32)
        k = k_ref[rows, :].astype(F32)
        qd = q * jnp.exp(bc)
        ki = (k * jnp.exp(-bc)).astype(BF16)
        ke = (k * jnp.exp(bt - bc)).astype(BF16)

        r_i = lax.broadcasted_iota(jnp.int32, (rows_n, rows_n), 0)
        c_i = lax.broadcasted_iota(jnp.int32, (rows_n, rows_n), 1)
        same_chunk = (r_i // CHUNK) == (c_i // CHUNK)
        att_ok = same_chunk & ((r_i >= c_i) if forward else (r_i <= c_i))
        wide = (rows_n, nch * LANES)
        own_block = (lax.broadcasted_iota(jnp.int32, wide, 0) // CHUNK
                     == lax.broadcasted_iota(jnp.int32, wide, 1) // LANES)
        ke_blk = jnp.where(own_block, jnp.tile(ke, (1, nch)), jnp.zeros(wide, BF16))

        entering = []
        for h in range(2):
            hs = slice(h * GLA_DV, (h + 1) * GLA_DV)
            kv_t = _dot_tn(v_ref[rows, hs], ke_blk)
            st = st_ref[h]
            states = [None] * nch
            for c in (range(nch) if forward else range(nch - 1, -1, -1)):
                states[c] = st
                st = st * dec[c] + kv_t[:, c * LANES:(c + 1) * LANES]
            st_ref[h] = st
            entering.append(jnp.concatenate(states, axis=1).astype(BF16))

        for h in range(2):
            hs = slice(h * GLA_DV, (h + 1) * GLA_DV)
            qh = jnp.where((lane < GLA_DK) if h == 0 else (lane >= GLA_DK), qd, 0.0).astype(BF16)
            att = jnp.where(att_ok, _dot_nt(qh, ki), 0.0).astype(BF16)
            q_blk = jnp.where(own_block, jnp.tile(qh, (1, nch)), jnp.zeros(wide, BF16))
            o = _dot(att, v_ref[rows, hs]) + _dot_nt(q_blk, entering[h])
            if not final:
                ob_ref[rows, hs] = o
            else:
                o = o + ob_ref[rows, hs]
                o = o * lax.rsqrt(jnp.mean(o * o, axis=-1, keepdims=True) + EPS) * gn_ref[:, hs]
                o_ref[rows, hs] = (o * sr_ref[rows, hs].astype(F32)).astype(BF16)

    def main_start(j):
        return pl.multiple_of(GLA_LEAD + j * GLA_ROWS, CHUNK)

    st_ref[...] = jnp.zeros(st_ref.shape, F32)

    def bwd(i, carry):
        block(main_start(n_main - 1 - i), GLA_ROWS, False, False)
        return carry

    lax.fori_loop(0, n_main, bwd, 0, unroll=16)
    block(0, GLA_LEAD, False, False)

    st_ref[...] = jnp.zeros(st_ref.shape, F32)
    block(0, GLA_LEAD, True, True)

    def fwd(i, carry):
        block(main_start(i), GLA_ROWS, True, True)
        return carry

    lax.fori_loop(0, n_main, fwd, 0, unroll=16)


def _gla(qa, ka, va, gf, gb, sr, gn, lp):
    n = qa.shape[0]
    nb = n // lp
    pairs = GLA_HEADS // 2

    def blk(c):
        return pl.BlockSpec((lp, c), lambda b, p: (b, p))

    return pl.pallas_call(
        _gla_body,
        out_shape=jax.ShapeDtypeStruct((n, GLA_VAL), BF16),
        grid=(nb, pairs),
        in_specs=[blk(LANES), blk(LANES), blk(2 * GLA_DV), blk(LANES), blk(LANES),
                  blk(2 * GLA_DV), pl.BlockSpec((1, 2 * GLA_DV), lambda b, p: (0, p))],
        out_specs=blk(2 * GLA_DV),
        scratch_shapes=[pltpu.VMEM((lp, 2 * GLA_DV), F32),
                        pltpu.VMEM((2, GLA_DV, LANES), F32)],
        compiler_params=pltpu.CompilerParams(
            dimension_semantics=("parallel", "parallel"), vmem_limit_bytes=VMEM_LIMIT),
        name="gla",
    )(qa, ka, va, gf, gb, sr, gn)


def _attn_body(q_ref, k_ref, vt_ref, o_ref, *s_ref, padf, online):
    tq = q_ref.shape[0]
    lp = k_ref.shape[0]
    group = ATT_HEADS // ATT_KV_HEADS
    cols = group * tq
    n_main = (lp - LANES) // TK_ATT
    low = lax.broadcasted_iota(jnp.int32, (tq, LANES), 1) < HEAD_DIM
    if online:
        (s_ref,) = s_ref

    for kvh in range(ATT_KV_HEADS):
        hs = slice(kvh * LANES, (kvh + 1) * LANES)
        parts = []
        for s in range(group // 2):
            slab = kvh * (group // 2) + s
            q = q_ref[:, slab * LANES:(slab + 1) * LANES]
            zero = jnp.zeros_like(q)
            parts += [jnp.where(low, q, zero), jnp.where(low, zero, q)]
        qs = jnp.concatenate(parts, axis=0)

        def scores(start, size, qs=qs, hs=hs):
            return _dot_nt(k_ref[pl.ds(start, size), hs], qs)

        def absorb(s, start, size, m, acc, hs=hs):
            if not online:
                return m, acc + _dot(vt_ref[hs, pl.ds(start, size)], jnp.exp2(s).astype(BF16))
            m_new = jnp.maximum(m, jnp.max(s, axis=0, keepdims=True))
            p = jnp.exp2(s - m_new).astype(BF16)
            pv = _dot(vt_ref[hs, pl.ds(start, size)], p)
            return m_new, acc * jnp.exp2(m - m_new) + pv

        def main_start(j):
            return pl.multiple_of(LANES + j * TK_ATT, LANES)

        def masked(s):
            return jnp.where(lax.broadcasted_iota(jnp.int32, s.shape, 0) >= padf, s, -1e30)

        acc = jnp.zeros((LANES, cols), F32)
        if online:
            m, acc = absorb(masked(scores(0, LANES)), 0, LANES,
                            jnp.full((1, cols), -jnp.inf, F32), acc)
            assert n_main % 2 == 0
            s_ref[0] = scores(main_start(0), TK_ATT)

            def pair(i, carry, last):
                m, acc = carry
                s_ref[1] = scores(main_start(2 * i + 1), TK_ATT)
                m, acc = absorb(s_ref[0], main_start(2 * i), TK_ATT, m, acc)
                if not last:
                    s_ref[0] = scores(main_start(2 * i + 2), TK_ATT)
                return absorb(s_ref[1], main_start(2 * i + 1), TK_ATT, m, acc)

            m, acc = lax.fori_loop(0, n_main // 2 - 1, lambda i, c: pair(i, c, False), (m, acc))
            _, acc = pair(n_main // 2 - 1, (m, acc), True)
        else:
            first = LANES + TK_ATT
            _, acc = absorb(masked(scores(0, first)), 0, first, None, acc)

            def direct(j, acc):
                _, acc = absorb(scores(main_start(j), TK_ATT), main_start(j), TK_ATT, None, acc)
                return acc

            acc = lax.fori_loop(1, n_main, direct, acc, unroll=n_main - 1)
        out_t = acc[:HEAD_DIM] / acc[HEAD_DIM:]
        for s in range(group // 2):
            slab = kvh * (group // 2) + s
            pair_t = jnp.concatenate([out_t[:, (2 * s) * tq:(2 * s + 1) * tq],
                                      out_t[:, (2 * s + 1) * tq:(2 * s + 2) * tq]], axis=0)
            o_ref[:, slab * LANES:(slab + 1) * LANES] = pair_t.T.astype(BF16)


def _attn(qb, kb, vt, lp, padf, online):
    n = qb.shape[0]
    nb = n // lp
    assert lp % TQ_ATT == 0 and (lp - LANES) % TK_ATT == 0
    nq = lp // TQ_ATT
    q_spec = pl.BlockSpec((TQ_ATT, ATT_Q), lambda b, i: (b * nq + i, 0))
    return pl.pallas_call(
        functools.partial(_attn_body, padf=padf, online=online),
        out_shape=jax.ShapeDtypeStruct((n, ATT_Q), BF16),
        grid=(nb, nq),
        in_specs=[q_spec, pl.BlockSpec((lp, kb.shape[1]), lambda b, i: (b, 0)),
                  pl.BlockSpec((vt.shape[0], lp), lambda b, i: (0, b))],
        out_specs=q_spec,
        scratch_shapes=([pltpu.VMEM((2, TK_ATT, ATT_HEADS // ATT_KV_HEADS * TQ_ATT), F32)]
                        if online else []),
        compiler_params=pltpu.CompilerParams(
            dimension_semantics=("parallel", "arbitrary"), vmem_limit_bytes=VMEM_LIMIT),
        name="attn",
    )(qb, kb, vt)


def _merge_body(h_ref, g_ref, a_ref, b_ref, wg_ref, bm_ref, wpa_ref, wpb_ref, wo_ref, o_ref,
                *, lp, padf):
    tm, d = h_ref.shape
    block_pos = (pl.program_id(0) % (lp // tm)) * tm
    for r in range(tm // TS_MERGE):
        rows = slice(r * TS_MERGE, (r + 1) * TS_MERGE)
        x = h_ref[rows, :]
        pa = _dot(a_ref[rows, :], wpa_ref[...])
        pb = _dot(b_ref[rows, :], wpb_ref[...])
        zb = _rmsnorm_rows(x, g_ref[...]).astype(BF16)
        gate_a = jax.nn.sigmoid(_dot(zb, wg_ref[:, :d]) + bm_ref[:, :d])
        gate_b = jax.nn.sigmoid(_dot(zb, wg_ref[:, d:]) + bm_ref[:, d:])
        out = _dot((gate_a * pa + gate_b * pb).astype(BF16), wo_ref[...])
        pos = block_pos + r * TS_MERGE + lax.broadcasted_iota(jnp.int32, (TS_MERGE, 1), 0)
        o_ref[rows, :] = x + jnp.where(pos >= padf, out, 0.0)


def _merge(hp, gain, a, b, wg, bm, wpa, wpb, wo, lp, padf, layer):
    n, d = hp.shape
    lead = (layer,)
    assert lp % TM_TOK == 0 and TM_TOK % TS_MERGE == 0

    def row(c):
        return pl.BlockSpec((TM_TOK, c), lambda i: (i, 0))

    return pl.pallas_call(
        functools.partial(_merge_body, lp=lp, padf=padf),
        out_shape=jax.ShapeDtypeStruct((n, d), F32),
        grid=(n // TM_TOK,),
        in_specs=[row(d), _resident((1, d)), row(GLA_VAL), row(ATT_Q), _resident(wg.shape, lead),
                  _resident(bm.shape, lead), _resident(wpa.shape, lead),
                  _resident(wpb.shape, lead), _resident(wo.shape, lead)],
        out_specs=row(d),
        compiler_params=pltpu.CompilerParams(
            dimension_semantics=("parallel",), vmem_limit_bytes=VMEM_LIMIT),
        name="merge",
    )(hp, gain, a, b, wg, bm, wpa, wpb, wo)


def _final_body(h_ref, g_ref, o_ref):
    o_ref[...] = _rmsnorm_rows(h_ref[...], g_ref[...])


def _final(hp, gain, nb, n_tok, lp, skip):
    d = hp.shape[1]
    assert n_tok % TM_FIN == 0
    per_out = n_tok // TM_FIN
    return pl.pallas_call(
        _final_body,
        out_shape=jax.ShapeDtypeStruct((nb * n_tok, d), F32),
        grid=(nb, per_out),
        in_specs=[pl.BlockSpec((pl.Element(TM_FIN), pl.Element(d)),
                               lambda b, i: (pl.multiple_of(b * lp + skip + i * TM_FIN, LANES), 0)),
                  _resident((1, d))],
        out_specs=pl.BlockSpec((TM_FIN, d), lambda b, i: (b * per_out + i, 0)),
        compiler_params=pltpu.CompilerParams(
            dimension_semantics=("parallel", "parallel"), vmem_limit_bytes=VMEM_LIMIT),
        name="final_norm",
    )(hp, gain)


def _rope_tables(n_tok, lp, padf):
    quarter = HEAD_DIM // 4
    pos = jnp.arange(n_tok)
    inv = ROPE_THETA ** (-jnp.arange(0, HEAD_DIM // 2, 2, dtype=F32) / (HEAD_DIM // 2))
    ang_r = (pos // GRID_W).astype(F32)[:, None] * inv
    ang_c = (pos % GRID_W).astype(F32)[:, None] * inv
    cos = jnp.concatenate([jnp.cos(ang_r)] * 2 + [jnp.cos(ang_c)] * 2, axis=-1)
    sin = jnp.concatenate([-jnp.sin(ang_r), jnp.sin(ang_r), -jnp.sin(ang_c), jnp.sin(ang_c)],
                          axis=-1)
    lead = padf + N_META
    cos = jnp.concatenate([jnp.ones((lead, HEAD_DIM), F32), cos], axis=0)
    sin = jnp.concatenate([jnp.zeros((lead, HEAD_DIM), F32), sin], axis=0)
    assert cos.shape == (lp, 4 * quarter)
    return jnp.tile(cos, (1, 2)), jnp.tile(sin, (1, 2))


def _prep_w_in(w_in):
    sizes = (GLA_KEY, GLA_KEY, GLA_VAL, GLA_VAL, GLA_RANK, GLA_RANK, ATT_Q, ATT_KV, ATT_KV)
    offs = np.concatenate([[0], np.cumsum(sizes)])
    qa, ka, va, ra, lrf, lrb, qb, kb, vb = (w_in[..., offs[i]:offs[i + 1]] for i in range(9))

    def spread(w, twice):
        heads = [w[..., h * HEAD_DIM:(h + 1) * HEAD_DIM] for h in range(ATT_KV_HEADS)]
        return jnp.concatenate([x for h in heads for x in (h, h if twice else jnp.zeros_like(h))],
                               axis=-1)

    lr = jnp.concatenate([lrf, lrb, jnp.zeros(lrf.shape[:-1] + (LANES - 2 * GLA_RANK,), F32)], -1)
    w_all = jnp.concatenate([qa, ka, va, ra, qb, spread(kb, True), lr], axis=-1).astype(BF16)
    return w_all, jnp.swapaxes(spread(vb, False), -1, -2).astype(BF16)


def _prep_gate(w2, b2):
    depth = w2.shape[0]
    w = jnp.zeros((depth, LANES, 2 * GLA_KEY), F32)
    w = w.at[:, :GLA_RANK, :GLA_KEY].set(w2[:, 0]).at[:, GLA_RANK:2 * GLA_RANK, GLA_KEY:].set(w2[:, 1])
    return w.astype(BF16), b2.reshape(depth, 1, 2 * GLA_KEY)


def kernel(x, meta_tokens, norm_gains, ffn_w_gate, ffn_w_up, ffn_w_down, w_in, gla_w2, gla_b2,
           gla_gn, q_norm, k_norm, w_pa, w_pb, b_merge, w_out, final_norm):
    nb, n_tok, d = x.shape
    depth = w_in.shape[0]
    lp = -(-(n_tok + N_META) // LANES) * LANES
    padf = lp - n_tok - N_META
    assert (padf + N_META) % CHUNK == 0 and n_tok % GRID_W == 0

    meta = jnp.broadcast_to(meta_tokens[None].astype(x.dtype), (nb, N_META, d))
    hp = jnp.concatenate([jnp.zeros((nb, padf, d), x.dtype), meta, x], axis=1).reshape(nb * lp, d)
    cos_t, sin_t = _rope_tables(n_tok, lp, padf)

    g_split = w_in.shape[2] - 2 * d
    ffn_w = tuple(w.astype(BF16) for w in (ffn_w_gate, ffn_w_up, ffn_w_down))
    w_all, wvt = _prep_w_in(w_in[:, :, :g_split])
    w2, b2 = _prep_gate(gla_w2, gla_b2)
    merge_w = (w_in[:, :, g_split:].astype(BF16), b_merge.reshape(depth, 1, 2 * d),
               w_pa.astype(BF16), w_pb.astype(BF16), w_out.astype(BF16))
    for l in range(depth):
        gains = norm_gains[l].reshape(3, 1, d)
        hp = _ffn(hp, gains[0], *ffn_w, (l, 0))
        qa, ka, va, sr, gf, gb, qb, kb, vt = _proj(
            hp, gains[1], w_all, wvt, w2, b2,
            jnp.tile(q_norm[l], 2).reshape(1, LANES), jnp.tile(k_norm[l], 2).reshape(1, LANES),
            cos_t, sin_t, lp, l)
        a = _gla(qa, ka, va, gf, gb, sr, gla_gn[l].reshape(1, GLA_VAL), lp)
        small = (jnp.max(jnp.abs(q_norm[l])) * jnp.max(jnp.abs(k_norm[l])) * ATT_SCORE_BOUND
                 <= ATT_SAFE_SCORE)
        b = lax.cond(small, functools.partial(_attn, lp=lp, padf=padf, online=False),
                     functools.partial(_attn, lp=lp, padf=padf, online=True), qb, kb, vt)
        hp = _merge(hp, gains[1], a, b, *merge_w, lp, padf, l)
        hp = _ffn(hp, gains[2], *ffn_w, (l, 1))
    out = _final(hp, final_norm.reshape(1, d), nb, n_tok, lp, padf + N_META)
    return out.reshape(nb, n_tok, d)
```

```python
import functools

import jax
import jax.numpy as jnp
import numpy as np
from jax import lax
from jax.experimental import pallas as pl
from jax.experimental.pallas import tpu as pltpu

F32 = jnp.float32
BF16 = jnp.bfloat16

N_META = 16
GRID_W = 64
GLA_HEADS = 4
GLA_DK = 64
GLA_DV = 128
GLA_KEY = GLA_HEADS * GLA_DK
GLA_VAL = GLA_HEADS * GLA_DV
GLA_RANK = 16
GLA_TAU = 16.0
CHUNK = 64
ATT_HEADS = 8
ATT_KV_HEADS = 2
HEAD_DIM = 64
ATT_Q = ATT_HEADS * HEAD_DIM
ATT_KV = ATT_KV_HEADS * HEAD_DIM
ROPE_THETA = 10000.0
EPS = 1e-6

LANES = 128
MXU_COLS = 256
VMEM_LIMIT = 56 * 1024 * 1024

TM_FFN = 768
TS_FFN = 384
TM_TOK = 1056
TS_MERGE = 352
TM_PROJ = 1408
TS_PROJ = 384
TQ_ATT = 1408
TK_ATT = 512
TM_FIN = 1024
GLA_ROWS = 256
GLA_LEAD = 128


def _resident(shape, lead=()):
    tail = tuple(shape[len(lead):])
    return pl.BlockSpec((None,) * len(lead) + tail, lambda *_: tuple(lead) + (0,) * len(tail),
                        pipeline_mode=pl.Buffered(1))


def _rmsnorm_rows(x, gain):
    return x * lax.rsqrt(jnp.mean(x * x, axis=-1, keepdims=True) + EPS) * gain


def _dot(a, b):
    return jnp.dot(a, b, preferred_element_type=F32)


def _dot_nt(a, b):
    return lax.dot_general(a, b, (((1,), (1,)), ((), ())), preferred_element_type=F32)


def _dot_tn(a, b):
    return lax.dot_general(a, b, (((0,), (0,)), ((), ())), preferred_element_type=F32)


def _silu(x):
    return x * jax.nn.sigmoid(x)


def _ffn_body(x_ref, g_ref, wg_ref, wu_ref, wd_ref, o_ref):
    for r in range(x_ref.shape[0] // TS_FFN):
        rows = slice(r * TS_FFN, (r + 1) * TS_FFN)
        x = x_ref[rows, :]
        xb = _rmsnorm_rows(x, g_ref[...]).astype(BF16)
        gate = _dot(xb, wg_ref[...])
        up = _dot(xb, wu_ref[...])
        o_ref[rows, :] = x + 0.5 * _dot((_silu(gate) * up).astype(BF16), wd_ref[...])


def _ffn(hp, gain, wg, wu, wd, which):
    n, d = hp.shape
    d_ff = wg.shape[-1]
    assert n % TM_FFN == 0 and TM_FFN % TS_FFN == 0 and d_ff % MXU_COLS == 0
    row = pl.BlockSpec((TM_FFN, d), lambda i: (i, 0))
    return pl.pallas_call(
        _ffn_body,
        out_shape=jax.ShapeDtypeStruct((n, d), F32),
        grid=(n // TM_FFN,),
        in_specs=[row, _resident((1, d)), _resident(wg.shape, which), _resident(wu.shape, which),
                  _resident(wd.shape, which)],
        out_specs=row,
        compiler_params=pltpu.CompilerParams(
            dimension_semantics=("parallel",), vmem_limit_bytes=VMEM_LIMIT),
        name="ffn",
    )(hp, gain, wg, wu, wd)


_C_QA = 0
_C_KA = _C_QA + GLA_KEY
_C_VA = _C_KA + GLA_KEY
_C_RA = _C_VA + GLA_VAL
_C_QB = _C_RA + GLA_VAL
_C_KB = _C_QB + ATT_Q
_C_LR = _C_KB + 2 * ATT_KV
_C_END = _C_LR + LANES
_Q_SCALE = HEAD_DIM ** -0.5 * float(np.log2(np.e))
ATT_SCORE_BOUND = 1.05 * HEAD_DIM * _Q_SCALE
ATT_SAFE_SCORE = 60.0


def _proj_body(h_ref, g_ref, w_ref, wvt_ref, w2_ref, b2_ref, gq_ref, gk_ref, cos_ref, sin_ref,
               qa_ref, ka_ref, va_ref, sr_ref, gf_ref, gb_ref, qb_ref, kb_ref, vt_ref):
    tm = h_ref.shape[0]
    for lo_row in range(0, tm, TS_PROJ):
        _proj_rows(slice(lo_row, min(lo_row + TS_PROJ, tm)), h_ref, g_ref, w_ref, wvt_ref, w2_ref,
                   b2_ref, gq_ref, gk_ref, cos_ref, sin_ref, qa_ref, ka_ref, va_ref, sr_ref,
                   gf_ref, gb_ref, qb_ref, kb_ref, vt_ref)


def _proj_rows(rows, h_ref, g_ref, w_ref, wvt_ref, w2_ref, b2_ref, gq_ref, gk_ref, cos_ref,
               sin_ref, qa_ref, ka_ref, va_ref, sr_ref, gf_ref, gb_ref, qb_ref, kb_ref, vt_ref):
    zb = _rmsnorm_rows(h_ref[rows, :], g_ref[...]).astype(BF16)

    def proj(lo, hi):
        return _dot(zb, w_ref[:, lo:hi])

    cos = cos_ref[rows, :]
    sin = sin_ref[rows, :]
    lane = lax.broadcasted_iota(jnp.int32, cos.shape, 1)
    low_head = lane < HEAD_DIM
    first_half = (lane % (HEAD_DIM // 2)) < (HEAD_DIM // 4)

    def norm_rope(x, gain, scale):
        x2 = x * x
        s_lo = jnp.sum(jnp.where(low_head, x2, 0.0), axis=-1, keepdims=True)
        s_hi = jnp.sum(jnp.where(low_head, 0.0, x2), axis=-1, keepdims=True)
        inv = jnp.where(low_head, lax.rsqrt(s_lo * (1.0 / HEAD_DIM) + EPS),
                        lax.rsqrt(s_hi * (1.0 / HEAD_DIM) + EPS))
        y = x * inv * gain
        partner = jnp.where(first_half, pltpu.roll(y, LANES - HEAD_DIM // 4, 1),
                            pltpu.roll(y, HEAD_DIM // 4, 1))
        return (y * cos + partner * sin) * scale

    for base, width, out_ref, gain_ref, scale in ((_C_QB, ATT_Q, qb_ref, gq_ref, _Q_SCALE),
                                                  (_C_KB, 2 * ATT_KV, kb_ref, gk_ref, 1.0)):
        for c in range(width // MXU_COLS):
            x = proj(base + c * MXU_COLS, base + (c + 1) * MXU_COLS)
            for s in range(MXU_COLS // LANES):
                sl = slice(c * MXU_COLS + s * LANES, c * MXU_COLS + (s + 1) * LANES)
                out_ref[rows, sl] = norm_rope(x[:, s * LANES:(s + 1) * LANES], gain_ref[...],
                                              scale).astype(BF16)
    lr = proj(_C_LR, _C_END).astype(BF16)
    pre = _dot(lr, w2_ref[...]) + b2_ref[...]
    lg = (jnp.minimum(pre, 0.0) - jnp.log(1.0 + jnp.exp(-jnp.abs(pre)))) * (1.0 / GLA_TAU)
    gf_ref[rows, :] = lg[:, :GLA_KEY]
    gb_ref[rows, :] = lg[:, GLA_KEY:]
    qa_ref[rows, :] = (proj(_C_QA, _C_KA) * GLA_DK ** -0.5).astype(BF16)
    ka_ref[rows, :] = proj(_C_KA, _C_VA).astype(BF16)
    va_ref[rows, :] = proj(_C_VA, _C_RA).astype(BF16)
    sr_ref[rows, :] = _silu(proj(_C_RA, _C_QB)).astype(BF16)
    vt = _dot_nt(wvt_ref[...], zb)
    row_i = lax.broadcasted_iota(jnp.int32, vt.shape, 0)
    vt_ref[:, rows] = jnp.where(row_i % LANES < HEAD_DIM, vt, 1.0).astype(BF16)


def _proj(hp, gain, w_all, wvt, w2, b2, gq, gk, cos_t, sin_t, lp, layer):
    n, d = hp.shape
    assert lp % TM_PROJ == 0
    tiles_per_seq = lp // TM_PROJ

    def row(c):
        return pl.BlockSpec((TM_PROJ, c), lambda i: (i, 0))

    table = pl.BlockSpec((TM_PROJ, LANES), lambda i: (i % tiles_per_seq, 0))
    outs = [(GLA_KEY, BF16), (GLA_KEY, BF16), (GLA_VAL, BF16), (GLA_VAL, BF16),
            (GLA_KEY, F32), (GLA_KEY, F32), (ATT_Q, BF16), (2 * ATT_KV, BF16)]
    vt_rows = wvt.shape[1]
    return pl.pallas_call(
        _proj_body,
        out_shape=[jax.ShapeDtypeStruct((n, c), t) for c, t in outs]
        + [jax.ShapeDtypeStruct((vt_rows, n), BF16)],
        grid=(n // TM_PROJ,),
        in_specs=[row(d), _resident((1, d)), _resident(w_all.shape, (layer,)),
                  _resident(wvt.shape, (layer,)), _resident(w2.shape, (layer,)),
                  _resident(b2.shape, (layer,)), _resident((1, LANES)),
                  _resident((1, LANES)), table, table],
        out_specs=[row(c) for c, _ in outs]
        + [pl.BlockSpec((vt_rows, TM_PROJ), lambda i: (0, i))],
        compiler_params=pltpu.CompilerParams(
            dimension_semantics=("parallel",), vmem_limit_bytes=VMEM_LIMIT),
        name="proj",
    )(hp, gain, w_all, wvt, w2, b2, gq, gk, cos_t, sin_t)


def _gla_body(q_ref, k_ref, v_ref, gf_ref, gb_ref, sr_ref, gn_ref, o_ref, ob_ref, st_ref):
    lp = q_ref.shape[0]
    assert GLA_LEAD % CHUNK == 0 and (lp - GLA_LEAD) % GLA_ROWS == 0
    n_main = (lp - GLA_LEAD) // GLA_ROWS

    def block(start, rows_n, forward, final):
        nch = rows_n // CHUNK
        rows = pl.ds(start, rows_n)
        pos = lax.broadcasted_iota(jnp.int32, (rows_n, LANES), 0) % CHUNK
        lane = lax.broadcasted_iota(jnp.int32, (rows_n, LANES), 1)

        bc = (gf_ref if forward else gb_ref)[rows, :]
        shift = 1
        while shift < CHUNK:
            if forward:
                moved, ok = pltpu.roll(bc, shift, 0), pos >= shift
            else:
                moved, ok = pltpu.roll(bc, rows_n - shift, 0), pos < CHUNK - shift
            bc = bc + jnp.where(ok, moved, 0.0)
            shift *= 2
        edge = CHUNK - 1 if forward else 0
        bt3 = bc.reshape(nch, CHUNK, LANES)[:, edge:edge + 1, :]
        bt = jnp.broadcast_to(bt3, (nch, CHUNK, LANES)).reshape(rows_n, LANES)
        dec = jnp.exp(bt3)

        q = q_ref[rows, :].astype(F32)
        k = k_ref[rows, :].astype(F32)
        qd = q * jnp.exp(bc)
        ki = (k * jnp.exp(-bc)).astype(BF16)
        ke = (k * jnp.exp(bt - bc)).astype(BF16)

        r_i = lax.broadcasted_iota(jnp.int32, (rows_n, rows_n), 0)
        c_i = lax.broadcasted_iota(jnp.int32, (rows_n, rows_n), 1)
        same_chunk = (r_i // CHUNK) == (c_i // CHUNK)
        att_ok = same_chunk & ((r_i >= c_i) if forward else (r_i <= c_i))
        wide = (rows_n, nch * LANES)
        own_block = (lax.broadcasted_iota(jnp.int32, wide, 0) // CHUNK
                     == lax.broadcasted_iota(jnp.int32, wide, 1) // LANES)
        ke_blk = jnp.where(own_block, jnp.tile(ke, (1, nch)), jnp.zeros(wide, BF16))

        entering = []
        for h in range(2):
            hs = slice(h * GLA_DV, (h + 1) * GLA_DV)
            kv_t = _dot_tn(v_ref[rows, hs], ke_blk)
            st = st_ref[h]
            states = [None] * nch
            for c in (range(nch) if forward else range(nch - 1, -1, -1)):
                states[c] = st
                st = st * dec[c] + kv_t[:, c * LANES:(c + 1) * LANES]
            st_ref[h] = st
            entering.append(jnp.concatenate(states, axis=1).astype(BF16))

        for h in range(2):
            hs = slice(h * GLA_DV, (h + 1) * GLA_DV)
            qh = jnp.where((lane < GLA_DK) if h == 0 else (lane >= GLA_DK), qd, 0.0).astype(BF16)
            att = jnp.where(att_ok, _dot_nt(qh, ki), 0.0).astype(BF16)
            q_blk = jnp.where(own_block, jnp.tile(qh, (1, nch)), jnp.zeros(wide, BF16))
            o = _dot(att, v_ref[rows, hs]) + _dot_nt(q_blk, entering[h])
            if not final:
                ob_ref[rows, hs] = o
            else:
                o = o + ob_ref[rows, hs]
                o = o * lax.rsqrt(jnp.mean(o * o, axis=-1, keepdims=True) + EPS) * gn_ref[:, hs]
                o_ref[rows, hs] = (o * sr_ref[rows, hs].astype(F32)).astype(BF16)

    def main_start(j):
        return pl.multiple_of(GLA_LEAD + j * GLA_ROWS, CHUNK)

    st_ref[...] = jnp.zeros(st_ref.shape, F32)

    def bwd(i, carry):
        block(main_start(n_main - 1 - i), GLA_ROWS, False, False)
        return carry

    lax.fori_loop(0, n_main, bwd, 0, unroll=16)
    block(0, GLA_LEAD, False, False)

    st_ref[...] = jnp.zeros(st_ref.shape, F32)
    block(0, GLA_LEAD, True, True)

    def fwd(i, carry):
        block(main_start(i), GLA_ROWS, True, True)
        return carry

    lax.fori_loop(0, n_main, fwd, 0, unroll=16)


def _gla(qa, ka, va, gf, gb, sr, gn, lp):
    n = qa.shape[0]
    nb = n // lp
    pairs = GLA_HEADS // 2

    def blk(c):
        return pl.BlockSpec((lp, c), lambda b, p: (b, p))

    return pl.pallas_call(
        _gla_body,
        out_shape=jax.ShapeDtypeStruct((n, GLA_VAL), BF16),
        grid=(nb, pairs),
        in_specs=[blk(LANES), blk(LANES), blk(2 * GLA_DV), blk(LANES), blk(LANES),
                  blk(2 * GLA_DV), pl.BlockSpec((1, 2 * GLA_DV), lambda b, p: (0, p))],
        out_specs=blk(2 * GLA_DV),
        scratch_shapes=[pltpu.VMEM((lp, 2 * GLA_DV), F32),
                        pltpu.VMEM((2, GLA_DV, LANES), F32)],
        compiler_params=pltpu.CompilerParams(
            dimension_semantics=("parallel", "parallel"), vmem_limit_bytes=VMEM_LIMIT),
        name="gla",
    )(qa, ka, va, gf, gb, sr, gn)


def _attn_body(q_ref, k_ref, vt_ref, o_ref, *s_ref, padf, online):
    tq = q_ref.shape[0]
    lp = k_ref.shape[0]
    group = ATT_HEADS // ATT_KV_HEADS
    cols = group * tq
    n_main = (lp - LANES) // TK_ATT
    low = lax.broadcasted_iota(jnp.int32, (tq, LANES), 1) < HEAD_DIM
    if online:
        (s_ref,) = s_ref

    for kvh in range(ATT_KV_HEADS):
        hs = slice(kvh * LANES, (kvh + 1) * LANES)
        parts = []
        for s in range(group // 2):
            slab = kvh * (group // 2) + s
            q = q_ref[:, slab * LANES:(slab + 1) * LANES]
            zero = jnp.zeros_like(q)
            parts += [jnp.where(low, q, zero), jnp.where(low, zero, q)]
        qs = jnp.concatenate(parts, axis=0)

        def scores(start, size, qs=qs, hs=hs):
            return _dot_nt(k_ref[pl.ds(start, size), hs], qs)

        def absorb(s, start, size, m, acc, hs=hs):
            if not online:
                return m, acc + _dot(vt_ref[hs, pl.ds(start, size)], jnp.exp2(s).astype(BF16))
            m_new = jnp.maximum(m, jnp.max(s, axis=0, keepdims=True))
            p = jnp.exp2(s - m_new).astype(BF16)
            pv = _dot(vt_ref[hs, pl.ds(start, size)], p)
            return m_new, acc * jnp.exp2(m - m_new) + pv

        def main_start(j):
            return pl.multiple_of(LANES + j * TK_ATT, LANES)

        def masked(s):
            return jnp.where(lax.broadcasted_iota(jnp.int32, s.shape, 0) >= padf, s, -1e30)

        acc = jnp.zeros((LANES, cols), F32)
        if online:
            m, acc = absorb(masked(scores(0, LANES)), 0, LANES,
                            jnp.full((1, cols), -jnp.inf, F32), acc)
            assert n_main % 2 == 0
            s_ref[0] = scores(main_start(0), TK_ATT)

            def pair(i, carry, last):
                m, acc = carry
                s_ref[1] = scores(main_start(2 * i + 1), TK_ATT)
                m, acc = absorb(s_ref[0], main_start(2 * i), TK_ATT, m, acc)
                if not last:
                    s_ref[0] = scores(main_start(2 * i + 2), TK_ATT)
                return absorb(s_ref[1], main_start(2 * i + 1), TK_ATT, m, acc)

            m, acc = lax.fori_loop(0, n_main // 2 - 1, lambda i, c: pair(i, c, False), (m, acc))
            _, acc = pair(n_main // 2 - 1, (m, acc), True)
        else:
            first = LANES + TK_ATT
            _, acc = absorb(masked(scores(0, first)), 0, first, None, acc)

            def direct(j, acc):
                _, acc = absorb(scores(main_start(j), TK_ATT), main_start(j), TK_ATT, None, acc)
                return acc

            acc = lax.fori_loop(1, n_main, direct, acc, unroll=n_main - 1)
        out_t = acc[:HEAD_DIM] / acc[HEAD_DIM:]
        for s in range(group // 2):
            slab = kvh * (group // 2) + s
            pair_t = jnp.concatenate([out_t[:, (2 * s) * tq:(2 * s + 1) * tq],
                                      out_t[:, (2 * s + 1) * tq:(2 * s + 2) * tq]], axis=0)
            o_ref[:, slab * LANES:(slab + 1) * LANES] = pair_t.T.astype(BF16)


def _attn(qb, kb, vt, lp, padf, online):
    n = qb.shape[0]
    nb = n // lp
    assert lp % TQ_ATT == 0 and (lp - LANES) % TK_ATT == 0
    nq = lp // TQ_ATT
    q_spec = pl.BlockSpec((TQ_ATT, ATT_Q), lambda b, i: (b * nq + i, 0))
    return pl.pallas_call(
        functools.partial(_attn_body, padf=padf, online=online),
        out_shape=jax.ShapeDtypeStruct((n, ATT_Q), BF16),
        grid=(nb, nq),
        in_specs=[q_spec, pl.BlockSpec((lp, kb.shape[1]), lambda b, i: (b, 0)),
                  pl.BlockSpec((vt.shape[0], lp), lambda b, i: (0, b))],
        out_specs=q_spec,
        scratch_shapes=([pltpu.VMEM((2, TK_ATT, ATT_HEADS // ATT_KV_HEADS * TQ_ATT), F32)]
                        if online else []),
        compiler_params=pltpu.CompilerParams(
            dimension_semantics=("parallel", "arbitrary"), vmem_limit_bytes=VMEM_LIMIT),
        name="attn",
    )(qb, kb, vt)


def _merge_body(h_ref, g_ref, a_ref, b_ref, wg_ref, bm_ref, wpa_ref, wpb_ref, wo_ref, o_ref,
                *, lp, padf):
    tm, d = h_ref.shape
    block_pos = (pl.program_id(0) % (lp // tm)) * tm
    for r in range(tm // TS_MERGE):
        rows = slice(r * TS_MERGE, (r + 1) * TS_MERGE)
        x = h_ref[rows, :]
        pa = _dot(a_ref[rows, :], wpa_ref[...])
        pb = _dot(b_ref[rows, :], wpb_ref[...])
        zb = _rmsnorm_rows(x, g_ref[...]).astype(BF16)
        gate_a = jax.nn.sigmoid(_dot(zb, wg_ref[:, :d]) + bm_ref[:, :d])
        gate_b = jax.nn.sigmoid(_dot(zb, wg_ref[:, d:]) + bm_ref[:, d:])
        out = _dot((gate_a * pa + gate_b * pb).astype(BF16), wo_ref[...])
        pos = block_pos + r * TS_MERGE + lax.broadcasted_iota(jnp.int32, (TS_MERGE, 1), 0)
        o_ref[rows, :] = x + jnp.where(pos >= padf, out, 0.0)


def _merge(hp, gain, a, b, wg, bm, wpa, wpb, wo, lp, padf, layer):
    n, d = hp.shape
    lead = (layer,)
    assert lp % TM_TOK == 0 and TM_TOK % TS_MERGE == 0

    def row(c):
        return pl.BlockSpec((TM_TOK, c), lambda i: (i, 0))

    return pl.pallas_call(
        functools.partial(_merge_body, lp=lp, padf=padf),
        out_shape=jax.ShapeDtypeStruct((n, d), F32),
        grid=(n // TM_TOK,),
        in_specs=[row(d), _resident((1, d)), row(GLA_VAL), row(ATT_Q), _resident(wg.shape, lead),
                  _resident(bm.shape, lead), _resident(wpa.shape, lead),
                  _resident(wpb.shape, lead), _resident(wo.shape, lead)],
        out_specs=row(d),
        compiler_params=pltpu.CompilerParams(
            dimension_semantics=("parallel",), vmem_limit_bytes=VMEM_LIMIT),
        name="merge",
    )(hp, gain, a, b, wg, bm, wpa, wpb, wo)


def _final_body(h_ref, g_ref, o_ref):
    o_ref[...] = _rmsnorm_rows(h_ref[...], g_ref[...])


def _final(hp, gain, nb, n_tok, lp, skip):
    d = hp.shape[1]
    assert n_tok % TM_FIN == 0
    per_out = n_tok // TM_FIN
    return pl.pallas_call(
        _final_body,
        out_shape=jax.ShapeDtypeStruct((nb * n_tok, d), F32),
        grid=(nb, per_out),
        in_specs=[pl.BlockSpec((pl.Element(TM_FIN), pl.Element(d)),
                               lambda b, i: (pl.multiple_of(b * lp + skip + i * TM_FIN, LANES), 0)),
                  _resident((1, d))],
        out_specs=pl.BlockSpec((TM_FIN, d), lambda b, i: (b * per_out + i, 0)),
        compiler_params=pltpu.CompilerParams(
            dimension_semantics=("parallel", "parallel"), vmem_limit_bytes=VMEM_LIMIT),
        name="final_norm",
    )(hp, gain)


def _rope_tables(n_tok, lp, padf):
    quarter = HEAD_DIM // 4
    pos = jnp.arange(n_tok)
    inv = ROPE_THETA ** (-jnp.arange(0, HEAD_DIM // 2, 2, dtype=F32) / (HEAD_DIM // 2))
    ang_r = (pos // GRID_W).astype(F32)[:, None] * inv
    ang_c = (pos % GRID_W).astype(F32)[:, None] * inv
    cos = jnp.concatenate([jnp.cos(ang_r)] * 2 + [jnp.cos(ang_c)] * 2, axis=-1)
    sin = jnp.concatenate([-jnp.sin(ang_r), jnp.sin(ang_r), -jnp.sin(ang_c), jnp.sin(ang_c)],
                          axis=-1)
    lead = padf + N_META
    cos = jnp.concatenate([jnp.ones((lead, HEAD_DIM), F32), cos], axis=0)
    sin = jnp.concatenate([jnp.zeros((lead, HEAD_DIM), F32), sin], axis=0)
    assert cos.shape == (lp, 4 * quarter)
    return jnp.tile(cos, (1, 2)), jnp.tile(sin, (1, 2))


def _prep_w_in(w_in):
    sizes = (GLA_KEY, GLA_KEY, GLA_VAL, GLA_VAL, GLA_RANK, GLA_RANK, ATT_Q, ATT_KV, ATT_KV)
    offs = np.concatenate([[0], np.cumsum(sizes)])
    qa, ka, va, ra, lrf, lrb, qb, kb, vb = (w_in[..., offs[i]:offs[i + 1]] for i in range(9))

    def spread(w, twice):
        heads = [w[..., h * HEAD_DIM:(h + 1) * HEAD_DIM] for h in range(ATT_KV_HEADS)]
        return jnp.concatenate([x for h in heads for x in (h, h if twice else jnp.zeros_like(h))],
                               axis=-1)

    lr = jnp.concatenate([lrf, lrb, jnp.zeros(lrf.shape[:-1] + (LANES - 2 * GLA_RANK,), F32)], -1)
    w_all = jnp.concatenate([qa, ka, va, ra, qb, spread(kb, True), lr], axis=-1).astype(BF16)
    return w_all, jnp.swapaxes(spread(vb, False), -1, -2).astype(BF16)


def _prep_gate(w2, b2):
    depth = w2.shape[0]
    w = jnp.zeros((depth, LANES, 2 * GLA_KEY), F32)
    w = w.at[:, :GLA_RANK, :GLA_KEY].set(w2[:, 0]).at[:, GLA_RANK:2 * GLA_RANK, GLA_KEY:].set(w2[:, 1])
    return w.astype(BF16), b2.reshape(depth, 1, 2 * GLA_KEY)


def kernel(x, meta_tokens, norm_gains, ffn_w_gate, ffn_w_up, ffn_w_down, w_in, gla_w2, gla_b2,
           gla_gn, q_norm, k_norm, w_pa, w_pb, b_merge, w_out, final_norm):
    nb, n_tok, d = x.shape
    depth = w_in.shape[0]
    lp = -(-(n_tok + N_META) // LANES) * LANES
    padf = lp - n_tok - N_META
    assert (padf + N_META) % CHUNK == 0 and n_tok % GRID_W == 0

    meta = jnp.broadcast_to(meta_tokens[None].astype(x.dtype), (nb, N_META, d))
    hp = jnp.concatenate([jnp.zeros((nb, padf, d), x.dtype), meta, x], axis=1).reshape(nb * lp, d)
    cos_t, sin_t = _rope_tables(n_tok, lp, padf)

    g_split = w_in.shape[2] - 2 * d
    ffn_w = tuple(w.astype(BF16) for w in (ffn_w_gate, ffn_w_up, ffn_w_down))
    w_all, wvt = _prep_w_in(w_in[:, :, :g_split])
    w2, b2 = _prep_gate(gla_w2, gla_b2)
    merge_w = (w_in[:, :, g_split:].astype(BF16), b_merge.reshape(depth, 1, 2 * d),
               w_pa.astype(BF16), w_pb.astype(BF16), w_out.astype(BF16))
    for l in range(depth):
        gains = norm_gains[l].reshape(3, 1, d)
        hp = _ffn(hp, gains[0], *ffn_w, (l, 0))
        qa, ka, va, sr, gf, gb, qb, kb, vt = _proj(
            hp, gains[1], w_all, wvt, w2, b2,
            jnp.tile(q_norm[l], 2).reshape(1, LANES), jnp.tile(k_norm[l], 2).reshape(1, LANES),
            cos_t, sin_t, lp, l)
        a = _gla(qa, ka, va, gf, gb, sr, gla_gn[l].reshape(1, GLA_VAL), lp)
        small = (jnp.max(jnp.abs(q_norm[l])) * jnp.max(jnp.abs(k_norm[l])) * ATT_SCORE_BOUND
                 <= ATT_SAFE_SCORE)
        b = lax.cond(small, functools.partial(_attn, lp=lp, padf=padf, online=False),
                     functools.partial(_attn, lp=lp, padf=padf, online=True), qb, kb, vt)
        hp = _merge(hp, gains[1], a, b, *merge_w, lp, padf, l)
        hp = _ffn(hp, gains[2], *ffn_w, (l, 1))
    out = _final(hp, final_norm.reshape(1, d), nb, n_tok, lp, padf + N_META)
    return out.reshape(nb, n_tok, d)
```

```python
import functools

import jax
import jax.numpy as jnp
import numpy as np
from jax import lax
from jax.experimental import pallas as pl
from jax.experimental.pallas import tpu as pltpu

F32 = jnp.float32
BF16 = jnp.bfloat16

N_META = 16
GRID_W = 64
GLA_HEADS = 4
GLA_DK = 64
GLA_DV = 128
GLA_KEY = GLA_HEADS * GLA_DK
GLA_VAL = GLA_HEADS * GLA_DV
GLA_RANK = 16
GLA_TAU = 16.0
CHUNK = 64
ATT_HEADS = 8
ATT_KV_HEADS = 2
HEAD_DIM = 64
ATT_Q = ATT_HEADS * HEAD_DIM
ATT_KV = ATT_KV_HEADS * HEAD_DIM
ROPE_THETA = 10000.0
EPS = 1e-6

LANES = 128
MXU_COLS = 256
VMEM_LIMIT = 56 * 1024 * 1024

TM_FFN = 768
TS_FFN = 384
TM_TOK = 1056
TS_MERGE = 352
TM_PROJ = 1408
TS_PROJ = 512
TQ_ATT = 384
TK_ATT = 512
TM_FIN = 1024
GLA_ROWS = 256
GLA_LEAD = 128
GLA_UNROLL = 8


def _resident(shape, lead=()):
    tail = tuple(shape[len(lead):])
    return pl.BlockSpec((None,) * len(lead) + tail, lambda *_: tuple(lead) + (0,) * len(tail),
                        pipeline_mode=pl.Buffered(1))


def _rmsnorm_rows(x, gain):
    return x * lax.rsqrt(jnp.mean(x * x, axis=-1, keepdims=True) + EPS) * gain


def _dot(a, b):
    return jnp.dot(a, b, preferred_element_type=F32)


def _dot_nt(a, b):
    return lax.dot_general(a, b, (((1,), (1,)), ((), ())), preferred_element_type=F32)


def _dot_tn(a, b):
    return lax.dot_general(a, b, (((0,), (0,)), ((), ())), preferred_element_type=F32)


def _silu(x):
    return x * jax.nn.sigmoid(x)


def _ffn_body(x_ref, g_ref, wg_ref, wu_ref, wd_ref, o_ref):
    for r in range(x_ref.shape[0] // TS_FFN):
        rows = slice(r * TS_FFN, (r + 1) * TS_FFN)
        x = x_ref[rows, :]
        xb = _rmsnorm_rows(x, g_ref[...]).astype(BF16)
        gate = _dot(xb, wg_ref[...])
        up = _dot(xb, wu_ref[...])
        o_ref[rows, :] = x + 0.5 * _dot((_silu(gate) * up).astype(BF16), wd_ref[...])


def _ffn(hp, gain, wg, wu, wd, which):
    n, d = hp.shape
    d_ff = wg.shape[-1]
    assert n % TM_FFN == 0 and TM_FFN % TS_FFN == 0 and d_ff % MXU_COLS == 0
    row = pl.BlockSpec((TM_FFN, d), lambda i: (i, 0))
    return pl.pallas_call(
        _ffn_body,
        out_shape=jax.ShapeDtypeStruct((n, d), F32),
        grid=(n // TM_FFN,),
        in_specs=[row, _resident((1, d)), _resident(wg.shape, which), _resident(wu.shape, which),
                  _resident(wd.shape, which)],
        out_specs=row,
        compiler_params=pltpu.CompilerParams(
            dimension_semantics=("parallel",), vmem_limit_bytes=VMEM_LIMIT),
        name="ffn",
    )(hp, gain, wg, wu, wd)


_C_QA = 0
_C_KA = _C_QA + GLA_KEY
_C_VA = _C_KA + GLA_KEY
_C_RA = _C_VA + GLA_VAL
_C_QB = _C_RA + GLA_VAL
_C_KB = _C_QB + ATT_Q
_C_LR = _C_KB + 2 * ATT_KV
_C_END = _C_LR + LANES
_Q_SCALE = HEAD_DIM ** -0.5 * float(np.log2(np.e))
ATT_SCORE_BOUND = 1.05 * HEAD_DIM * _Q_SCALE
ATT_SAFE_SCORE = 60.0


def _proj_body(h_ref, g_ref, w_ref, wvt_ref, w2_ref, b2_ref, gq_ref, gk_ref, cos_ref, sin_ref,
               qa_ref, ka_ref, va_ref, sr_ref, gf_ref, gb_ref, qb_ref, kb_ref, vt_ref):
    tm = h_ref.shape[0]
    for lo_row in range(0, tm, TS_PROJ):
        _proj_rows(slice(lo_row, min(lo_row + TS_PROJ, tm)), h_ref, g_ref, w_ref, wvt_ref, w2_ref,
                   b2_ref, gq_ref, gk_ref, cos_ref, sin_ref, qa_ref, ka_ref, va_ref, sr_ref,
                   gf_ref, gb_ref, qb_ref, kb_ref, vt_ref)


def _proj_rows(rows, h_ref, g_ref, w_ref, wvt_ref, w2_ref, b2_ref, gq_ref, gk_ref, cos_ref,
               sin_ref, qa_ref, ka_ref, va_ref, sr_ref, gf_ref, gb_ref, qb_ref, kb_ref, vt_ref):
    zb = _rmsnorm_rows(h_ref[rows, :], g_ref[...]).astype(BF16)

    def proj(lo, hi):
        return _dot(zb, w_ref[:, lo:hi])

    cos = cos_ref[rows, :]
    sin = sin_ref[rows, :]
    lane = lax.broadcasted_iota(jnp.int32, cos.shape, 1)
    low_head = lane < HEAD_DIM
    first_half = (lane % (HEAD_DIM // 2)) < (HEAD_DIM // 4)

    def norm_rope(x, gain, scale):
        x2 = x * x
        s_lo = jnp.sum(jnp.where(low_head, x2, 0.0), axis=-1, keepdims=True)
        s_hi = jnp.sum(jnp.where(low_head, 0.0, x2), axis=-1, keepdims=True)
        inv = jnp.where(low_head, lax.rsqrt(s_lo * (1.0 / HEAD_DIM) + EPS),
                        lax.rsqrt(s_hi * (1.0 / HEAD_DIM) + EPS))
        y = x * inv * gain
        partner = jnp.where(first_half, pltpu.roll(y, LANES - HEAD_DIM // 4, 1),
                            pltpu.roll(y, HEAD_DIM // 4, 1))
        return (y * cos + partner * sin) * scale

    for base, width, out_ref, gain_ref, scale in ((_C_QB, ATT_Q, qb_ref, gq_ref, _Q_SCALE),
                                                  (_C_KB, 2 * ATT_KV, kb_ref, gk_ref, 1.0)):
        for c in range(width // MXU_COLS):
            x = proj(base + c * MXU_COLS, base + (c + 1) * MXU_COLS)
            for s in range(MXU_COLS // LANES):
                sl = slice(c * MXU_COLS + s * LANES, c * MXU_COLS + (s + 1) * LANES)
                out_ref[rows, sl] = norm_rope(x[:, s * LANES:(s + 1) * LANES], gain_ref[...],
                                              scale).astype(BF16)
    lr = proj(_C_LR, _C_END).astype(BF16)
    pre = _dot(lr, w2_ref[...]) + b2_ref[...]
    lg = (jnp.minimum(pre, 0.0) - jnp.log(1.0 + jnp.exp(-jnp.abs(pre)))) * (1.0 / GLA_TAU)
    gf_ref[rows, :] = lg[:, :GLA_KEY]
    gb_ref[rows, :] = lg[:, GLA_KEY:]
    qa_ref[rows, :] = (proj(_C_QA, _C_KA) * GLA_DK ** -0.5).astype(BF16)
    ka_ref[rows, :] = proj(_C_KA, _C_VA).astype(BF16)
    va_ref[rows, :] = proj(_C_VA, _C_RA).astype(BF16)
    sr_ref[rows, :] = _silu(proj(_C_RA, _C_QB)).astype(BF16)
    vt = _dot_nt(wvt_ref[...], zb)
    row_i = lax.broadcasted_iota(jnp.int32, vt.shape, 0)
    vt_ref[:, rows] = jnp.where(row_i % LANES < HEAD_DIM, vt, 1.0).astype(BF16)


def _proj(hp, gain, w_all, wvt, w2, b2, gq, gk, cos_t, sin_t, lp, layer):
    n, d = hp.shape
    assert lp % TM_PROJ == 0
    tiles_per_seq = lp // TM_PROJ

    def row(c):
        return pl.BlockSpec((TM_PROJ, c), lambda i: (i, 0))

    table = pl.BlockSpec((TM_PROJ, LANES), lambda i: (i % tiles_per_seq, 0))
    outs = [(GLA_KEY, BF16), (GLA_KEY, BF16), (GLA_VAL, BF16), (GLA_VAL, BF16),
            (GLA_KEY, F32), (GLA_KEY, F32), (ATT_Q, BF16), (2 * ATT_KV, BF16)]
    vt_rows = wvt.shape[1]
    return pl.pallas_call(
        _proj_body,
        out_shape=[jax.ShapeDtypeStruct((n, c), t) for c, t in outs]
        + [jax.ShapeDtypeStruct((vt_rows, n), BF16)],
        grid=(n // TM_PROJ,),
        in_specs=[row(d), _resident((1, d)), _resident(w_all.shape, (layer,)),
                  _resident(wvt.shape, (layer,)), _resident(w2.shape, (layer,)),
                  _resident(b2.shape, (layer,)), _resident((1, LANES)),
                  _resident((1, LANES)), table, table],
        out_specs=[row(c) for c, _ in outs]
        + [pl.BlockSpec((vt_rows, TM_PROJ), lambda i: (0, i))],
        compiler_params=pltpu.CompilerParams(
            dimension_semantics=("parallel",), vmem_limit_bytes=VMEM_LIMIT),
        name="proj",
    )(hp, gain, w_all, wvt, w2, b2, gq, gk, cos_t, sin_t)


def _gla_body(q_ref, k_ref, v_ref, gf_ref, gb_ref, sr_ref, gn_ref, o_ref, ob_ref, st_ref):
    lp = q_ref.shape[0]
    assert GLA_LEAD % CHUNK == 0 and (lp - GLA_LEAD) % GLA_ROWS == 0
    n_main = (lp - GLA_LEAD) // GLA_ROWS

    def block(start, rows_n, forward, final):
        nch = rows_n // CHUNK
        rows = pl.ds(start, rows_n)
        pos = lax.broadcasted_iota(jnp.int32, (rows_n, LANES), 0) % CHUNK
        lane = lax.broadcasted_iota(jnp.int32, (rows_n, LANES), 1)

        bc = (gf_ref if forward else gb_ref)[rows, :]
        shift = 1
        while shift < CHUNK:
            if forward:
                moved, ok = pltpu.roll(bc, shift, 0), pos >= shift
            else:
                moved, ok = pltpu.roll(bc, rows_n - shift, 0), pos < CHUNK - shift
            bc = bc + jnp.where(ok, moved, 0.0)
            shift *= 2
        edge = CHUNK - 1 if forward else 0
        bt3 = bc.reshape(nch, CHUNK, LANES)[:, edge:edge + 1, :]
        bt = jnp.broadcast_to(bt3, (nch, CHUNK, LANES)).reshape(rows_n, LANES)
        dec = jnp.exp(bt3)

        q = q_ref[rows, :].astype(F32)
        k = k_ref[rows, :].astype(F32)
        qd = q * jnp.exp(bc)
        ki = (k * jnp.exp(-bc)).astype(BF16)
        ke = (k * jnp.exp(bt - bc)).astype(BF16)

        r_i = lax.broadcasted_iota(jnp.int32, (rows_n, rows_n), 0)
        c_i = lax.broadcasted_iota(jnp.int32, (rows_n, rows_n), 1)
        same_chunk = (r_i // CHUNK) == (c_i // CHUNK)
        att_ok = same_chunk & ((r_i >= c_i) if forward else (r_i <= c_i))
        wide = (rows_n, nch * LANES)
        own_block = (lax.broadcasted_iota(jnp.int32, wide, 0) // CHUNK
                     == lax.broadcasted_iota(jnp.int32, wide, 1) // LANES)
        ke_blk = jnp.where(own_block, jnp.tile(ke, (1, nch)), jnp.zeros(wide, BF16))

        entering = []
        for h in range(2):
            hs = slice(h * GLA_DV, (h + 1) * GLA_DV)
            kv_t = _dot_tn(v_ref[rows, hs], ke_blk)
            st = st_ref[h]
            states = [None] * nch
            for c in (range(nch) if forward else range(nch - 1, -1, -1)):
                states[c] = st
                st = st * dec[c] + kv_t[:, c * LANES:(c + 1) * LANES]
            st_ref[h] = st
            entering.append(jnp.concatenate(states, axis=1).astype(BF16))

        for h in range(2):
            hs = slice(h * GLA_DV, (h + 1) * GLA_DV)
            qh = jnp.where((lane < GLA_DK) if h == 0 else (lane >= GLA_DK), qd, 0.0).astype(BF16)
            att = jnp.where(att_ok, _dot_nt(qh, ki), 0.0).astype(BF16)
            q_blk = jnp.where(own_block, jnp.tile(qh, (1, nch)), jnp.zeros(wide, BF16))
            o = _dot(att, v_ref[rows, hs]) + _dot_nt(q_blk, entering[h])
            if not final:
                ob_ref[rows, hs] = o
            else:
                o = o + ob_ref[rows, hs]
                o = o * lax.rsqrt(jnp.mean(o * o, axis=-1, keepdims=True) + EPS) * gn_ref[:, hs]
                o_ref[rows, hs] = (o * sr_ref[rows, hs].astype(F32)).astype(BF16)

    def main_start(j):
        return pl.multiple_of(GLA_LEAD + j * GLA_ROWS, CHUNK)

    st_ref[...] = jnp.zeros(st_ref.shape, F32)

    def bwd(i, carry):
        block(main_start(n_main - 1 - i), GLA_ROWS, False, False)
        return carry

    lax.fori_loop(0, n_main, bwd, 0, unroll=GLA_UNROLL)
    block(0, GLA_LEAD, False, False)

    st_ref[...] = jnp.zeros(st_ref.shape, F32)
    block(0, GLA_LEAD, True, True)

    def fwd(i, carry):
        block(main_start(i), GLA_ROWS, True, True)
        return carry

    lax.fori_loop(0, n_main, fwd, 0, unroll=GLA_UNROLL)


def _gla(qa, ka, va, gf, gb, sr, gn, lp):
    n = qa.shape[0]
    nb = n // lp
    pairs = GLA_HEADS // 2

    def blk(c):
        return pl.BlockSpec((lp, c), lambda b, p: (b, p))

    return pl.pallas_call(
        _gla_body,
        out_shape=jax.ShapeDtypeStruct((n, GLA_VAL), BF16),
        grid=(nb, pairs),
        in_specs=[blk(LANES), blk(LANES), blk(2 * GLA_DV), blk(LANES), blk(LANES),
                  blk(2 * GLA_DV), pl.BlockSpec((1, 2 * GLA_DV), lambda b, p: (0, p))],
        out_specs=blk(2 * GLA_DV),
        scratch_shapes=[pltpu.VMEM((lp, 2 * GLA_DV), F32),
                        pltpu.VMEM((2, GLA_DV, LANES), F32)],
        compiler_params=pltpu.CompilerParams(
            dimension_semantics=("parallel", "parallel"), vmem_limit_bytes=VMEM_LIMIT),
        name="gla",
    )(qa, ka, va, gf, gb, sr, gn)


def _attn_body(q_ref, k_ref, vt_ref, o_ref, *s_ref, padf, online):
    tq = q_ref.shape[0]
    lp = k_ref.shape[0]
    group = ATT_HEADS // ATT_KV_HEADS
    cols = group * tq
    n_main = (lp - LANES) // TK_ATT
    low = lax.broadcasted_iota(jnp.int32, (tq, LANES), 1) < HEAD_DIM
    if online:
        (s_ref,) = s_ref

    for kvh in range(ATT_KV_HEADS):
        hs = slice(kvh * LANES, (kvh + 1) * LANES)
        parts = []
        for s in range(group // 2):
            slab = kvh * (group // 2) + s
            q = q_ref[:, slab * LANES:(slab + 1) * LANES]
            zero = jnp.zeros_like(q)
            parts += [jnp.where(low, q, zero), jnp.where(low, zero, q)]
        qs = jnp.concatenate(parts, axis=0)

        def scores(start, size, qs=qs, hs=hs):
            return _dot_nt(k_ref[pl.ds(start, size), hs], qs)

        def absorb(s, start, size, m, acc, hs=hs):
            if not online:
                return m, acc + _dot(vt_ref[hs, pl.ds(start, size)], jnp.exp2(s).astype(BF16))
            m_new = jnp.maximum(m, jnp.max(s, axis=0, keepdims=True))
            p = jnp.exp2(s - m_new).astype(BF16)
            pv = _dot(vt_ref[hs, pl.ds(start, size)], p)
            return m_new, acc * jnp.exp2(m - m_new) + pv

        def main_start(j):
            return pl.multiple_of(LANES + j * TK_ATT, LANES)

        def masked(s):
            return jnp.where(lax.broadcasted_iota(jnp.int32, s.shape, 0) >= padf, s, -1e30)

        acc = jnp.zeros((LANES, cols), F32)
        if online:
            m, acc = absorb(masked(scores(0, LANES)), 0, LANES,
                            jnp.full((1, cols), -jnp.inf, F32), acc)
            assert n_main % 2 == 0
            s_ref[0] = scores(main_start(0), TK_ATT)

            def pair(i, carry, last):
                m, acc = carry
                s_ref[1] = scores(main_start(2 * i + 1), TK_ATT)
                m, acc = absorb(s_ref[0], main_start(2 * i), TK_ATT, m, acc)
                if not last:
                    s_ref[0] = scores(main_start(2 * i + 2), TK_ATT)
                return absorb(s_ref[1], main_start(2 * i + 1), TK_ATT, m, acc)

            m, acc = lax.fori_loop(0, n_main // 2 - 1, lambda i, c: pair(i, c, False), (m, acc))
            _, acc = pair(n_main // 2 - 1, (m, acc), True)
        else:
            first = LANES + TK_ATT
            _, acc = absorb(masked(scores(0, first)), 0, first, None, acc)

            def direct(j, acc):
                _, acc = absorb(scores(main_start(j), TK_ATT), main_start(j), TK_ATT, None, acc)
                return acc

            acc = lax.fori_loop(1, n_main, direct, acc, unroll=n_main - 1)
        out_t = acc[:HEAD_DIM] / acc[HEAD_DIM:]
        for s in range(group // 2):
            slab = kvh * (group // 2) + s
            pair_t = jnp.concatenate([out_t[:, (2 * s) * tq:(2 * s + 1) * tq],
                                      out_t[:, (2 * s + 1) * tq:(2 * s + 2) * tq]], axis=0)
            o_ref[:, slab * LANES:(slab + 1) * LANES] = pair_t.T.astype(BF16)


def _attn(qb, kb, vt, lp, padf, online):
    n = qb.shape[0]
    nb = n // lp
    assert lp % TQ_ATT == 0 and (lp - LANES) % TK_ATT == 0
    nq = lp // TQ_ATT
    q_spec = pl.BlockSpec((TQ_ATT, ATT_Q), lambda b, i: (b * nq + i, 0))
    return pl.pallas_call(
        functools.partial(_attn_body, padf=padf, online=online),
        out_shape=jax.ShapeDtypeStruct((n, ATT_Q), BF16),
        grid=(nb, nq),
        in_specs=[q_spec, pl.BlockSpec((lp, kb.shape[1]), lambda b, i: (b, 0)),
                  pl.BlockSpec((vt.shape[0], lp), lambda b, i: (0, b))],
        out_specs=q_spec,
        scratch_shapes=([pltpu.VMEM((2, TK_ATT, ATT_HEADS // ATT_KV_HEADS * TQ_ATT), F32)]
                        if online else []),
        compiler_params=pltpu.CompilerParams(
            dimension_semantics=("parallel", "arbitrary"), vmem_limit_bytes=VMEM_LIMIT),
        name="attn",
    )(qb, kb, vt)


def _merge_body(h_ref, g_ref, a_ref, b_ref, wg_ref, bm_ref, wpa_ref, wpb_ref, wo_ref, o_ref,
                *, lp, padf):
    tm, d = h_ref.shape
    block_pos = (pl.program_id(0) % (lp // tm)) * tm
    for r in range(tm // TS_MERGE):
        rows = slice(r * TS_MERGE, (r + 1) * TS_MERGE)
        x = h_ref[rows, :]
        pa = _dot(a_ref[rows, :], wpa_ref[...])
        pb = _dot(b_ref[rows, :], wpb_ref[...])
        zb = _rmsnorm_rows(x, g_ref[...]).astype(BF16)
        gate_a = jax.nn.sigmoid(_dot(zb, wg_ref[:, :d]) + bm_ref[:, :d])
        gate_b = jax.nn.sigmoid(_dot(zb, wg_ref[:, d:]) + bm_ref[:, d:])
        out = _dot((gate_a * pa + gate_b * pb).astype(BF16), wo_ref[...])
        pos = block_pos + r * TS_MERGE + lax.broadcasted_iota(jnp.int32, (TS_MERGE, 1), 0)
        o_ref[rows, :] = x + jnp.where(pos >= padf, out, 0.0)


def _merge(hp, gain, a, b, wg, bm, wpa, wpb, wo, lp, padf, layer):
    n, d = hp.shape
    lead = (layer,)
    assert lp % TM_TOK == 0 and TM_TOK % TS_MERGE == 0

    def row(c):
        return pl.BlockSpec((TM_TOK, c), lambda i: (i, 0))

    return pl.pallas_call(
        functools.partial(_merge_body, lp=lp, padf=padf),
        out_shape=jax.ShapeDtypeStruct((n, d), F32),
        grid=(n // TM_TOK,),
        in_specs=[row(d), _resident((1, d)), row(GLA_VAL), row(ATT_Q), _resident(wg.shape, lead),
                  _resident(bm.shape, lead), _resident(wpa.shape, lead),
                  _resident(wpb.shape, lead), _resident(wo.shape, lead)],
        out_specs=row(d),
        compiler_params=pltpu.CompilerParams(
            dimension_semantics=("parallel",), vmem_limit_bytes=VMEM_LIMIT),
        name="merge",
    )(hp, gain, a, b, wg, bm, wpa, wpb, wo)


def _final_body(h_ref, g_ref, o_ref):
    o_ref[...] = _rmsnorm_rows(h_ref[...], g_ref[...])


def _final(hp, gain, nb, n_tok, lp, skip):
    d = hp.shape[1]
    assert n_tok % TM_FIN == 0
    per_out = n_tok // TM_FIN
    return pl.pallas_call(
        _final_body,
        out_shape=jax.ShapeDtypeStruct((nb * n_tok, d), F32),
        grid=(nb, per_out),
        in_specs=[pl.BlockSpec((pl.Element(TM_FIN), pl.Element(d)),
                               lambda b, i: (pl.multiple_of(b * lp + skip + i * TM_FIN, LANES), 0)),
                  _resident((1, d))],
        out_specs=pl.BlockSpec((TM_FIN, d), lambda b, i: (b * per_out + i, 0)),
        compiler_params=pltpu.CompilerParams(
            dimension_semantics=("parallel", "parallel"), vmem_limit_bytes=VMEM_LIMIT),
        name="final_norm",
    )(hp, gain)


def _rope_tables(n_tok, lp, padf):
    quarter = HEAD_DIM // 4
    pos = jnp.arange(n_tok)
    inv = ROPE_THETA ** (-jnp.arange(0, HEAD_DIM // 2, 2, dtype=F32) / (HEAD_DIM // 2))
    ang_r = (pos // GRID_W).astype(F32)[:, None] * inv
    ang_c = (pos % GRID_W).astype(F32)[:, None] * inv
    cos = jnp.concatenate([jnp.cos(ang_r)] * 2 + [jnp.cos(ang_c)] * 2, axis=-1)
    sin = jnp.concatenate([-jnp.sin(ang_r), jnp.sin(ang_r), -jnp.sin(ang_c), jnp.sin(ang_c)],
                          axis=-1)
    lead = padf + N_META
    cos = jnp.concatenate([jnp.ones((lead, HEAD_DIM), F32), cos], axis=0)
    sin = jnp.concatenate([jnp.zeros((lead, HEAD_DIM), F32), sin], axis=0)
    assert cos.shape == (lp, 4 * quarter)
    return jnp.tile(cos, (1, 2)), jnp.tile(sin, (1, 2))


def _prep_w_in(w_in):
    sizes = (GLA_KEY, GLA_KEY, GLA_VAL, GLA_VAL, GLA_RANK, GLA_RANK, ATT_Q, ATT_KV, ATT_KV)
    offs = np.concatenate([[0], np.cumsum(sizes)])
    qa, ka, va, ra, lrf, lrb, qb, kb, vb = (w_in[..., offs[i]:offs[i + 1]] for i in range(9))

    def spread(w, twice):
        heads = [w[..., h * HEAD_DIM:(h + 1) * HEAD_DIM] for h in range(ATT_KV_HEADS)]
        return jnp.concatenate([x for h in heads for x in (h, h if twice else jnp.zeros_like(h))],
                               axis=-1)

    lr = jnp.concatenate([lrf, lrb, jnp.zeros(lrf.shape[:-1] + (LANES - 2 * GLA_RANK,), F32)], -1)
    w_all = jnp.concatenate([qa, ka, va, ra, qb, spread(kb, True), lr], axis=-1).astype(BF16)
    return w_all, jnp.swapaxes(spread(vb, False), -1, -2).astype(BF16)


def _prep_gate(w2, b2):
    depth = w2.shape[0]
    w = jnp.zeros((depth, LANES, 2 * GLA_KEY), F32)
    w = w.at[:, :GLA_RANK, :GLA_KEY].set(w2[:, 0]).at[:, GLA_RANK:2 * GLA_RANK, GLA_KEY:].set(w2[:, 1])
    return w.astype(BF16), b2.reshape(depth, 1, 2 * GLA_KEY)


def kernel(x, meta_tokens, norm_gains, ffn_w_gate, ffn_w_up, ffn_w_down, w_in, gla_w2, gla_b2,
           gla_gn, q_norm, k_norm, w_pa, w_pb, b_merge, w_out, final_norm):
    nb, n_tok, d = x.shape
    depth = w_in.shape[0]
    lp = -(-(n_tok + N_META) // LANES) * LANES
    padf = lp - n_tok - N_META
    assert (padf + N_META) % CHUNK == 0 and n_tok % GRID_W == 0

    meta = jnp.broadcast_to(meta_tokens[None].astype(x.dtype), (nb, N_META, d))
    hp = jnp.concatenate([jnp.zeros((nb, padf, d), x.dtype), meta, x], axis=1).reshape(nb * lp, d)
    cos_t, sin_t = _rope_tables(n_tok, lp, padf)

    g_split = w_in.shape[2] - 2 * d
    ffn_w = tuple(w.astype(BF16) for w in (ffn_w_gate, ffn_w_up, ffn_w_down))
    w_all, wvt = _prep_w_in(w_in[:, :, :g_split])
    w2, b2 = _prep_gate(gla_w2, gla_b2)
    merge_w = (w_in[:, :, g_split:].astype(BF16), b_merge.reshape(depth, 1, 2 * d),
               w_pa.astype(BF16), w_pb.astype(BF16), w_out.astype(BF16))
    for l in range(depth):
        gains = norm_gains[l].reshape(3, 1, d)
        hp = _ffn(hp, gains[0], *ffn_w, (l, 0))
        qa, ka, va, sr, gf, gb, qb, kb, vt = _proj(
            hp, gains[1], w_all, wvt, w2, b2,
            jnp.tile(q_norm[l], 2).reshape(1, LANES), jnp.tile(k_norm[l], 2).reshape(1, LANES),
            cos_t, sin_t, lp, l)
        a = _gla(qa, ka, va, gf, gb, sr, gla_gn[l].reshape(1, GLA_VAL), lp)
        small = (jnp.max(jnp.abs(q_norm[l])) * jnp.max(jnp.abs(k_norm[l])) * ATT_SCORE_BOUND
                 <= ATT_SAFE_SCORE)
        b = lax.cond(small, functools.partial(_attn, lp=lp, padf=padf, online=False),
                     functools.partial(_attn, lp=lp, padf=padf, online=True), qb, kb, vt)
        hp = _merge(hp, gains[1], a, b, *merge_w, lp, padf, l)
        hp = _ffn(hp, gains[2], *ffn_w, (l, 1))
    out = _final(hp, final_norm.reshape(1, d), nb, n_tok, lp, padf + N_META)
    return out.reshape(nb, n_tok, d)
```

```python
import functools

import jax
import jax.numpy as jnp
import numpy as np
from jax import lax
from jax.experimental import pallas as pl
from jax.experimental.pallas import tpu as pltpu

F32 = jnp.float32
BF16 = jnp.bfloat16

N_META = 16
GRID_W = 64
GLA_HEADS = 4
GLA_DK = 64
GLA_DV = 128
GLA_KEY = GLA_HEADS * GLA_DK
GLA_VAL = GLA_HEADS * GLA_DV
GLA_RANK = 16
GLA_TAU = 16.0
CHUNK = 64
ATT_HEADS = 8
ATT_KV_HEADS = 2
HEAD_DIM = 64
ATT_Q = ATT_HEADS * HEAD_DIM
ATT_KV = ATT_KV_HEADS * HEAD_DIM
ROPE_THETA = 10000.0
EPS = 1e-6

LANES = 128
MXU_COLS = 256
VMEM_LIMIT = 56 * 1024 * 1024

TM_FFN = 768
TS_FFN = 256
TM_TOK = 1056
TS_MERGE = 352
TM_PROJ = 1408
TS_PROJ = 512
TQ_ATT = 384
TK_ATT = 512
TM_FIN = 1024
GLA_ROWS = 256
GLA_LEAD = 128
GLA_UNROLL = 16


def _resident(shape, lead=()):
    tail = tuple(shape[len(lead):])
    return pl.BlockSpec((None,) * len(lead) + tail, lambda *_: tuple(lead) + (0,) * len(tail),
                        pipeline_mode=pl.Buffered(1))


def _rmsnorm_rows(x, gain):
    return x * lax.rsqrt(jnp.mean(x * x, axis=-1, keepdims=True) + EPS) * gain


def _dot(a, b):
    return jnp.dot(a, b, preferred_element_type=F32)


def _dot_nt(a, b):
    return lax.dot_general(a, b, (((1,), (1,)), ((), ())), preferred_element_type=F32)


def _dot_tn(a, b):
    return lax.dot_general(a, b, (((0,), (0,)), ((), ())), preferred_element_type=F32)


def _silu(x):
    return x * jax.nn.sigmoid(x)


def _ffn_body(x_ref, g_ref, wg_ref, wu_ref, wd_ref, o_ref):
    for r in range(x_ref.shape[0] // TS_FFN):
        rows = slice(r * TS_FFN, (r + 1) * TS_FFN)
        x = x_ref[rows, :]
        xb = _rmsnorm_rows(x, g_ref[...]).astype(BF16)
        gate = _dot(xb, wg_ref[...])
        up = _dot(xb, wu_ref[...])
        o_ref[rows, :] = x + 0.5 * _dot((_silu(gate) * up).astype(BF16), wd_ref[...])


def _ffn(hp, gain, wg, wu, wd, which):
    n, d = hp.shape
    d_ff = wg.shape[-1]
    assert n % TM_FFN == 0 and TM_FFN % TS_FFN == 0 and d_ff % MXU_COLS == 0
    row = pl.BlockSpec((TM_FFN, d), lambda i: (i, 0))
    return pl.pallas_call(
        _ffn_body,
        out_shape=jax.ShapeDtypeStruct((n, d), F32),
        grid=(n // TM_FFN,),
        in_specs=[row, _resident((1, d)), _resident(wg.shape, which), _resident(wu.shape, which),
                  _resident(wd.shape, which)],
        out_specs=row,
        compiler_params=pltpu.CompilerParams(
            dimension_semantics=("parallel",), vmem_limit_bytes=VMEM_LIMIT),
        name="ffn",
    )(hp, gain, wg, wu, wd)


_C_QA = 0
_C_KA = _C_QA + GLA_KEY
_C_VA = _C_KA + GLA_KEY
_C_RA = _C_VA + GLA_VAL
_C_QB = _C_RA + GLA_VAL
_C_KB = _C_QB + ATT_Q
_C_LR = _C_KB + 2 * ATT_KV
_C_END = _C_LR + LANES
_Q_SCALE = HEAD_DIM ** -0.5 * float(np.log2(np.e))
ATT_SCORE_BOUND = 1.05 * HEAD_DIM * _Q_SCALE
ATT_SAFE_SCORE = 60.0


def _proj_body(h_ref, g_ref, w_ref, wvt_ref, w2_ref, b2_ref, gq_ref, gk_ref, cos_ref, sin_ref,
               qa_ref, ka_ref, va_ref, sr_ref, gf_ref, gb_ref, qb_ref, kb_ref, vt_ref):
    tm = h_ref.shape[0]
    for lo_row in range(0, tm, TS_PROJ):
        _proj_rows(slice(lo_row, min(lo_row + TS_PROJ, tm)), h_ref, g_ref, w_ref, wvt_ref, w2_ref,
                   b2_ref, gq_ref, gk_ref, cos_ref, sin_ref, qa_ref, ka_ref, va_ref, sr_ref,
                   gf_ref, gb_ref, qb_ref, kb_ref, vt_ref)


def _proj_rows(rows, h_ref, g_ref, w_ref, wvt_ref, w2_ref, b2_ref, gq_ref, gk_ref, cos_ref,
               sin_ref, qa_ref, ka_ref, va_ref, sr_ref, gf_ref, gb_ref, qb_ref, kb_ref, vt_ref):
    zb = _rmsnorm_rows(h_ref[rows, :], g_ref[...]).astype(BF16)

    def proj(lo, hi):
        return _dot(zb, w_ref[:, lo:hi])

    cos = cos_ref[rows, :]
    sin = sin_ref[rows, :]
    lane = lax.broadcasted_iota(jnp.int32, cos.shape, 1)
    low_head = lane < HEAD_DIM
    first_half = (lane % (HEAD_DIM // 2)) < (HEAD_DIM // 4)

    def norm_rope(x, gain, scale):
        x2 = x * x
        s_lo = jnp.sum(jnp.where(low_head, x2, 0.0), axis=-1, keepdims=True)
        s_hi = jnp.sum(jnp.where(low_head, 0.0, x2), axis=-1, keepdims=True)
        inv = jnp.where(low_head, lax.rsqrt(s_lo * (1.0 / HEAD_DIM) + EPS),
                        lax.rsqrt(s_hi * (1.0 / HEAD_DIM) + EPS))
        y = x * inv * gain
        partner = jnp.where(first_half, pltpu.roll(y, LANES - HEAD_DIM // 4, 1),
                            pltpu.roll(y, HEAD_DIM // 4, 1))
        return (y * cos + partner * sin) * scale

    for base, width, out_ref, gain_ref, scale in ((_C_QB, ATT_Q, qb_ref, gq_ref, _Q_SCALE),
                                                  (_C_KB, 2 * ATT_KV, kb_ref, gk_ref, 1.0)):
        for c in range(width // MXU_COLS):
            x = proj(base + c * MXU_COLS, base + (c + 1) * MXU_COLS)
            for s in range(MXU_COLS // LANES):
                sl = slice(c * MXU_COLS + s * LANES, c * MXU_COLS + (s + 1) * LANES)
                out_ref[rows, sl] = norm_rope(x[:, s * LANES:(s + 1) * LANES], gain_ref[...],
                                              scale).astype(BF16)
    lr = proj(_C_LR, _C_END).astype(BF16)
    pre = _dot(lr, w2_ref[...]) + b2_ref[...]
    lg = (jnp.minimum(pre, 0.0) - jnp.log(1.0 + jnp.exp(-jnp.abs(pre)))) * (1.0 / GLA_TAU)
    gf_ref[rows, :] = lg[:, :GLA_KEY]
    gb_ref[rows, :] = lg[:, GLA_KEY:]
    qa_ref[rows, :] = (proj(_C_QA, _C_KA) * GLA_DK ** -0.5).astype(BF16)
    ka_ref[rows, :] = proj(_C_KA, _C_VA).astype(BF16)
    va_ref[rows, :] = proj(_C_VA, _C_RA).astype(BF16)
    sr_ref[rows, :] = _silu(proj(_C_RA, _C_QB)).astype(BF16)
    vt = _dot_nt(wvt_ref[...], zb)
    row_i = lax.broadcasted_iota(jnp.int32, vt.shape, 0)
    vt_ref[:, rows] = jnp.where(row_i % LANES < HEAD_DIM, vt, 1.0).astype(BF16)


def _proj(hp, gain, w_all, wvt, w2, b2, gq, gk, cos_t, sin_t, lp, layer):
    n, d = hp.shape
    assert lp % TM_PROJ == 0
    tiles_per_seq = lp // TM_PROJ

    def row(c):
        return pl.BlockSpec((TM_PROJ, c), lambda i: (i, 0))

    table = pl.BlockSpec((TM_PROJ, LANES), lambda i: (i % tiles_per_seq, 0))
    outs = [(GLA_KEY, BF16), (GLA_KEY, BF16), (GLA_VAL, BF16), (GLA_VAL, BF16),
            (GLA_KEY, F32), (GLA_KEY, F32), (ATT_Q, BF16), (2 * ATT_KV, BF16)]
    vt_rows = wvt.shape[1]
    return pl.pallas_call(
        _proj_body,
        out_shape=[jax.ShapeDtypeStruct((n, c), t) for c, t in outs]
        + [jax.ShapeDtypeStruct((vt_rows, n), BF16)],
        grid=(n // TM_PROJ,),
        in_specs=[row(d), _resident((1, d)), _resident(w_all.shape, (layer,)),
                  _resident(wvt.shape, (layer,)), _resident(w2.shape, (layer,)),
                  _resident(b2.shape, (layer,)), _resident((1, LANES)),
                  _resident((1, LANES)), table, table],
        out_specs=[row(c) for c, _ in outs]
        + [pl.BlockSpec((vt_rows, TM_PROJ), lambda i: (0, i))],
        compiler_params=pltpu.CompilerParams(
            dimension_semantics=("parallel",), vmem_limit_bytes=VMEM_LIMIT),
        name="proj",
    )(hp, gain, w_all, wvt, w2, b2, gq, gk, cos_t, sin_t)


def _gla_body(q_ref, k_ref, v_ref, gf_ref, gb_ref, sr_ref, gn_ref, o_ref, ob_ref, st_ref):
    lp = q_ref.shape[0]
    assert GLA_LEAD % CHUNK == 0 and (lp - GLA_LEAD) % GLA_ROWS == 0
    n_main = (lp - GLA_LEAD) // GLA_ROWS

    def block(start, rows_n, forward, final):
        nch = rows_n // CHUNK
        rows = pl.ds(start, rows_n)
        pos = lax.broadcasted_iota(jnp.int32, (rows_n, LANES), 0) % CHUNK
        lane = lax.broadcasted_iota(jnp.int32, (rows_n, LANES), 1)

        bc = (gf_ref if forward else gb_ref)[rows, :]
        shift = 1
        while shift < CHUNK:
            if forward:
                moved, ok = pltpu.roll(bc, shift, 0), pos >= shift
            else:
                moved, ok = pltpu.roll(bc, rows_n - shift, 0), pos < CHUNK - shift
            bc = bc + jnp.where(ok, moved, 0.0)
            shift *= 2
        edge = CHUNK - 1 if forward else 0
        bt3 = bc.reshape(nch, CHUNK, LANES)[:, edge:edge + 1, :]
        bt = jnp.broadcast_to(bt3, (nch, CHUNK, LANES)).reshape(rows_n, LANES)
        dec = jnp.exp(bt3)

        q = q_ref[rows, :].astype(F32)
        k = k_ref[rows, :].astype(F32)
        qd = q * jnp.exp(bc)
        ki = (k * jnp.exp(-bc)).astype(BF16)
        ke = (k * jnp.exp(bt - bc)).astype(BF16)

        r_i = lax.broadcasted_iota(jnp.int32, (rows_n, rows_n), 0)
        c_i = lax.broadcasted_iota(jnp.int32, (rows_n, rows_n), 1)
        same_chunk = (r_i // CHUNK) == (c_i // CHUNK)
        att_ok = same_chunk & ((r_i >= c_i) if forward else (r_i <= c_i))
        wide = (rows_n, nch * LANES)
        own_block = (lax.broadcasted_iota(jnp.int32, wide, 0) // CHUNK
                     == lax.broadcasted_iota(jnp.int32, wide, 1) // LANES)
        ke_blk = jnp.where(own_block, jnp.tile(ke, (1, nch)), jnp.zeros(wide, BF16))

        entering = []
        for h in range(2):
            hs = slice(h * GLA_DV, (h + 1) * GLA_DV)
            kv_t = _dot_tn(v_ref[rows, hs], ke_blk)
            st = st_ref[h]
            states = [None] * nch
            for c in (range(nch) if forward else range(nch - 1, -1, -1)):
                states[c] = st
                st = st * dec[c] + kv_t[:, c * LANES:(c + 1) * LANES]
            st_ref[h] = st
            entering.append(jnp.concatenate(states, axis=1).astype(BF16))

        for h in range(2):
            hs = slice(h * GLA_DV, (h + 1) * GLA_DV)
            qh = jnp.where((lane < GLA_DK) if h == 0 else (lane >= GLA_DK), qd, 0.0).astype(BF16)
            att = jnp.where(att_ok, _dot_nt(qh, ki), 0.0).astype(BF16)
            q_blk = jnp.where(own_block, jnp.tile(qh, (1, nch)), jnp.zeros(wide, BF16))
            o = _dot(att, v_ref[rows, hs]) + _dot_nt(q_blk, entering[h])
            if not final:
                ob_ref[rows, hs] = o
            else:
                o = o + ob_ref[rows, hs]
                o = o * lax.rsqrt(jnp.mean(o * o, axis=-1, keepdims=True) + EPS) * gn_ref[:, hs]
                o_ref[rows, hs] = (o * sr_ref[rows, hs].astype(F32)).astype(BF16)

    def main_start(j):
        return pl.multiple_of(GLA_LEAD + j * GLA_ROWS, CHUNK)

    st_ref[...] = jnp.zeros(st_ref.shape, F32)

    def bwd(i, carry):
        block(main_start(n_main - 1 - i), GLA_ROWS, False, False)
        return carry

    lax.fori_loop(0, n_main, bwd, 0, unroll=GLA_UNROLL)
    block(0, GLA_LEAD, False, False)

    st_ref[...] = jnp.zeros(st_ref.shape, F32)
    block(0, GLA_LEAD, True, True)

    def fwd(i, carry):
        block(main_start(i), GLA_ROWS, True, True)
        return carry

    lax.fori_loop(0, n_main, fwd, 0, unroll=GLA_UNROLL)


def _gla(qa, ka, va, gf, gb, sr, gn, lp):
    n = qa.shape[0]
    nb = n // lp
    pairs = GLA_HEADS // 2

    def blk(c):
        return pl.BlockSpec((lp, c), lambda b, p: (b, p))

    return pl.pallas_call(
        _gla_body,
        out_shape=jax.ShapeDtypeStruct((n, GLA_VAL), BF16),
        grid=(nb, pairs),
        in_specs=[blk(LANES), blk(LANES), blk(2 * GLA_DV), blk(LANES), blk(LANES),
                  blk(2 * GLA_DV), pl.BlockSpec((1, 2 * GLA_DV), lambda b, p: (0, p))],
        out_specs=blk(2 * GLA_DV),
        scratch_shapes=[pltpu.VMEM((lp, 2 * GLA_DV), F32),
                        pltpu.VMEM((2, GLA_DV, LANES), F32)],
        compiler_params=pltpu.CompilerParams(
            dimension_semantics=("parallel", "parallel"), vmem_limit_bytes=VMEM_LIMIT),
        name="gla",
    )(qa, ka, va, gf, gb, sr, gn)


def _attn_body(q_ref, k_ref, vt_ref, o_ref, *s_ref, padf, online):
    tq = q_ref.shape[0]
    lp = k_ref.shape[0]
    group = ATT_HEADS // ATT_KV_HEADS
    cols = group * tq
    n_main = (lp - LANES) // TK_ATT
    low = lax.broadcasted_iota(jnp.int32, (tq, LANES), 1) < HEAD_DIM
    if online:
        (s_ref,) = s_ref

    for kvh in range(ATT_KV_HEADS):
        hs = slice(kvh * LANES, (kvh + 1) * LANES)
        parts = []
        for s in range(group // 2):
            slab = kvh * (group // 2) + s
            q = q_ref[:, slab * LANES:(slab + 1) * LANES]
            zero = jnp.zeros_like(q)
            parts += [jnp.where(low, q, zero), jnp.where(low, zero, q)]
        qs = jnp.concatenate(parts, axis=0)

        def scores(start, size, qs=qs, hs=hs):
            return _dot_nt(k_ref[pl.ds(start, size), hs], qs)

        def absorb(s, start, size, m, acc, hs=hs):
            if not online:
                return m, acc + _dot(vt_ref[hs, pl.ds(start, size)], jnp.exp2(s).astype(BF16))
            m_new = jnp.maximum(m, jnp.max(s, axis=0, keepdims=True))
            p = jnp.exp2(s - m_new).astype(BF16)
            pv = _dot(vt_ref[hs, pl.ds(start, size)], p)
            return m_new, acc * jnp.exp2(m - m_new) + pv

        def main_start(j):
            return pl.multiple_of(LANES + j * TK_ATT, LANES)

        def masked(s):
            return jnp.where(lax.broadcasted_iota(jnp.int32, s.shape, 0) >= padf, s, -1e30)

        acc = jnp.zeros((LANES, cols), F32)
        if online:
            m, acc = absorb(masked(scores(0, LANES)), 0, LANES,
                            jnp.full((1, cols), -jnp.inf, F32), acc)
            assert n_main % 2 == 0
            s_ref[0] = scores(main_start(0), TK_ATT)

            def pair(i, carry, last):
                m, acc = carry
                s_ref[1] = scores(main_start(2 * i + 1), TK_ATT)
                m, acc = absorb(s_ref[0], main_start(2 * i), TK_ATT, m, acc)
                if not last:
                    s_ref[0] = scores(main_start(2 * i + 2), TK_ATT)
                return absorb(s_ref[1], main_start(2 * i + 1), TK_ATT, m, acc)

            m, acc = lax.fori_loop(0, n_main // 2 - 1, lambda i, c: pair(i, c, False), (m, acc))
            _, acc = pair(n_main // 2 - 1, (m, acc), True)
        else:
            first = LANES + TK_ATT
            _, acc = absorb(masked(scores(0, first)), 0, first, None, acc)

            def direct(j, acc):
                _, acc = absorb(scores(main_start(j), TK_ATT), main_start(j), TK_ATT, None, acc)
                return acc

            acc = lax.fori_loop(1, n_main, direct, acc, unroll=n_main - 1)
        out_t = acc[:HEAD_DIM] / acc[HEAD_DIM:]
        for s in range(group // 2):
            slab = kvh * (group // 2) + s
            pair_t = jnp.concatenate([out_t[:, (2 * s) * tq:(2 * s + 1) * tq],
                                      out_t[:, (2 * s + 1) * tq:(2 * s + 2) * tq]], axis=0)
            o_ref[:, slab * LANES:(slab + 1) * LANES] = pair_t.T.astype(BF16)


def _attn(qb, kb, vt, lp, padf, online):
    n = qb.shape[0]
    nb = n // lp
    assert lp % TQ_ATT == 0 and (lp - LANES) % TK_ATT == 0
    nq = lp // TQ_ATT
    q_spec = pl.BlockSpec((TQ_ATT, ATT_Q), lambda b, i: (b * nq + i, 0))
    return pl.pallas_call(
        functools.partial(_attn_body, padf=padf, online=online),
        out_shape=jax.ShapeDtypeStruct((n, ATT_Q), BF16),
        grid=(nb, nq),
        in_specs=[q_spec, pl.BlockSpec((lp, kb.shape[1]), lambda b, i: (b, 0)),
                  pl.BlockSpec((vt.shape[0], lp), lambda b, i: (0, b))],
        out_specs=q_spec,
        scratch_shapes=([pltpu.VMEM((2, TK_ATT, ATT_HEADS // ATT_KV_HEADS * TQ_ATT), F32)]
                        if online else []),
        compiler_params=pltpu.CompilerParams(
            dimension_semantics=("parallel", "arbitrary"), vmem_limit_bytes=VMEM_LIMIT),
        name="attn",
    )(qb, kb, vt)


def _merge_body(h_ref, g_ref, a_ref, b_ref, wg_ref, bm_ref, wpa_ref, wpb_ref, wo_ref, o_ref,
                *, lp, padf):
    tm, d = h_ref.shape
    block_pos = (pl.program_id(0) % (lp // tm)) * tm
    for r in range(tm // TS_MERGE):
        rows = slice(r * TS_MERGE, (r + 1) * TS_MERGE)
        x = h_ref[rows, :]
        pa = _dot(a_ref[rows, :], wpa_ref[...])
        pb = _dot(b_ref[rows, :], wpb_ref[...])
        zb = _rmsnorm_rows(x, g_ref[...]).astype(BF16)
        gate_a = jax.nn.sigmoid(_dot(zb, wg_ref[:, :d]) + bm_ref[:, :d])
        gate_b = jax.nn.sigmoid(_dot(zb, wg_ref[:, d:]) + bm_ref[:, d:])
        out = _dot((gate_a * pa + gate_b * pb).astype(BF16), wo_ref[...])
        pos = block_pos + r * TS_MERGE + lax.broadcasted_iota(jnp.int32, (TS_MERGE, 1), 0)
        o_ref[rows, :] = x + jnp.where(pos >= padf, out, 0.0)


def _merge(hp, gain, a, b, wg, bm, wpa, wpb, wo, lp, padf, layer):
    n, d = hp.shape
    lead = (layer,)
    assert lp % TM_TOK == 0 and TM_TOK % TS_MERGE == 0

    def row(c):
        return pl.BlockSpec((TM_TOK, c), lambda i: (i, 0))

    return pl.pallas_call(
        functools.partial(_merge_body, lp=lp, padf=padf),
        out_shape=jax.ShapeDtypeStruct((n, d), F32),
        grid=(n // TM_TOK,),
        in_specs=[row(d), _resident((1, d)), row(GLA_VAL), row(ATT_Q), _resident(wg.shape, lead),
                  _resident(bm.shape, lead), _resident(wpa.shape, lead),
                  _resident(wpb.shape, lead), _resident(wo.shape, lead)],
        out_specs=row(d),
        compiler_params=pltpu.CompilerParams(
            dimension_semantics=("parallel",), vmem_limit_bytes=VMEM_LIMIT),
        name="merge",
    )(hp, gain, a, b, wg, bm, wpa, wpb, wo)


def _final_body(h_ref, g_ref, o_ref):
    o_ref[...] = _rmsnorm_rows(h_ref[...], g_ref[...])


def _final(hp, gain, nb, n_tok, lp, skip):
    d = hp.shape[1]
    assert n_tok % TM_FIN == 0
    per_out = n_tok // TM_FIN
    return pl.pallas_call(
        _final_body,
        out_shape=jax.ShapeDtypeStruct((nb * n_tok, d), F32),
        grid=(nb, per_out),
        in_specs=[pl.BlockSpec((pl.Element(TM_FIN), pl.Element(d)),
                               lambda b, i: (pl.multiple_of(b * lp + skip + i * TM_FIN, LANES), 0)),
                  _resident((1, d))],
        out_specs=pl.BlockSpec((TM_FIN, d), lambda b, i: (b * per_out + i, 0)),
        compiler_params=pltpu.CompilerParams(
            dimension_semantics=("parallel", "parallel"), vmem_limit_bytes=VMEM_LIMIT),
        name="final_norm",
    )(hp, gain)


def _rope_tables(n_tok, lp, padf):
    quarter = HEAD_DIM // 4
    pos = jnp.arange(n_tok)
    inv = ROPE_THETA ** (-jnp.arange(0, HEAD_DIM // 2, 2, dtype=F32) / (HEAD_DIM // 2))
    ang_r = (pos // GRID_W).astype(F32)[:, None] * inv
    ang_c = (pos % GRID_W).astype(F32)[:, None] * inv
    cos = jnp.concatenate([jnp.cos(ang_r)] * 2 + [jnp.cos(ang_c)] * 2, axis=-1)
    sin = jnp.concatenate([-jnp.sin(ang_r), jnp.sin(ang_r), -jnp.sin(ang_c), jnp.sin(ang_c)],
                          axis=-1)
    lead = padf + N_META
    cos = jnp.concatenate([jnp.ones((lead, HEAD_DIM), F32), cos], axis=0)
    sin = jnp.concatenate([jnp.zeros((lead, HEAD_DIM), F32), sin], axis=0)
    assert cos.shape == (lp, 4 * quarter)
    return jnp.tile(cos, (1, 2)), jnp.tile(sin, (1, 2))


def _prep_w_in(w_in):
    sizes = (GLA_KEY, GLA_KEY, GLA_VAL, GLA_VAL, GLA_RANK, GLA_RANK, ATT_Q, ATT_KV, ATT_KV)
    offs = np.concatenate([[0], np.cumsum(sizes)])
    qa, ka, va, ra, lrf, lrb, qb, kb, vb = (w_in[..., offs[i]:offs[i + 1]] for i in range(9))

    def spread(w, twice):
        heads = [w[..., h * HEAD_DIM:(h + 1) * HEAD_DIM] for h in range(ATT_KV_HEADS)]
        return jnp.concatenate([x for h in heads for x in (h, h if twice else jnp.zeros_like(h))],
                               axis=-1)

    lr = jnp.concatenate([lrf, lrb, jnp.zeros(lrf.shape[:-1] + (LANES - 2 * GLA_RANK,), F32)], -1)
    w_all = jnp.concatenate([qa, ka, va, ra, qb, spread(kb, True), lr], axis=-1).astype(BF16)
    return w_all, jnp.swapaxes(spread(vb, False), -1, -2).astype(BF16)


def _prep_gate(w2, b2):
    depth = w2.shape[0]
    w = jnp.zeros((depth, LANES, 2 * GLA_KEY), F32)
    w = w.at[:, :GLA_RANK, :GLA_KEY].set(w2[:, 0]).at[:, GLA_RANK:2 * GLA_RANK, GLA_KEY:].set(w2[:, 1])
    return w.astype(BF16), b2.reshape(depth, 1, 2 * GLA_KEY)


def kernel(x, meta_tokens, norm_gains, ffn_w_gate, ffn_w_up, ffn_w_down, w_in, gla_w2, gla_b2,
           gla_gn, q_norm, k_norm, w_pa, w_pb, b_merge, w_out, final_norm):
    nb, n_tok, d = x.shape
    depth = w_in.shape[0]
    lp = -(-(n_tok + N_META) // LANES) * LANES
    padf = lp - n_tok - N_META
    assert (padf + N_META) % CHUNK == 0 and n_tok % GRID_W == 0

    meta = jnp.broadcast_to(meta_tokens[None].astype(x.dtype), (nb, N_META, d))
    hp = jnp.concatenate([jnp.zeros((nb, padf, d), x.dtype), meta, x], axis=1).reshape(nb * lp, d)
    cos_t, sin_t = _rope_tables(n_tok, lp, padf)

    g_split = w_in.shape[2] - 2 * d
    ffn_w = tuple(w.astype(BF16) for w in (ffn_w_gate, ffn_w_up, ffn_w_down))
    w_all, wvt = _prep_w_in(w_in[:, :, :g_split])
    w2, b2 = _prep_gate(gla_w2, gla_b2)
    merge_w = (w_in[:, :, g_split:].astype(BF16), b_merge.reshape(depth, 1, 2 * d),
               w_pa.astype(BF16), w_pb.astype(BF16), w_out.astype(BF16))
    for l in range(depth):
        gains = norm_gains[l].reshape(3, 1, d)
        hp = _ffn(hp, gains[0], *ffn_w, (l, 0))
        qa, ka, va, sr, gf, gb, qb, kb, vt = _proj(
            hp, gains[1], w_all, wvt, w2, b2,
            jnp.tile(q_norm[l], 2).reshape(1, LANES), jnp.tile(k_norm[l], 2).reshape(1, LANES),
            cos_t, sin_t, lp, l)
        a = _gla(qa, ka, va, gf, gb, sr, gla_gn[l].reshape(1, GLA_VAL), lp)
        small = (jnp.max(jnp.abs(q_norm[l])) * jnp.max(jnp.abs(k_norm[l])) * ATT_SCORE_BOUND
                 <= ATT_SAFE_SCORE)
        b = lax.cond(small, functools.partial(_attn, lp=lp, padf=padf, online=False),
                     functools.partial(_attn, lp=lp, padf=padf, online=True), qb, kb, vt)
        hp = _merge(hp, gains[1], a, b, *merge_w, lp, padf, l)
        hp = _ffn(hp, gains[2], *ffn_w, (l, 1))
    out = _final(hp, final_norm.reshape(1, d), nb, n_tok, lp, padf + N_META)
    return out.reshape(nb, n_tok, d)
```

```python
import functools

import jax
import jax.numpy as jnp
import numpy as np
from jax import lax
from jax.experimental import pallas as pl
from jax.experimental.pallas import tpu as pltpu

F32 = jnp.float32
BF16 = jnp.bfloat16

N_META = 16
GRID_W = 64
GLA_HEADS = 4
GLA_DK = 64
GLA_DV = 128
GLA_KEY = GLA_HEADS * GLA_DK
GLA_VAL = GLA_HEADS * GLA_DV
GLA_RANK = 16
GLA_TAU = 16.0
CHUNK = 64
ATT_HEADS = 8
ATT_KV_HEADS = 2
HEAD_DIM = 64
ATT_Q = ATT_HEADS * HEAD_DIM
ATT_KV = ATT_KV_HEADS * HEAD_DIM
ROPE_THETA = 10000.0
EPS = 1e-6

LANES = 128
MXU_COLS = 256
VMEM_LIMIT = 56 * 1024 * 1024

TM_FFN = 768
TS_FFN = 192
TM_TOK = 1056
TS_MERGE = 176
TM_PROJ = 1408
TS_PROJ = 512
TQ_ATT = 384
TK_ATT = 512
TM_FIN = 1024
GLA_ROWS = 256
GLA_LEAD = 128
GLA_UNROLL = 16


def _resident(shape, lead=()):
    tail = tuple(shape[len(lead):])
    return pl.BlockSpec((None,) * len(lead) + tail, lambda *_: tuple(lead) + (0,) * len(tail),
                        pipeline_mode=pl.Buffered(1))


def _rmsnorm_rows(x, gain):
    return x * lax.rsqrt(jnp.mean(x * x, axis=-1, keepdims=True) + EPS) * gain


def _dot(a, b):
    return jnp.dot(a, b, preferred_element_type=F32)


def _dot_nt(a, b):
    return lax.dot_general(a, b, (((1,), (1,)), ((), ())), preferred_element_type=F32)


def _dot_tn(a, b):
    return lax.dot_general(a, b, (((0,), (0,)), ((), ())), preferred_element_type=F32)


def _silu(x):
    return x * jax.nn.sigmoid(x)


def _ffn_body(x_ref, g_ref, wg_ref, wu_ref, wd_ref, o_ref):
    for r in range(x_ref.shape[0] // TS_FFN):
        rows = slice(r * TS_FFN, (r + 1) * TS_FFN)
        x = x_ref[rows, :]
        xb = _rmsnorm_rows(x, g_ref[...]).astype(BF16)
        gate = _dot(xb, wg_ref[...])
        up = _dot(xb, wu_ref[...])
        o_ref[rows, :] = x + 0.5 * _dot((_silu(gate) * up).astype(BF16), wd_ref[...])


def _ffn(hp, gain, wg, wu, wd, which):
    n, d = hp.shape
    d_ff = wg.shape[-1]
    assert n % TM_FFN == 0 and TM_FFN % TS_FFN == 0 and d_ff % MXU_COLS == 0
    row = pl.BlockSpec((TM_FFN, d), lambda i: (i, 0))
    return pl.pallas_call(
        _ffn_body,
        out_shape=jax.ShapeDtypeStruct((n, d), F32),
        grid=(n // TM_FFN,),
        in_specs=[row, _resident((1, d)), _resident(wg.shape, which), _resident(wu.shape, which),
                  _resident(wd.shape, which)],
        out_specs=row,
        compiler_params=pltpu.CompilerParams(
            dimension_semantics=("parallel",), vmem_limit_bytes=VMEM_LIMIT),
        name="ffn",
    )(hp, gain, wg, wu, wd)


_C_QA = 0
_C_KA = _C_QA + GLA_KEY
_C_VA = _C_KA + GLA_KEY
_C_RA = _C_VA + GLA_VAL
_C_QB = _C_RA + GLA_VAL
_C_KB = _C_QB + ATT_Q
_C_LR = _C_KB + 2 * ATT_KV
_C_END = _C_LR + LANES
_Q_SCALE = HEAD_DIM ** -0.5 * float(np.log2(np.e))
ATT_SCORE_BOUND = 1.05 * HEAD_DIM * _Q_SCALE
ATT_SAFE_SCORE = 60.0


def _proj_body(h_ref, g_ref, w_ref, wvt_ref, w2_ref, b2_ref, gq_ref, gk_ref, cos_ref, sin_ref,
               qa_ref, ka_ref, va_ref, sr_ref, gf_ref, gb_ref, qb_ref, kb_ref, vt_ref):
    tm = h_ref.shape[0]
    for lo_row in range(0, tm, TS_PROJ):
        _proj_rows(slice(lo_row, min(lo_row + TS_PROJ, tm)), h_ref, g_ref, w_ref, wvt_ref, w2_ref,
                   b2_ref, gq_ref, gk_ref, cos_ref, sin_ref, qa_ref, ka_ref, va_ref, sr_ref,
                   gf_ref, gb_ref, qb_ref, kb_ref, vt_ref)


def _proj_rows(rows, h_ref, g_ref, w_ref, wvt_ref, w2_ref, b2_ref, gq_ref, gk_ref, cos_ref,
               sin_ref, qa_ref, ka_ref, va_ref, sr_ref, gf_ref, gb_ref, qb_ref, kb_ref, vt_ref):
    zb = _rmsnorm_rows(h_ref[rows, :], g_ref[...]).astype(BF16)

    def proj(lo, hi):
        return _dot(zb, w_ref[:, lo:hi])

    cos = cos_ref[rows, :]
    sin = sin_ref[rows, :]
    lane = lax.broadcasted_iota(jnp.int32, cos.shape, 1)
    low_head = lane < HEAD_DIM
    first_half = (lane % (HEAD_DIM // 2)) < (HEAD_DIM // 4)

    def norm_rope(x, gain, scale):
        x2 = x * x
        s_lo = jnp.sum(jnp.where(low_head, x2, 0.0), axis=-1, keepdims=True)
        s_hi = jnp.sum(jnp.where(low_head, 0.0, x2), axis=-1, keepdims=True)
        inv = jnp.where(low_head, lax.rsqrt(s_lo * (1.0 / HEAD_DIM) + EPS),
                        lax.rsqrt(s_hi * (1.0 / HEAD_DIM) + EPS))
        y = x * inv * gain
        partner = jnp.where(first_half, pltpu.roll(y, LANES - HEAD_DIM // 4, 1),
                            pltpu.roll(y, HEAD_DIM // 4, 1))
        return (y * cos + partner * sin) * scale

    for base, width, out_ref, gain_ref, scale in ((_C_QB, ATT_Q, qb_ref, gq_ref, _Q_SCALE),
                                                  (_C_KB, 2 * ATT_KV, kb_ref, gk_ref, 1.0)):
        for c in range(width // MXU_COLS):
            x = proj(base + c * MXU_COLS, base + (c + 1) * MXU_COLS)
            for s in range(MXU_COLS // LANES):
                sl = slice(c * MXU_COLS + s * LANES, c * MXU_COLS + (s + 1) * LANES)
                out_ref[rows, sl] = norm_rope(x[:, s * LANES:(s + 1) * LANES], gain_ref[...],
                                              scale).astype(BF16)
    lr = proj(_C_LR, _C_END).astype(BF16)
    pre = _dot(lr, w2_ref[...]) + b2_ref[...]
    lg = (jnp.minimum(pre, 0.0) - jnp.log(1.0 + jnp.exp(-jnp.abs(pre)))) * (1.0 / GLA_TAU)
    gf_ref[rows, :] = lg[:, :GLA_KEY]
    gb_ref[rows, :] = lg[:, GLA_KEY:]
    qa_ref[rows, :] = (proj(_C_QA, _C_KA) * GLA_DK ** -0.5).astype(BF16)
    ka_ref[rows, :] = proj(_C_KA, _C_VA).astype(BF16)
    va_ref[rows, :] = proj(_C_VA, _C_RA).astype(BF16)
    sr_ref[rows, :] = _silu(proj(_C_RA, _C_QB)).astype(BF16)
    vt = _dot_nt(wvt_ref[...], zb)
    row_i = lax.broadcasted_iota(jnp.int32, vt.shape, 0)
    vt_ref[:, rows] = jnp.where(row_i % LANES < HEAD_DIM, vt, 1.0).astype(BF16)


def _proj(hp, gain, w_all, wvt, w2, b2, gq, gk, cos_t, sin_t, lp, layer):
    n, d = hp.shape
    assert lp % TM_PROJ == 0
    tiles_per_seq = lp // TM_PROJ

    def row(c):
        return pl.BlockSpec((TM_PROJ, c), lambda i: (i, 0))

    table = pl.BlockSpec((TM_PROJ, LANES), lambda i: (i % tiles_per_seq, 0))
    outs = [(GLA_KEY, BF16), (GLA_KEY, BF16), (GLA_VAL, BF16), (GLA_VAL, BF16),
            (GLA_KEY, F32), (GLA_KEY, F32), (ATT_Q, BF16), (2 * ATT_KV, BF16)]
    vt_rows = wvt.shape[1]
    return pl.pallas_call(
        _proj_body,
        out_shape=[jax.ShapeDtypeStruct((n, c), t) for c, t in outs]
        + [jax.ShapeDtypeStruct((vt_rows, n), BF16)],
        grid=(n // TM_PROJ,),
        in_specs=[row(d), _resident((1, d)), _resident(w_all.shape, (layer,)),
                  _resident(wvt.shape, (layer,)), _resident(w2.shape, (layer,)),
                  _resident(b2.shape, (layer,)), _resident((1, LANES)),
                  _resident((1, LANES)), table, table],
        out_specs=[row(c) for c, _ in outs]
        + [pl.BlockSpec((vt_rows, TM_PROJ), lambda i: (0, i))],
        compiler_params=pltpu.CompilerParams(
            dimension_semantics=("parallel",), vmem_limit_bytes=VMEM_LIMIT),
        name="proj",
    )(hp, gain, w_all, wvt, w2, b2, gq, gk, cos_t, sin_t)


def _gla_body(q_ref, k_ref, v_ref, gf_ref, gb_ref, sr_ref, gn_ref, o_ref, ob_ref, st_ref):
    lp = q_ref.shape[0]
    assert GLA_LEAD % CHUNK == 0 and (lp - GLA_LEAD) % GLA_ROWS == 0
    n_main = (lp - GLA_LEAD) // GLA_ROWS

    def block(start, rows_n, forward, final):
        nch = rows_n // CHUNK
        rows = pl.ds(start, rows_n)
        pos = lax.broadcasted_iota(jnp.int32, (rows_n, LANES), 0) % CHUNK
        lane = lax.broadcasted_iota(jnp.int32, (rows_n, LANES), 1)

        bc = (gf_ref if forward else gb_ref)[rows, :]
        shift = 1
        while shift < CHUNK:
            if forward:
                moved, ok = pltpu.roll(bc, shift, 0), pos >= shift
            else:
                moved, ok = pltpu.roll(bc, rows_n - shift, 0), pos < CHUNK - shift
            bc = bc + jnp.where(ok, moved, 0.0)
            shift *= 2
        edge = CHUNK - 1 if forward else 0
        bt3 = bc.reshape(nch, CHUNK, LANES)[:, edge:edge + 1, :]
        bt = jnp.broadcast_to(bt3, (nch, CHUNK, LANES)).reshape(rows_n, LANES)
        dec = jnp.exp(bt3)

        q = q_ref[rows, :].astype(F32)
        k = k_ref[rows, :].astype(F32)
        qd = q * jnp.exp(bc)
        ki = (k * jnp.exp(-bc)).astype(BF16)
        ke = (k * jnp.exp(bt - bc)).astype(BF16)

        r_i = lax.broadcasted_iota(jnp.int32, (rows_n, rows_n), 0)
        c_i = lax.broadcasted_iota(jnp.int32, (rows_n, rows_n), 1)
        same_chunk = (r_i // CHUNK) == (c_i // CHUNK)
        att_ok = same_chunk & ((r_i >= c_i) if forward else (r_i <= c_i))
        wide = (rows_n, nch * LANES)
        own_block = (lax.broadcasted_iota(jnp.int32, wide, 0) // CHUNK
                     == lax.broadcasted_iota(jnp.int32, wide, 1) // LANES)
        ke_blk = jnp.where(own_block, jnp.tile(ke, (1, nch)), jnp.zeros(wide, BF16))

        entering = []
        for h in range(2):
            hs = slice(h * GLA_DV, (h + 1) * GLA_DV)
            kv_t = _dot_tn(v_ref[rows, hs], ke_blk)
            st = st_ref[h]
            states = [None] * nch
            for c in (range(nch) if forward else range(nch - 1, -1, -1)):
                states[c] = st
                st = st * dec[c] + kv_t[:, c * LANES:(c + 1) * LANES]
            st_ref[h] = st
            entering.append(jnp.concatenate(states, axis=1).astype(BF16))

        for h in range(2):
            hs = slice(h * GLA_DV, (h + 1) * GLA_DV)
            qh = jnp.where((lane < GLA_DK) if h == 0 else (lane >= GLA_DK), qd, 0.0).astype(BF16)
            att = jnp.where(att_ok, _dot_nt(qh, ki), 0.0).astype(BF16)
            q_blk = jnp.where(own_block, jnp.tile(qh, (1, nch)), jnp.zeros(wide, BF16))
            o = _dot(att, v_ref[rows, hs]) + _dot_nt(q_blk, entering[h])
            if not final:
                ob_ref[rows, hs] = o
            else:
                o = o + ob_ref[rows, hs]
                o = o * lax.rsqrt(jnp.mean(o * o, axis=-1, keepdims=True) + EPS) * gn_ref[:, hs]
                o_ref[rows, hs] = (o * sr_ref[rows, hs].astype(F32)).astype(BF16)

    def main_start(j):
        return pl.multiple_of(GLA_LEAD + j * GLA_ROWS, CHUNK)

    st_ref[...] = jnp.zeros(st_ref.shape, F32)

    def bwd(i, carry):
        block(main_start(n_main - 1 - i), GLA_ROWS, False, False)
        return carry

    lax.fori_loop(0, n_main, bwd, 0, unroll=GLA_UNROLL)
    block(0, GLA_LEAD, False, False)

    st_ref[...] = jnp.zeros(st_ref.shape, F32)
    block(0, GLA_LEAD, True, True)

    def fwd(i, carry):
        block(main_start(i), GLA_ROWS, True, True)
        return carry

    lax.fori_loop(0, n_main, fwd, 0, unroll=GLA_UNROLL)


def _gla(qa, ka, va, gf, gb, sr, gn, lp):
    n = qa.shape[0]
    nb = n // lp
    pairs = GLA_HEADS // 2

    def blk(c):
        return pl.BlockSpec((lp, c), lambda b, p: (b, p))

    return pl.pallas_call(
        _gla_body,
        out_shape=jax.ShapeDtypeStruct((n, GLA_VAL), BF16),
        grid=(nb, pairs),
        in_specs=[blk(LANES), blk(LANES), blk(2 * GLA_DV), blk(LANES), blk(LANES),
                  blk(2 * GLA_DV), pl.BlockSpec((1, 2 * GLA_DV), lambda b, p: (0, p))],
        out_specs=blk(2 * GLA_DV),
        scratch_shapes=[pltpu.VMEM((lp, 2 * GLA_DV), F32),
                        pltpu.VMEM((2, GLA_DV, LANES), F32)],
        compiler_params=pltpu.CompilerParams(
            dimension_semantics=("parallel", "parallel"), vmem_limit_bytes=VMEM_LIMIT),
        name="gla",
    )(qa, ka, va, gf, gb, sr, gn)


def _attn_body(q_ref, k_ref, vt_ref, o_ref, *s_ref, padf, online):
    tq = q_ref.shape[0]
    lp = k_ref.shape[0]
    group = ATT_HEADS // ATT_KV_HEADS
    cols = group * tq
    n_main = (lp - LANES) // TK_ATT
    low = lax.broadcasted_iota(jnp.int32, (tq, LANES), 1) < HEAD_DIM
    if online:
        (s_ref,) = s_ref

    for kvh in range(ATT_KV_HEADS):
        hs = slice(kvh * LANES, (kvh + 1) * LANES)
        parts = []
        for s in range(group // 2):
            slab = kvh * (group // 2) + s
            q = q_ref[:, slab * LANES:(slab + 1) * LANES]
            zero = jnp.zeros_like(q)
            parts += [jnp.where(low, q, zero), jnp.where(low, zero, q)]
        qs = jnp.concatenate(parts, axis=0)

        def scores(start, size, qs=qs, hs=hs):
            return _dot_nt(k_ref[pl.ds(start, size), hs], qs)

        def absorb(s, start, size, m, acc, hs=hs):
            if not online:
                return m, acc + _dot(vt_ref[hs, pl.ds(start, size)], jnp.exp2(s).astype(BF16))
            m_new = jnp.maximum(m, jnp.max(s, axis=0, keepdims=True))
            p = jnp.exp2(s - m_new).astype(BF16)
            pv = _dot(vt_ref[hs, pl.ds(start, size)], p)
            return m_new, acc * jnp.exp2(m - m_new) + pv

        def main_start(j):
            return pl.multiple_of(LANES + j * TK_ATT, LANES)

        def masked(s):
            return jnp.where(lax.broadcasted_iota(jnp.int32, s.shape, 0) >= padf, s, -1e30)

        acc = jnp.zeros((LANES, cols), F32)
        if online:
            m, acc = absorb(masked(scores(0, LANES)), 0, LANES,
                            jnp.full((1, cols), -jnp.inf, F32), acc)
            assert n_main % 2 == 0
            s_ref[0] = scores(main_start(0), TK_ATT)

            def pair(i, carry, last):
                m, acc = carry
                s_ref[1] = scores(main_start(2 * i + 1), TK_ATT)
                m, acc = absorb(s_ref[0], main_start(2 * i), TK_ATT, m, acc)
                if not last:
                    s_ref[0] = scores(main_start(2 * i + 2), TK_ATT)
                return absorb(s_ref[1], main_start(2 * i + 1), TK_ATT, m, acc)

            m, acc = lax.fori_loop(0, n_main // 2 - 1, lambda i, c: pair(i, c, False), (m, acc))
            _, acc = pair(n_main // 2 - 1, (m, acc), True)
        else:
            first = LANES + TK_ATT
            _, acc = absorb(masked(scores(0, first)), 0, first, None, acc)

            def direct(j, acc):
                _, acc = absorb(scores(main_start(j), TK_ATT), main_start(j), TK_ATT, None, acc)
                return acc

            acc = lax.fori_loop(1, n_main, direct, acc, unroll=n_main - 1)
        out_t = acc[:HEAD_DIM] / acc[HEAD_DIM:]
        for s in range(group // 2):
            slab = kvh * (group // 2) + s
            pair_t = jnp.concatenate([out_t[:, (2 * s) * tq:(2 * s + 1) * tq],
                                      out_t[:, (2 * s + 1) * tq:(2 * s + 2) * tq]], axis=0)
            o_ref[:, slab * LANES:(slab + 1) * LANES] = pair_t.T.astype(BF16)


def _attn(qb, kb, vt, lp, padf, online):
    n = qb.shape[0]
    nb = n // lp
    assert lp % TQ_ATT == 0 and (lp - LANES) % TK_ATT == 0
    nq = lp // TQ_ATT
    q_spec = pl.BlockSpec((TQ_ATT, ATT_Q), lambda b, i: (b * nq + i, 0))
    return pl.pallas_call(
        functools.partial(_attn_body, padf=padf, online=online),
        out_shape=jax.ShapeDtypeStruct((n, ATT_Q), BF16),
        grid=(nb, nq),
        in_specs=[q_spec, pl.BlockSpec((lp, kb.shape[1]), lambda b, i: (b, 0)),
                  pl.BlockSpec((vt.shape[0], lp), lambda b, i: (0, b))],
        out_specs=q_spec,
        scratch_shapes=([pltpu.VMEM((2, TK_ATT, ATT_HEADS // ATT_KV_HEADS * TQ_ATT), F32)]
                        if online else []),
        compiler_params=pltpu.CompilerParams(
            dimension_semantics=("parallel", "arbitrary"), vmem_limit_bytes=VMEM_LIMIT),
        name="attn",
    )(qb, kb, vt)


def _merge_body(h_ref, g_ref, a_ref, b_ref, wg_ref, bm_ref, wpa_ref, wpb_ref, wo_ref, o_ref,
                *, lp, padf):
    tm, d = h_ref.shape
    block_pos = (pl.program_id(0) % (lp // tm)) * tm
    for r in range(tm // TS_MERGE):
        rows = slice(r * TS_MERGE, (r + 1) * TS_MERGE)
        x = h_ref[rows, :]
        pa = _dot(a_ref[rows, :], wpa_ref[...])
        pb = _dot(b_ref[rows, :], wpb_ref[...])
        zb = _rmsnorm_rows(x, g_ref[...]).astype(BF16)
        gate_a = jax.nn.sigmoid(_dot(zb, wg_ref[:, :d]) + bm_ref[:, :d])
        gate_b = jax.nn.sigmoid(_dot(zb, wg_ref[:, d:]) + bm_ref[:, d:])
        out = _dot((gate_a * pa + gate_b * pb).astype(BF16), wo_ref[...])
        pos = block_pos + r * TS_MERGE + lax.broadcasted_iota(jnp.int32, (TS_MERGE, 1), 0)
        o_ref[rows, :] = x + jnp.where(pos >= padf, out, 0.0)


def _merge(hp, gain, a, b, wg, bm, wpa, wpb, wo, lp, padf, layer):
    n, d = hp.shape
    lead = (layer,)
    assert lp % TM_TOK == 0 and TM_TOK % TS_MERGE == 0

    def row(c):
        return pl.BlockSpec((TM_TOK, c), lambda i: (i, 0))

    return pl.pallas_call(
        functools.partial(_merge_body, lp=lp, padf=padf),
        out_shape=jax.ShapeDtypeStruct((n, d), F32),
        grid=(n // TM_TOK,),
        in_specs=[row(d), _resident((1, d)), row(GLA_VAL), row(ATT_Q), _resident(wg.shape, lead),
                  _resident(bm.shape, lead), _resident(wpa.shape, lead),
                  _resident(wpb.shape, lead), _resident(wo.shape, lead)],
        out_specs=row(d),
        compiler_params=pltpu.CompilerParams(
            dimension_semantics=("parallel",), vmem_limit_bytes=VMEM_LIMIT),
        name="merge",
    )(hp, gain, a, b, wg, bm, wpa, wpb, wo)


def _final_body(h_ref, g_ref, o_ref):
    o_ref[...] = _rmsnorm_rows(h_ref[...], g_ref[...])


def _final(hp, gain, nb, n_tok, lp, skip):
    d = hp.shape[1]
    assert n_tok % TM_FIN == 0
    per_out = n_tok // TM_FIN
    return pl.pallas_call(
        _final_body,
        out_shape=jax.ShapeDtypeStruct((nb * n_tok, d), F32),
        grid=(nb, per_out),
        in_specs=[pl.BlockSpec((pl.Element(TM_FIN), pl.Element(d)),
                               lambda b, i: (pl.multiple_of(b * lp + skip + i * TM_FIN, LANES), 0)),
                  _resident((1, d))],
        out_specs=pl.BlockSpec((TM_FIN, d), lambda b, i: (b * per_out + i, 0)),
        compiler_params=pltpu.CompilerParams(
            dimension_semantics=("parallel", "parallel"), vmem_limit_bytes=VMEM_LIMIT),
        name="final_norm",
    )(hp, gain)


def _rope_tables(n_tok, lp, padf):
    quarter = HEAD_DIM // 4
    pos = jnp.arange(n_tok)
    inv = ROPE_THETA ** (-jnp.arange(0, HEAD_DIM // 2, 2, dtype=F32) / (HEAD_DIM // 2))
    ang_r = (pos // GRID_W).astype(F32)[:, None] * inv
    ang_c = (pos % GRID_W).astype(F32)[:, None] * inv
    cos = jnp.concatenate([jnp.cos(ang_r)] * 2 + [jnp.cos(ang_c)] * 2, axis=-1)
    sin = jnp.concatenate([-jnp.sin(ang_r), jnp.sin(ang_r), -jnp.sin(ang_c), jnp.sin(ang_c)],
                          axis=-1)
    lead = padf + N_META
    cos = jnp.concatenate([jnp.ones((lead, HEAD_DIM), F32), cos], axis=0)
    sin = jnp.concatenate([jnp.zeros((lead, HEAD_DIM), F32), sin], axis=0)
    assert cos.shape == (lp, 4 * quarter)
    return jnp.tile(cos, (1, 2)), jnp.tile(sin, (1, 2))


def _prep_w_in(w_in):
    sizes = (GLA_KEY, GLA_KEY, GLA_VAL, GLA_VAL, GLA_RANK, GLA_RANK, ATT_Q, ATT_KV, ATT_KV)
    offs = np.concatenate([[0], np.cumsum(sizes)])
    qa, ka, va, ra, lrf, lrb, qb, kb, vb = (w_in[..., offs[i]:offs[i + 1]] for i in range(9))

    def spread(w, twice):
        heads = [w[..., h * HEAD_DIM:(h + 1) * HEAD_DIM] for h in range(ATT_KV_HEADS)]
        return jnp.concatenate([x for h in heads for x in (h, h if twice else jnp.zeros_like(h))],
                               axis=-1)

    lr = jnp.concatenate([lrf, lrb, jnp.zeros(lrf.shape[:-1] + (LANES - 2 * GLA_RANK,), F32)], -1)
    w_all = jnp.concatenate([qa, ka, va, ra, qb, spread(kb, True), lr], axis=-1).astype(BF16)
    return w_all, jnp.swapaxes(spread(vb, False), -1, -2).astype(BF16)


def _prep_gate(w2, b2):
    depth = w2.shape[0]
    w = jnp.zeros((depth, LANES, 2 * GLA_KEY), F32)
    w = w.at[:, :GLA_RANK, :GLA_KEY].set(w2[:, 0]).at[:, GLA_RANK:2 * GLA_RANK, GLA_KEY:].set(w2[:, 1])
    return w.astype(BF16), b2.reshape(depth, 1, 2 * GLA_KEY)


def kernel(x, meta_tokens, norm_gains, ffn_w_gate, ffn_w_up, ffn_w_down, w_in, gla_w2, gla_b2,
           gla_gn, q_norm, k_norm, w_pa, w_pb, b_merge, w_out, final_norm):
    nb, n_tok, d = x.shape
    depth = w_in.shape[0]
    lp = -(-(n_tok + N_META) // LANES) * LANES
    padf = lp - n_tok - N_META
    assert (padf + N_META) % CHUNK == 0 and n_tok % GRID_W == 0

    meta = jnp.broadcast_to(meta_tokens[None].astype(x.dtype), (nb, N_META, d))
    hp = jnp.concatenate([jnp.zeros((nb, padf, d), x.dtype), meta, x], axis=1).reshape(nb * lp, d)
    cos_t, sin_t = _rope_tables(n_tok, lp, padf)

    g_split = w_in.shape[2] - 2 * d
    ffn_w = tuple(w.astype(BF16) for w in (ffn_w_gate, ffn_w_up, ffn_w_down))
    w_all, wvt = _prep_w_in(w_in[:, :, :g_split])
    w2, b2 = _prep_gate(gla_w2, gla_b2)
    merge_w = (w_in[:, :, g_split:].astype(BF16), b_merge.reshape(depth, 1, 2 * d),
               w_pa.astype(BF16), w_pb.astype(BF16), w_out.astype(BF16))
    for l in range(depth):
        gains = norm_gains[l].reshape(3, 1, d)
        hp = _ffn(hp, gains[0], *ffn_w, (l, 0))
        qa, ka, va, sr, gf, gb, qb, kb, vt = _proj(
            hp, gains[1], w_all, wvt, w2, b2,
            jnp.tile(q_norm[l], 2).reshape(1, LANES), jnp.tile(k_norm[l], 2).reshape(1, LANES),
            cos_t, sin_t, lp, l)
        a = _gla(qa, ka, va, gf, gb, sr, gla_gn[l].reshape(1, GLA_VAL), lp)
        small = (jnp.max(jnp.abs(q_norm[l])) * jnp.max(jnp.abs(k_norm[l])) * ATT_SCORE_BOUND
                 <= ATT_SAFE_SCORE)
        b = lax.cond(small, functools.partial(_attn, lp=lp, padf=padf, online=False),
                     functools.partial(_attn, lp=lp, padf=padf, online=True), qb, kb, vt)
        hp = _merge(hp, gains[1], a, b, *merge_w, lp, padf, l)
        hp = _ffn(hp, gains[2], *ffn_w, (l, 1))
    out = _final(hp, final_norm.reshape(1, d), nb, n_tok, lp, padf + N_META)
    return out.reshape(nb, n_tok, d)
```

```python
import functools

import jax
import jax.numpy as jnp
import numpy as np
from jax import lax
from jax.experimental import pallas as pl
from jax.experimental.pallas import tpu as pltpu

F32 = jnp.float32
BF16 = jnp.bfloat16

N_META = 16
GRID_W = 64
GLA_HEADS = 4
GLA_DK = 64
GLA_DV = 128
GLA_KEY = GLA_HEADS * GLA_DK
GLA_VAL = GLA_HEADS * GLA_DV
GLA_RANK = 16
GLA_TAU = 16.0
CHUNK = 64
ATT_HEADS = 8
ATT_KV_HEADS = 2
HEAD_DIM = 64
ATT_Q = ATT_HEADS * HEAD_DIM
ATT_KV = ATT_KV_HEADS * HEAD_DIM
ROPE_THETA = 10000.0
EPS = 1e-6

LANES = 128
MXU_COLS = 256
VMEM_LIMIT = 56 * 1024 * 1024

TM_FFN = 1536
TS_FFN = 192
TM_TOK = 1056
TS_MERGE = 176
TM_PROJ = 1408
TS_PROJ = 512
TQ_ATT = 384
TK_ATT = 512
TM_FIN = 1024
GLA_ROWS = 256
GLA_LEAD = 128
GLA_UNROLL = 16


def _resident(shape, lead=()):
    tail = tuple(shape[len(lead):])
    return pl.BlockSpec((None,) * len(lead) + tail, lambda *_: tuple(lead) + (0,) * len(tail),
                        pipeline_mode=pl.Buffered(1))


def _rmsnorm_rows(x, gain):
    return x * lax.rsqrt(jnp.mean(x * x, axis=-1, keepdims=True) + EPS) * gain


def _dot(a, b):
    return jnp.dot(a, b, preferred_element_type=F32)


def _dot_nt(a, b):
    return lax.dot_general(a, b, (((1,), (1,)), ((), ())), preferred_element_type=F32)


def _dot_tn(a, b):
    return lax.dot_general(a, b, (((0,), (0,)), ((), ())), preferred_element_type=F32)


def _silu(x):
    return x * jax.nn.sigmoid(x)


def _ffn_body(x_ref, g_ref, wg_ref, wu_ref, wd_ref, o_ref):
    for r in range(x_ref.shape[0] // TS_FFN):
        rows = slice(r * TS_FFN, (r + 1) * TS_FFN)
        x = x_ref[rows, :]
        xb = _rmsnorm_rows(x, g_ref[...]).astype(BF16)
        gate = _dot(xb, wg_ref[...])
        up = _dot(xb, wu_ref[...])
        o_ref[rows, :] = x + 0.5 * _dot((_silu(gate) * up).astype(BF16), wd_ref[...])


def _ffn(hp, gain, wg, wu, wd, which):
    n, d = hp.shape
    d_ff = wg.shape[-1]
    assert n % TM_FFN == 0 and TM_FFN % TS_FFN == 0 and d_ff % MXU_COLS == 0
    row = pl.BlockSpec((TM_FFN, d), lambda i: (i, 0))
    return pl.pallas_call(
        _ffn_body,
        out_shape=jax.ShapeDtypeStruct((n, d), F32),
        grid=(n // TM_FFN,),
        in_specs=[row, _resident((1, d)), _resident(wg.shape, which), _resident(wu.shape, which),
                  _resident(wd.shape, which)],
        out_specs=row,
        compiler_params=pltpu.CompilerParams(
            dimension_semantics=("parallel",), vmem_limit_bytes=VMEM_LIMIT),
        name="ffn",
    )(hp, gain, wg, wu, wd)


_C_QA = 0
_C_KA = _C_QA + GLA_KEY
_C_VA = _C_KA + GLA_KEY
_C_RA = _C_VA + GLA_VAL
_C_QB = _C_RA + GLA_VAL
_C_KB = _C_QB + ATT_Q
_C_LR = _C_KB + 2 * ATT_KV
_C_END = _C_LR + LANES
_Q_SCALE = HEAD_DIM ** -0.5 * float(np.log2(np.e))
ATT_SCORE_BOUND = 1.05 * HEAD_DIM * _Q_SCALE
ATT_SAFE_SCORE = 60.0


def _proj_body(h_ref, g_ref, w_ref, wvt_ref, w2_ref, b2_ref, gq_ref, gk_ref, cos_ref, sin_ref,
               qa_ref, ka_ref, va_ref, sr_ref, gf_ref, gb_ref, qb_ref, kb_ref, vt_ref):
    tm = h_ref.shape[0]
    for lo_row in range(0, tm, TS_PROJ):
        _proj_rows(slice(lo_row, min(lo_row + TS_PROJ, tm)), h_ref, g_ref, w_ref, wvt_ref, w2_ref,
                   b2_ref, gq_ref, gk_ref, cos_ref, sin_ref, qa_ref, ka_ref, va_ref, sr_ref,
                   gf_ref, gb_ref, qb_ref, kb_ref, vt_ref)


def _proj_rows(rows, h_ref, g_ref, w_ref, wvt_ref, w2_ref, b2_ref, gq_ref, gk_ref, cos_ref,
               sin_ref, qa_ref, ka_ref, va_ref, sr_ref, gf_ref, gb_ref, qb_ref, kb_ref, vt_ref):
    zb = _rmsnorm_rows(h_ref[rows, :], g_ref[...]).astype(BF16)

    def proj(lo, hi):
        return _dot(zb, w_ref[:, lo:hi])

    cos = cos_ref[rows, :]
    sin = sin_ref[rows, :]
    lane = lax.broadcasted_iota(jnp.int32, cos.shape, 1)
    low_head = lane < HEAD_DIM
    first_half = (lane % (HEAD_DIM // 2)) < (HEAD_DIM // 4)

    def norm_rope(x, gain, scale):
        x2 = x * x
        s_lo = jnp.sum(jnp.where(low_head, x2, 0.0), axis=-1, keepdims=True)
        s_hi = jnp.sum(jnp.where(low_head, 0.0, x2), axis=-1, keepdims=True)
        inv = jnp.where(low_head, lax.rsqrt(s_lo * (1.0 / HEAD_DIM) + EPS),
                        lax.rsqrt(s_hi * (1.0 / HEAD_DIM) + EPS))
        y = x * inv * gain
        partner = jnp.where(first_half, pltpu.roll(y, LANES - HEAD_DIM // 4, 1),
                            pltpu.roll(y, HEAD_DIM // 4, 1))
        return (y * cos + partner * sin) * scale

    for base, width, out_ref, gain_ref, scale in ((_C_QB, ATT_Q, qb_ref, gq_ref, _Q_SCALE),
                                                  (_C_KB, 2 * ATT_KV, kb_ref, gk_ref, 1.0)):
        for c in range(width // MXU_COLS):
            x = proj(base + c * MXU_COLS, base + (c + 1) * MXU_COLS)
            for s in range(MXU_COLS // LANES):
                sl = slice(c * MXU_COLS + s * LANES, c * MXU_COLS + (s + 1) * LANES)
                out_ref[rows, sl] = norm_rope(x[:, s * LANES:(s + 1) * LANES], gain_ref[...],
                                              scale).astype(BF16)
    lr = proj(_C_LR, _C_END).astype(BF16)
    pre = _dot(lr, w2_ref[...]) + b2_ref[...]
    lg = (jnp.minimum(pre, 0.0) - jnp.log(1.0 + jnp.exp(-jnp.abs(pre)))) * (1.0 / GLA_TAU)
    gf_ref[rows, :] = lg[:, :GLA_KEY]
    gb_ref[rows, :] = lg[:, GLA_KEY:]
    qa_ref[rows, :] = (proj(_C_QA, _C_KA) * GLA_DK ** -0.5).astype(BF16)
    ka_ref[rows, :] = proj(_C_KA, _C_VA).astype(BF16)
    va_ref[rows, :] = proj(_C_VA, _C_RA).astype(BF16)
    sr_ref[rows, :] = _silu(proj(_C_RA, _C_QB)).astype(BF16)
    vt = _dot_nt(wvt_ref[...], zb)
    row_i = lax.broadcasted_iota(jnp.int32, vt.shape, 0)
    vt_ref[:, rows] = jnp.where(row_i % LANES < HEAD_DIM, vt, 1.0).astype(BF16)


def _proj(hp, gain, w_all, wvt, w2, b2, gq, gk, cos_t, sin_t, lp, layer):
    n, d = hp.shape
    assert lp % TM_PROJ == 0
    tiles_per_seq = lp // TM_PROJ

    def row(c):
        return pl.BlockSpec((TM_PROJ, c), lambda i: (i, 0))

    table = pl.BlockSpec((TM_PROJ, LANES), lambda i: (i % tiles_per_seq, 0))
    outs = [(GLA_KEY, BF16), (GLA_KEY, BF16), (GLA_VAL, BF16), (GLA_VAL, BF16),
            (GLA_KEY, F32), (GLA_KEY, F32), (ATT_Q, BF16), (2 * ATT_KV, BF16)]
    vt_rows = wvt.shape[1]
    return pl.pallas_call(
        _proj_body,
        out_shape=[jax.ShapeDtypeStruct((n, c), t) for c, t in outs]
        + [jax.ShapeDtypeStruct((vt_rows, n), BF16)],
        grid=(n // TM_PROJ,),
        in_specs=[row(d), _resident((1, d)), _resident(w_all.shape, (layer,)),
                  _resident(wvt.shape, (layer,)), _resident(w2.shape, (layer,)),
                  _resident(b2.shape, (layer,)), _resident((1, LANES)),
                  _resident((1, LANES)), table, table],
        out_specs=[row(c) for c, _ in outs]
        + [pl.BlockSpec((vt_rows, TM_PROJ), lambda i: (0, i))],
        compiler_params=pltpu.CompilerParams(
            dimension_semantics=("parallel",), vmem_limit_bytes=VMEM_LIMIT),
        name="proj",
    )(hp, gain, w_all, wvt, w2, b2, gq, gk, cos_t, sin_t)


def _gla_body(q_ref, k_ref, v_ref, gf_ref, gb_ref, sr_ref, gn_ref, o_ref, ob_ref, st_ref):
    lp = q_ref.shape[0]
    assert GLA_LEAD % CHUNK == 0 and (lp - GLA_LEAD) % GLA_ROWS == 0
    n_main = (lp - GLA_LEAD) // GLA_ROWS

    def block(start, rows_n, forward, final):
        nch = rows_n // CHUNK
        rows = pl.ds(start, rows_n)
        pos = lax.broadcasted_iota(jnp.int32, (rows_n, LANES), 0) % CHUNK
        lane = lax.broadcasted_iota(jnp.int32, (rows_n, LANES), 1)

        bc = (gf_ref if forward else gb_ref)[rows, :]
        shift = 1
        while shift < CHUNK:
            if forward:
                moved, ok = pltpu.roll(bc, shift, 0), pos >= shift
            else:
                moved, ok = pltpu.roll(bc, rows_n - shift, 0), pos < CHUNK - shift
            bc = bc + jnp.where(ok, moved, 0.0)
            shift *= 2
        edge = CHUNK - 1 if forward else 0
        bt3 = bc.reshape(nch, CHUNK, LANES)[:, edge:edge + 1, :]
        bt = jnp.broadcast_to(bt3, (nch, CHUNK, LANES)).reshape(rows_n, LANES)
        dec = jnp.exp(bt3)

        q = q_ref[rows, :].astype(F32)
        k = k_ref[rows, :].astype(F32)
        qd = q * jnp.exp(bc)
        ki = (k * jnp.exp(-bc)).astype(BF16)
        ke = (k * jnp.exp(bt - bc)).astype(BF16)

        r_i = lax.broadcasted_iota(jnp.int32, (rows_n, rows_n), 0)
        c_i = lax.broadcasted_iota(jnp.int32, (rows_n, rows_n), 1)
        same_chunk = (r_i // CHUNK) == (c_i // CHUNK)
        att_ok = same_chunk & ((r_i >= c_i) if forward else (r_i <= c_i))
        wide = (rows_n, nch * LANES)
        own_block = (lax.broadcasted_iota(jnp.int32, wide, 0) // CHUNK
                     == lax.broadcasted_iota(jnp.int32, wide, 1) // LANES)
        ke_blk = jnp.where(own_block, jnp.tile(ke, (1, nch)), jnp.zeros(wide, BF16))

        entering = []
        for h in range(2):
            hs = slice(h * GLA_DV, (h + 1) * GLA_DV)
            kv_t = _dot_tn(v_ref[rows, hs], ke_blk)
            st = st_ref[h]
            states = [None] * nch
            for c in (range(nch) if forward else range(nch - 1, -1, -1)):
                states[c] = st
                st = st * dec[c] + kv_t[:, c * LANES:(c + 1) * LANES]
            st_ref[h] = st
            entering.append(jnp.concatenate(states, axis=1).astype(BF16))

        for h in range(2):
            hs = slice(h * GLA_DV, (h + 1) * GLA_DV)
            qh = jnp.where((lane < GLA_DK) if h == 0 else (lane >= GLA_DK), qd, 0.0).astype(BF16)
            att = jnp.where(att_ok, _dot_nt(qh, ki), 0.0).astype(BF16)
            q_blk = jnp.where(own_block, jnp.tile(qh, (1, nch)), jnp.zeros(wide, BF16))
            o = _dot(att, v_ref[rows, hs]) + _dot_nt(q_blk, entering[h])
            if not final:
                ob_ref[rows, hs] = o
            else:
                o = o + ob_ref[rows, hs]
                o = o * lax.rsqrt(jnp.mean(o * o, axis=-1, keepdims=True) + EPS) * gn_ref[:, hs]
                o_ref[rows, hs] = (o * sr_ref[rows, hs].astype(F32)).astype(BF16)

    def main_start(j):
        return pl.multiple_of(GLA_LEAD + j * GLA_ROWS, CHUNK)

    st_ref[...] = jnp.zeros(st_ref.shape, F32)

    def bwd(i, carry):
        block(main_start(n_main - 1 - i), GLA_ROWS, False, False)
        return carry

    lax.fori_loop(0, n_main, bwd, 0, unroll=GLA_UNROLL)
    block(0, GLA_LEAD, False, False)

    st_ref[...] = jnp.zeros(st_ref.shape, F32)
    block(0, GLA_LEAD, True, True)

    def fwd(i, carry):
        block(main_start(i), GLA_ROWS, True, True)
        return carry

    lax.fori_loop(0, n_main, fwd, 0, unroll=GLA_UNROLL)


def _gla(qa, ka, va, gf, gb, sr, gn, lp):
    n = qa.shape[0]
    nb = n // lp
    pairs = GLA_HEADS // 2

    def blk(c):
        return pl.BlockSpec((lp, c), lambda b, p: (b, p))

    return pl.pallas_call(
        _gla_body,
        out_shape=jax.ShapeDtypeStruct((n, GLA_VAL), BF16),
        grid=(nb, pairs),
        in_specs=[blk(LANES), blk(LANES), blk(2 * GLA_DV), blk(LANES), blk(LANES),
                  blk(2 * GLA_DV), pl.BlockSpec((1, 2 * GLA_DV), lambda b, p: (0, p))],
        out_specs=blk(2 * GLA_DV),
        scratch_shapes=[pltpu.VMEM((lp, 2 * GLA_DV), F32),
                        pltpu.VMEM((2, GLA_DV, LANES), F32)],
        compiler_params=pltpu.CompilerParams(
            dimension_semantics=("parallel", "parallel"), vmem_limit_bytes=VMEM_LIMIT),
        name="gla",
    )(qa, ka, va, gf, gb, sr, gn)


def _attn_body(q_ref, k_ref, vt_ref, o_ref, *s_ref, padf, online):
    tq = q_ref.shape[0]
    lp = k_ref.shape[0]
    group = ATT_HEADS // ATT_KV_HEADS
    cols = group * tq
    n_main = (lp - LANES) // TK_ATT
    low = lax.broadcasted_iota(jnp.int32, (tq, LANES), 1) < HEAD_DIM
    if online:
        (s_ref,) = s_ref

    for kvh in range(ATT_KV_HEADS):
        hs = slice(kvh * LANES, (kvh + 1) * LANES)
        parts = []
        for s in range(group // 2):
            slab = kvh * (group // 2) + s
            q = q_ref[:, slab * LANES:(slab + 1) * LANES]
            zero = jnp.zeros_like(q)
            parts += [jnp.where(low, q, zero), jnp.where(low, zero, q)]
        qs = jnp.concatenate(parts, axis=0)

        def scores(start, size, qs=qs, hs=hs):
            return _dot_nt(k_ref[pl.ds(start, size), hs], qs)

        def absorb(s, start, size, m, acc, hs=hs):
            if not online:
                return m, acc + _dot(vt_ref[hs, pl.ds(start, size)], jnp.exp2(s).astype(BF16))
            m_new = jnp.maximum(m, jnp.max(s, axis=0, keepdims=True))
            p = jnp.exp2(s - m_new).astype(BF16)
            pv = _dot(vt_ref[hs, pl.ds(start, size)], p)
            return m_new, acc * jnp.exp2(m - m_new) + pv

        def main_start(j):
            return pl.multiple_of(LANES + j * TK_ATT, LANES)

        def masked(s):
            return jnp.where(lax.broadcasted_iota(jnp.int32, s.shape, 0) >= padf, s, -1e30)

        acc = jnp.zeros((LANES, cols), F32)
        if online:
            m, acc = absorb(masked(scores(0, LANES)), 0, LANES,
                            jnp.full((1, cols), -jnp.inf, F32), acc)
            assert n_main % 2 == 0
            s_ref[0] = scores(main_start(0), TK_ATT)

            def pair(i, carry, last):
                m, acc = carry
                s_ref[1] = scores(main_start(2 * i + 1), TK_ATT)
                m, acc = absorb(s_ref[0], main_start(2 * i), TK_ATT, m, acc)
                if not last:
                    s_ref[0] = scores(main_start(2 * i + 2), TK_ATT)
                return absorb(s_ref[1], main_start(2 * i + 1), TK_ATT, m, acc)

            m, acc = lax.fori_loop(0, n_main // 2 - 1, lambda i, c: pair(i, c, False), (m, acc))
            _, acc = pair(n_main // 2 - 1, (m, acc), True)
        else:
            first = LANES + TK_ATT
            _, acc = absorb(masked(scores(0, first)), 0, first, None, acc)

            def direct(j, acc):
                _, acc = absorb(scores(main_start(j), TK_ATT), main_start(j), TK_ATT, None, acc)
                return acc

            acc = lax.fori_loop(1, n_main, direct, acc, unroll=n_main - 1)
        out_t = acc[:HEAD_DIM] / acc[HEAD_DIM:]
        for s in range(group // 2):
            slab = kvh * (group // 2) + s
            pair_t = jnp.concatenate([out_t[:, (2 * s) * tq:(2 * s + 1) * tq],
                                      out_t[:, (2 * s + 1) * tq:(2 * s + 2) * tq]], axis=0)
            o_ref[:, slab * LANES:(slab + 1) * LANES] = pair_t.T.astype(BF16)


def _attn(qb, kb, vt, lp, padf, online):
    n = qb.shape[0]
    nb = n // lp
    assert lp % TQ_ATT == 0 and (lp - LANES) % TK_ATT == 0
    nq = lp // TQ_ATT
    q_spec = pl.BlockSpec((TQ_ATT, ATT_Q), lambda b, i: (b * nq + i, 0))
    return pl.pallas_call(
        functools.partial(_attn_body, padf=padf, online=online),
        out_shape=jax.ShapeDtypeStruct((n, ATT_Q), BF16),
        grid=(nb, nq),
        in_specs=[q_spec, pl.BlockSpec((lp, kb.shape[1]), lambda b, i: (b, 0)),
                  pl.BlockSpec((vt.shape[0], lp), lambda b, i: (0, b))],
        out_specs=q_spec,
        scratch_shapes=([pltpu.VMEM((2, TK_ATT, ATT_HEADS // ATT_KV_HEADS * TQ_ATT), F32)]
                        if online else []),
        compiler_params=pltpu.CompilerParams(
            dimension_semantics=("parallel", "arbitrary"), vmem_limit_bytes=VMEM_LIMIT),
        name="attn",
    )(qb, kb, vt)


def _merge_body(h_ref, g_ref, a_ref, b_ref, wg_ref, bm_ref, wpa_ref, wpb_ref, wo_ref, o_ref,
                *, lp, padf):
    tm, d = h_ref.shape
    block_pos = (pl.program_id(0) % (lp // tm)) * tm
    for r in range(tm // TS_MERGE):
        rows = slice(r * TS_MERGE, (r + 1) * TS_MERGE)
        x = h_ref[rows, :]
        pa = _dot(a_ref[rows, :], wpa_ref[...])
        pb = _dot(b_ref[rows, :], wpb_ref[...])
        zb = _rmsnorm_rows(x, g_ref[...]).astype(BF16)
        gate_a = jax.nn.sigmoid(_dot(zb, wg_ref[:, :d]) + bm_ref[:, :d])
        gate_b = jax.nn.sigmoid(_dot(zb, wg_ref[:, d:]) + bm_ref[:, d:])
        out = _dot((gate_a * pa + gate_b * pb).astype(BF16), wo_ref[...])
        pos = block_pos + r * TS_MERGE + lax.broadcasted_iota(jnp.int32, (TS_MERGE, 1), 0)
        o_ref[rows, :] = x + jnp.where(pos >= padf, out, 0.0)


def _merge(hp, gain, a, b, wg, bm, wpa, wpb, wo, lp, padf, layer):
    n, d = hp.shape
    lead = (layer,)
    assert lp % TM_TOK == 0 and TM_TOK % TS_MERGE == 0

    def row(c):
        return pl.BlockSpec((TM_TOK, c), lambda i: (i, 0))

    return pl.pallas_call(
        functools.partial(_merge_body, lp=lp, padf=padf),
        out_shape=jax.ShapeDtypeStruct((n, d), F32),
        grid=(n // TM_TOK,),
        in_specs=[row(d), _resident((1, d)), row(GLA_VAL), row(ATT_Q), _resident(wg.shape, lead),
                  _resident(bm.shape, lead), _resident(wpa.shape, lead),
                  _resident(wpb.shape, lead), _resident(wo.shape, lead)],
        out_specs=row(d),
        compiler_params=pltpu.CompilerParams(
            dimension_semantics=("parallel",), vmem_limit_bytes=VMEM_LIMIT),
        name="merge",
    )(hp, gain, a, b, wg, bm, wpa, wpb, wo)


def _final_body(h_ref, g_ref, o_ref):
    o_ref[...] = _rmsnorm_rows(h_ref[...], g_ref[...])


def _final(hp, gain, nb, n_tok, lp, skip):
    d = hp.shape[1]
    assert n_tok % TM_FIN == 0
    per_out = n_tok // TM_FIN
    return pl.pallas_call(
        _final_body,
        out_shape=jax.ShapeDtypeStruct((nb * n_tok, d), F32),
        grid=(nb, per_out),
        in_specs=[pl.BlockSpec((pl.Element(TM_FIN), pl.Element(d)),
                               lambda b, i: (pl.multiple_of(b * lp + skip + i * TM_FIN, LANES), 0)),
                  _resident((1, d))],
        out_specs=pl.BlockSpec((TM_FIN, d), lambda b, i: (b * per_out + i, 0)),
        compiler_params=pltpu.CompilerParams(
            dimension_semantics=("parallel", "parallel"), vmem_limit_bytes=VMEM_LIMIT),
        name="final_norm",
    )(hp, gain)


def _rope_tables(n_tok, lp, padf):
    quarter = HEAD_DIM // 4
    pos = jnp.arange(n_tok)
    inv = ROPE_THETA ** (-jnp.arange(0, HEAD_DIM // 2, 2, dtype=F32) / (HEAD_DIM // 2))
    ang_r = (pos // GRID_W).astype(F32)[:, None] * inv
    ang_c = (pos % GRID_W).astype(F32)[:, None] * inv
    cos = jnp.concatenate([jnp.cos(ang_r)] * 2 + [jnp.cos(ang_c)] * 2, axis=-1)
    sin = jnp.concatenate([-jnp.sin(ang_r), jnp.sin(ang_r), -jnp.sin(ang_c), jnp.sin(ang_c)],
                          axis=-1)
    lead = padf + N_META
    cos = jnp.concatenate([jnp.ones((lead, HEAD_DIM), F32), cos], axis=0)
    sin = jnp.concatenate([jnp.zeros((lead, HEAD_DIM), F32), sin], axis=0)
    assert cos.shape == (lp, 4 * quarter)
    return jnp.tile(cos, (1, 2)), jnp.tile(sin, (1, 2))


def _prep_w_in(w_in):
    sizes = (GLA_KEY, GLA_KEY, GLA_VAL, GLA_VAL, GLA_RANK, GLA_RANK, ATT_Q, ATT_KV, ATT_KV)
    offs = np.concatenate([[0], np.cumsum(sizes)])
    qa, ka, va, ra, lrf, lrb, qb, kb, vb = (w_in[..., offs[i]:offs[i + 1]] for i in range(9))

    def spread(w, twice):
        heads = [w[..., h * HEAD_DIM:(h + 1) * HEAD_DIM] for h in range(ATT_KV_HEADS)]
        return jnp.concatenate([x for h in heads for x in (h, h if twice else jnp.zeros_like(h))],
                               axis=-1)

    lr = jnp.concatenate([lrf, lrb, jnp.zeros(lrf.shape[:-1] + (LANES - 2 * GLA_RANK,), F32)], -1)
    w_all = jnp.concatenate([qa, ka, va, ra, qb, spread(kb, True), lr], axis=-1).astype(BF16)
    return w_all, jnp.swapaxes(spread(vb, False), -1, -2).astype(BF16)


def _prep_gate(w2, b2):
    depth = w2.shape[0]
    w = jnp.zeros((depth, LANES, 2 * GLA_KEY), F32)
    w = w.at[:, :GLA_RANK, :GLA_KEY].set(w2[:, 0]).at[:, GLA_RANK:2 * GLA_RANK, GLA_KEY:].set(w2[:, 1])
    return w.astype(BF16), b2.reshape(depth, 1, 2 * GLA_KEY)


def kernel(x, meta_tokens, norm_gains, ffn_w_gate, ffn_w_up, ffn_w_down, w_in, gla_w2, gla_b2,
           gla_gn, q_norm, k_norm, w_pa, w_pb, b_merge, w_out, final_norm):
    nb, n_tok, d = x.shape
    depth = w_in.shape[0]
    lp = -(-(n_tok + N_META) // LANES) * LANES
    padf = lp - n_tok - N_META
    assert (padf + N_META) % CHUNK == 0 and n_tok % GRID_W == 0

    meta = jnp.broadcast_to(meta_tokens[None].astype(x.dtype), (nb, N_META, d))
    hp = jnp.concatenate([jnp.zeros((nb, padf, d), x.dtype), meta, x], axis=1).reshape(nb * lp, d)
    cos_t, sin_t = _rope_tables(n_tok, lp, padf)

    g_split = w_in.shape[2] - 2 * d
    ffn_w = tuple(w.astype(BF16) for w in (ffn_w_gate, ffn_w_up, ffn_w_down))
    w_all, wvt = _prep_w_in(w_in[:, :, :g_split])
    w2, b2 = _prep_gate(gla_w2, gla_b2)
    merge_w = (w_in[:, :, g_split:].astype(BF16), b_merge.reshape(depth, 1, 2 * d),
               w_pa.astype(BF16), w_pb.astype(BF16), w_out.astype(BF16))
    for l in range(depth):
        gains = norm_gains[l].reshape(3, 1, d)
        hp = _ffn(hp, gains[0], *ffn_w, (l, 0))
        qa, ka, va, sr, gf, gb, qb, kb, vt = _proj(
            hp, gains[1], w_all, wvt, w2, b2,
            jnp.tile(q_norm[l], 2).reshape(1, LANES), jnp.tile(k_norm[l], 2).reshape(1, LANES),
            cos_t, sin_t, lp, l)
        a = _gla(qa, ka, va, gf, gb, sr, gla_gn[l].reshape(1, GLA_VAL), lp)
        small = (jnp.max(jnp.abs(q_norm[l])) * jnp.max(jnp.abs(k_norm[l])) * ATT_SCORE_BOUND
                 <= ATT_SAFE_SCORE)
        b = lax.cond(small, functools.partial(_attn, lp=lp, padf=padf, online=False),
                     functools.partial(_attn, lp=lp, padf=padf, online=True), qb, kb, vt)
        hp = _merge(hp, gains[1], a, b, *merge_w, lp, padf, l)
        hp = _ffn(hp, gains[2], *ffn_w, (l, 1))
    out = _final(hp, final_norm.reshape(1, d), nb, n_tok, lp, padf + N_META)
    return out.reshape(nb, n_tok, d)
```

```python
import functools

import jax
import jax.numpy as jnp
import numpy as np
from jax import lax
from jax.experimental import pallas as pl
from jax.experimental.pallas import tpu as pltpu

F32 = jnp.float32
BF16 = jnp.bfloat16

N_META = 16
GRID_W = 64
GLA_HEADS = 4
GLA_DK = 64
GLA_DV = 128
GLA_KEY = GLA_HEADS * GLA_DK
GLA_VAL = GLA_HEADS * GLA_DV
GLA_RANK = 16
GLA_TAU = 16.0
CHUNK = 64
ATT_HEADS = 8
ATT_KV_HEADS = 2
HEAD_DIM = 64
ATT_Q = ATT_HEADS * HEAD_DIM
ATT_KV = ATT_KV_HEADS * HEAD_DIM
ROPE_THETA = 10000.0
EPS = 1e-6

LANES = 128
MXU_COLS = 256
VMEM_LIMIT = 56 * 1024 * 1024

TM_FFN = 768
TS_FFN = 192
TM_TOK = 1056
TS_MERGE = 176
TM_MF = 528
TM_PROJ = 1408
TS_PROJ = 512
TQ_ATT = 384
TK_ATT = 512
TM_FIN = 1024
GLA_ROWS = 256
GLA_LEAD = 128
GLA_UNROLL = 16


def _resident(shape, lead=()):
    tail = tuple(shape[len(lead):])
    return pl.BlockSpec((None,) * len(lead) + tail, lambda *_: tuple(lead) + (0,) * len(tail),
                        pipeline_mode=pl.Buffered(1))


def _rmsnorm_rows(x, gain):
    return x * lax.rsqrt(jnp.mean(x * x, axis=-1, keepdims=True) + EPS) * gain


def _dot(a, b):
    return jnp.dot(a, b, preferred_element_type=F32)


def _dot_nt(a, b):
    return lax.dot_general(a, b, (((1,), (1,)), ((), ())), preferred_element_type=F32)


def _dot_tn(a, b):
    return lax.dot_general(a, b, (((0,), (0,)), ((), ())), preferred_element_type=F32)


def _silu(x):
    return x * jax.nn.sigmoid(x)


def _ffn_body(x_ref, g_ref, wg_ref, wu_ref, wd_ref, o_ref):
    for r in range(x_ref.shape[0] // TS_FFN):
        rows = slice(r * TS_FFN, (r + 1) * TS_FFN)
        x = x_ref[rows, :]
        xb = _rmsnorm_rows(x, g_ref[...]).astype(BF16)
        gate = _dot(xb, wg_ref[...])
        up = _dot(xb, wu_ref[...])
        o_ref[rows, :] = x + 0.5 * _dot((_silu(gate) * up).astype(BF16), wd_ref[...])


def _ffn(hp, gain, wg, wu, wd, which):
    n, d = hp.shape
    d_ff = wg.shape[-1]
    assert n % TM_FFN == 0 and TM_FFN % TS_FFN == 0 and d_ff % MXU_COLS == 0
    row = pl.BlockSpec((TM_FFN, d), lambda i: (i, 0))
    return pl.pallas_call(
        _ffn_body,
        out_shape=jax.ShapeDtypeStruct((n, d), F32),
        grid=(n // TM_FFN,),
        in_specs=[row, _resident((1, d)), _resident(wg.shape, which), _resident(wu.shape, which),
                  _resident(wd.shape, which)],
        out_specs=row,
        compiler_params=pltpu.CompilerParams(
            dimension_semantics=("parallel",), vmem_limit_bytes=VMEM_LIMIT),
        name="ffn",
    )(hp, gain, wg, wu, wd)


_C_QA = 0
_C_KA = _C_QA + GLA_KEY
_C_VA = _C_KA + GLA_KEY
_C_RA = _C_VA + GLA_VAL
_C_QB = _C_RA + GLA_VAL
_C_KB = _C_QB + ATT_Q
_C_LR = _C_KB + 2 * ATT_KV
_C_END = _C_LR + LANES
_Q_SCALE = HEAD_DIM ** -0.5 * float(np.log2(np.e))
ATT_SCORE_BOUND = 1.05 * HEAD_DIM * _Q_SCALE
ATT_SAFE_SCORE = 60.0


def _proj_body(h_ref, g_ref, w_ref, wvt_ref, w2_ref, b2_ref, gq_ref, gk_ref, cos_ref, sin_ref,
               qa_ref, ka_ref, va_ref, sr_ref, gf_ref, gb_ref, qb_ref, kb_ref, vt_ref):
    tm = h_ref.shape[0]
    for lo_row in range(0, tm, TS_PROJ):
        _proj_rows(slice(lo_row, min(lo_row + TS_PROJ, tm)), h_ref, g_ref, w_ref, wvt_ref, w2_ref,
                   b2_ref, gq_ref, gk_ref, cos_ref, sin_ref, qa_ref, ka_ref, va_ref, sr_ref,
                   gf_ref, gb_ref, qb_ref, kb_ref, vt_ref)


def _proj_rows(rows, h_ref, g_ref, w_ref, wvt_ref, w2_ref, b2_ref, gq_ref, gk_ref, cos_ref,
               sin_ref, qa_ref, ka_ref, va_ref, sr_ref, gf_ref, gb_ref, qb_ref, kb_ref, vt_ref):
    zb = _rmsnorm_rows(h_ref[rows, :], g_ref[...]).astype(BF16)

    def proj(lo, hi):
        return _dot(zb, w_ref[:, lo:hi])

    cos = cos_ref[rows, :]
    sin = sin_ref[rows, :]
    lane = lax.broadcasted_iota(jnp.int32, cos.shape, 1)
    low_head = lane < HEAD_DIM
    first_half = (lane % (HEAD_DIM // 2)) < (HEAD_DIM // 4)

    def norm_rope(x, gain, scale):
        x2 = x * x
        s_lo = jnp.sum(jnp.where(low_head, x2, 0.0), axis=-1, keepdims=True)
        s_hi = jnp.sum(jnp.where(low_head, 0.0, x2), axis=-1, keepdims=True)
        inv = jnp.where(low_head, lax.rsqrt(s_lo * (1.0 / HEAD_DIM) + EPS),
                        lax.rsqrt(s_hi * (1.0 / HEAD_DIM) + EPS))
        y = x * inv * gain
        partner = jnp.where(first_half, pltpu.roll(y, LANES - HEAD_DIM // 4, 1),
                            pltpu.roll(y, HEAD_DIM // 4, 1))
        return (y * cos + partner * sin) * scale

    for base, width, out_ref, gain_ref, scale in ((_C_QB, ATT_Q, qb_ref, gq_ref, _Q_SCALE),
                                                  (_C_KB, 2 * ATT_KV, kb_ref, gk_ref, 1.0)):
        for c in range(width // MXU_COLS):
            x = proj(base + c * MXU_COLS, base + (c + 1) * MXU_COLS)
            for s in range(MXU_COLS // LANES):
                sl = slice(c * MXU_COLS + s * LANES, c * MXU_COLS + (s + 1) * LANES)
                out_ref[rows, sl] = norm_rope(x[:, s * LANES:(s + 1) * LANES], gain_ref[...],
                                              scale).astype(BF16)
    lr = proj(_C_LR, _C_END).astype(BF16)
    pre = _dot(lr, w2_ref[...]) + b2_ref[...]
    lg = (jnp.minimum(pre, 0.0) - jnp.log(1.0 + jnp.exp(-jnp.abs(pre)))) * (1.0 / GLA_TAU)
    gf_ref[rows, :] = lg[:, :GLA_KEY]
    gb_ref[rows, :] = lg[:, GLA_KEY:]
    qa_ref[rows, :] = (proj(_C_QA, _C_KA) * GLA_DK ** -0.5).astype(BF16)
    ka_ref[rows, :] = proj(_C_KA, _C_VA).astype(BF16)
    va_ref[rows, :] = proj(_C_VA, _C_RA).astype(BF16)
    sr_ref[rows, :] = _silu(proj(_C_RA, _C_QB)).astype(BF16)
    vt = _dot_nt(wvt_ref[...], zb)
    row_i = lax.broadcasted_iota(jnp.int32, vt.shape, 0)
    vt_ref[:, rows] = jnp.where(row_i % LANES < HEAD_DIM, vt, 1.0).astype(BF16)


def _proj(hp, gain, w_all, wvt, w2, b2, gq, gk, cos_t, sin_t, lp, layer):
    n, d = hp.shape
    assert lp % TM_PROJ == 0
    tiles_per_seq = lp // TM_PROJ

    def row(c):
        return pl.BlockSpec((TM_PROJ, c), lambda i: (i, 0))

    table = pl.BlockSpec((TM_PROJ, LANES), lambda i: (i % tiles_per_seq, 0))
    outs = [(GLA_KEY, BF16), (GLA_KEY, BF16), (GLA_VAL, BF16), (GLA_VAL, BF16),
            (GLA_KEY, F32), (GLA_KEY, F32), (ATT_Q, BF16), (2 * ATT_KV, BF16)]
    vt_rows = wvt.shape[1]
    return pl.pallas_call(
        _proj_body,
        out_shape=[jax.ShapeDtypeStruct((n, c), t) for c, t in outs]
        + [jax.ShapeDtypeStruct((vt_rows, n), BF16)],
        grid=(n // TM_PROJ,),
        in_specs=[row(d), _resident((1, d)), _resident(w_all.shape, (layer,)),
                  _resident(wvt.shape, (layer,)), _resident(w2.shape, (layer,)),
                  _resident(b2.shape, (layer,)), _resident((1, LANES)),
                  _resident((1, LANES)), table, table],
        out_specs=[row(c) for c, _ in outs]
        + [pl.BlockSpec((vt_rows, TM_PROJ), lambda i: (0, i))],
        compiler_params=pltpu.CompilerParams(
            dimension_semantics=("parallel",), vmem_limit_bytes=VMEM_LIMIT),
        name="proj",
    )(hp, gain, w_all, wvt, w2, b2, gq, gk, cos_t, sin_t)


def _gla_body(q_ref, k_ref, v_ref, gf_ref, gb_ref, sr_ref, gn_ref, o_ref, ob_ref, st_ref):
    lp = q_ref.shape[0]
    assert GLA_LEAD % CHUNK == 0 and (lp - GLA_LEAD) % GLA_ROWS == 0
    n_main = (lp - GLA_LEAD) // GLA_ROWS

    def block(start, rows_n, forward, final):
        nch = rows_n // CHUNK
        rows = pl.ds(start, rows_n)
        pos = lax.broadcasted_iota(jnp.int32, (rows_n, LANES), 0) % CHUNK
        lane = lax.broadcasted_iota(jnp.int32, (rows_n, LANES), 1)

        bc = (gf_ref if forward else gb_ref)[rows, :]
        shift = 1
        while shift < CHUNK:
            if forward:
                moved, ok = pltpu.roll(bc, shift, 0), pos >= shift
            else:
                moved, ok = pltpu.roll(bc, rows_n - shift, 0), pos < CHUNK - shift
            bc = bc + jnp.where(ok, moved, 0.0)
            shift *= 2
        edge = CHUNK - 1 if forward else 0
        bt3 = bc.reshape(nch, CHUNK, LANES)[:, edge:edge + 1, :]
        bt = jnp.broadcast_to(bt3, (nch, CHUNK, LANES)).reshape(rows_n, LANES)
        dec = jnp.exp(bt3)

        q = q_ref[rows, :].astype(F32)
        k = k_ref[rows, :].astype(F32)
        qd = q * jnp.exp(bc)
        ki = (k * jnp.exp(-bc)).astype(BF16)
        ke = (k * jnp.exp(bt - bc)).astype(BF16)

        r_i = lax.broadcasted_iota(jnp.int32, (rows_n, rows_n), 0)
        c_i = lax.broadcasted_iota(jnp.int32, (rows_n, rows_n), 1)
        same_chunk = (r_i // CHUNK) == (c_i // CHUNK)
        att_ok = same_chunk & ((r_i >= c_i) if forward else (r_i <= c_i))
        wide = (rows_n, nch * LANES)
        own_block = (lax.broadcasted_iota(jnp.int32, wide, 0) // CHUNK
                     == lax.broadcasted_iota(jnp.int32, wide, 1) // LANES)
        ke_blk = jnp.where(own_block, jnp.tile(ke, (1, nch)), jnp.zeros(wide, BF16))

        entering = []
        for h in range(2):
            hs = slice(h * GLA_DV, (h + 1) * GLA_DV)
            kv_t = _dot_tn(v_ref[rows, hs], ke_blk)
            st = st_ref[h]
            states = [None] * nch
            for c in (range(nch) if forward else range(nch - 1, -1, -1)):
                states[c] = st
                st = st * dec[c] + kv_t[:, c * LANES:(c + 1) * LANES]
            st_ref[h] = st
            entering.append(jnp.concatenate(states, axis=1).astype(BF16))

        for h in range(2):
            hs = slice(h * GLA_DV, (h + 1) * GLA_DV)
            qh = jnp.where((lane < GLA_DK) if h == 0 else (lane >= GLA_DK), qd, 0.0).astype(BF16)
            att = jnp.where(att_ok, _dot_nt(qh, ki), 0.0).astype(BF16)
            q_blk = jnp.where(own_block, jnp.tile(qh, (1, nch)), jnp.zeros(wide, BF16))
            o = _dot(att, v_ref[rows, hs]) + _dot_nt(q_blk, entering[h])
            if not final:
                ob_ref[rows, hs] = o
            else:
                o = o + ob_ref[rows, hs]
                o = o * lax.rsqrt(jnp.mean(o * o, axis=-1, keepdims=True) + EPS) * gn_ref[:, hs]
                o_ref[rows, hs] = (o * sr_ref[rows, hs].astype(F32)).astype(BF16)

    def main_start(j):
        return pl.multiple_of(GLA_LEAD + j * GLA_ROWS, CHUNK)

    st_ref[...] = jnp.zeros(st_ref.shape, F32)

    def bwd(i, carry):
        block(main_start(n_main - 1 - i), GLA_ROWS, False, False)
        return carry

    lax.fori_loop(0, n_main, bwd, 0, unroll=GLA_UNROLL)
    block(0, GLA_LEAD, False, False)

    st_ref[...] = jnp.zeros(st_ref.shape, F32)
    block(0, GLA_LEAD, True, True)

    def fwd(i, carry):
        block(main_start(i), GLA_ROWS, True, True)
        return carry

    lax.fori_loop(0, n_main, fwd, 0, unroll=GLA_UNROLL)


def _gla(qa, ka, va, gf, gb, sr, gn, lp):
    n = qa.shape[0]
    nb = n // lp
    pairs = GLA_HEADS // 2

    def blk(c):
        return pl.BlockSpec((lp, c), lambda b, p: (b, p))

    return pl.pallas_call(
        _gla_body,
        out_shape=jax.ShapeDtypeStruct((n, GLA_VAL), BF16),
        grid=(nb, pairs),
        in_specs=[blk(LANES), blk(LANES), blk(2 * GLA_DV), blk(LANES), blk(LANES),
                  blk(2 * GLA_DV), pl.BlockSpec((1, 2 * GLA_DV), lambda b, p: (0, p))],
        out_specs=blk(2 * GLA_DV),
        scratch_shapes=[pltpu.VMEM((lp, 2 * GLA_DV), F32),
                        pltpu.VMEM((2, GLA_DV, LANES), F32)],
        compiler_params=pltpu.CompilerParams(
            dimension_semantics=("parallel", "parallel"), vmem_limit_bytes=VMEM_LIMIT),
        name="gla",
    )(qa, ka, va, gf, gb, sr, gn)


def _attn_body(q_ref, k_ref, vt_ref, o_ref, *s_ref, padf, online):
    tq = q_ref.shape[0]
    lp = k_ref.shape[0]
    group = ATT_HEADS // ATT_KV_HEADS
    cols = group * tq
    n_main = (lp - LANES) // TK_ATT
    low = lax.broadcasted_iota(jnp.int32, (tq, LANES), 1) < HEAD_DIM
    if online:
        (s_ref,) = s_ref

    for kvh in range(ATT_KV_HEADS):
        hs = slice(kvh * LANES, (kvh + 1) * LANES)
        parts = []
        for s in range(group // 2):
            slab = kvh * (group // 2) + s
            q = q_ref[:, slab * LANES:(slab + 1) * LANES]
            zero = jnp.zeros_like(q)
            parts += [jnp.where(low, q, zero), jnp.where(low, zero, q)]
        qs = jnp.concatenate(parts, axis=0)

        def scores(start, size, qs=qs, hs=hs):
            return _dot_nt(k_ref[pl.ds(start, size), hs], qs)

        def absorb(s, start, size, m, acc, hs=hs):
            if not online:
                return m, acc + _dot(vt_ref[hs, pl.ds(start, size)], jnp.exp2(s).astype(BF16))
            m_new = jnp.maximum(m, jnp.max(s, axis=0, keepdims=True))
            p = jnp.exp2(s - m_new).astype(BF16)
            pv = _dot(vt_ref[hs, pl.ds(start, size)], p)
            return m_new, acc * jnp.exp2(m - m_new) + pv

        def main_start(j):
            return pl.multiple_of(LANES + j * TK_ATT, LANES)

        def masked(s):
            return jnp.where(lax.broadcasted_iota(jnp.int32, s.shape, 0) >= padf, s, -1e30)

        acc = jnp.zeros((LANES, cols), F32)
        if online:
            m, acc = absorb(masked(scores(0, LANES)), 0, LANES,
                            jnp.full((1, cols), -jnp.inf, F32), acc)
            assert n_main % 2 == 0
            s_ref[0] = scores(main_start(0), TK_ATT)

            def pair(i, carry, last):
                m, acc = carry
                s_ref[1] = scores(main_start(2 * i + 1), TK_ATT)
                m, acc = absorb(s_ref[0], main_start(2 * i), TK_ATT, m, acc)
                if not last:
                    s_ref[0] = scores(main_start(2 * i + 2), TK_ATT)
                return absorb(s_ref[1], main_start(2 * i + 1), TK_ATT, m, acc)

            m, acc = lax.fori_loop(0, n_main // 2 - 1, lambda i, c: pair(i, c, False), (m, acc))
            _, acc = pair(n_main // 2 - 1, (m, acc), True)
        else:
            first = LANES + TK_ATT
            _, acc = absorb(masked(scores(0, first)), 0, first, None, acc)

            def direct(j, acc):
                _, acc = absorb(scores(main_start(j), TK_ATT), main_start(j), TK_ATT, None, acc)
                return acc

            acc = lax.fori_loop(1, n_main, direct, acc, unroll=n_main - 1)
        out_t = acc[:HEAD_DIM] / acc[HEAD_DIM:]
        for s in range(group // 2):
            slab = kvh * (group // 2) + s
            pair_t = jnp.concatenate([out_t[:, (2 * s) * tq:(2 * s + 1) * tq],
                                      out_t[:, (2 * s + 1) * tq:(2 * s + 2) * tq]], axis=0)
            o_ref[:, slab * LANES:(slab + 1) * LANES] = pair_t.T.astype(BF16)


def _attn(qb, kb, vt, lp, padf, online):
    n = qb.shape[0]
    nb = n // lp
    assert lp % TQ_ATT == 0 and (lp - LANES) % TK_ATT == 0
    nq = lp // TQ_ATT
    q_spec = pl.BlockSpec((TQ_ATT, ATT_Q), lambda b, i: (b * nq + i, 0))
    return pl.pallas_call(
        functools.partial(_attn_body, padf=padf, online=online),
        out_shape=jax.ShapeDtypeStruct((n, ATT_Q), BF16),
        grid=(nb, nq),
        in_specs=[q_spec, pl.BlockSpec((lp, kb.shape[1]), lambda b, i: (b, 0)),
                  pl.BlockSpec((vt.shape[0], lp), lambda b, i: (0, b))],
        out_specs=q_spec,
        scratch_shapes=([pltpu.VMEM((2, TK_ATT, ATT_HEADS // ATT_KV_HEADS * TQ_ATT), F32)]
                        if online else []),
        compiler_params=pltpu.CompilerParams(
            dimension_semantics=("parallel", "arbitrary"), vmem_limit_bytes=VMEM_LIMIT),
        name="attn",
    )(qb, kb, vt)


def _merge_body(h_ref, g_ref, a_ref, b_ref, wg_ref, bm_ref, wpa_ref, wpb_ref, wo_ref, o_ref,
                *, lp, padf):
    tm, d = h_ref.shape
    block_pos = (pl.program_id(0) % (lp // tm)) * tm
    for r in range(tm // TS_MERGE):
        rows = slice(r * TS_MERGE, (r + 1) * TS_MERGE)
        x = h_ref[rows, :]
        pa = _dot(a_ref[rows, :], wpa_ref[...])
        pb = _dot(b_ref[rows, :], wpb_ref[...])
        zb = _rmsnorm_rows(x, g_ref[...]).astype(BF16)
        gate_a = jax.nn.sigmoid(_dot(zb, wg_ref[:, :d]) + bm_ref[:, :d])
        gate_b = jax.nn.sigmoid(_dot(zb, wg_ref[:, d:]) + bm_ref[:, d:])
        out = _dot((gate_a * pa + gate_b * pb).astype(BF16), wo_ref[...])
        pos = block_pos + r * TS_MERGE + lax.broadcasted_iota(jnp.int32, (TS_MERGE, 1), 0)
        o_ref[rows, :] = x + jnp.where(pos >= padf, out, 0.0)


def _merge(hp, gain, a, b, wg, bm, wpa, wpb, wo, lp, padf, layer):
    n, d = hp.shape
    lead = (layer,)
    assert lp % TM_TOK == 0 and TM_TOK % TS_MERGE == 0

    def row(c):
        return pl.BlockSpec((TM_TOK, c), lambda i: (i, 0))

    return pl.pallas_call(
        functools.partial(_merge_body, lp=lp, padf=padf),
        out_shape=jax.ShapeDtypeStruct((n, d), F32),
        grid=(n // TM_TOK,),
        in_specs=[row(d), _resident((1, d)), row(GLA_VAL), row(ATT_Q), _resident(wg.shape, lead),
                  _resident(bm.shape, lead), _resident(wpa.shape, lead),
                  _resident(wpb.shape, lead), _resident(wo.shape, lead)],
        out_specs=row(d),
        compiler_params=pltpu.CompilerParams(
            dimension_semantics=("parallel",), vmem_limit_bytes=VMEM_LIMIT),
        name="merge",
    )(hp, gain, a, b, wg, bm, wpa, wpb, wo)


def _merge_ffn_body(h_ref, g_ref, a_ref, b_ref, wg_ref, bm_ref, wpa_ref, wpb_ref, wo_ref,
                    g2_ref, fg_ref, fu_ref, fd_ref, o_ref, *, lp, padf):
    tm, d = h_ref.shape
    block_pos = (pl.program_id(0) % (lp // tm)) * tm
    for r in range(tm // TS_MERGE):
        rows = slice(r * TS_MERGE, (r + 1) * TS_MERGE)
        x = h_ref[rows, :]
        pa = _dot(a_ref[rows, :], wpa_ref[...])
        pb = _dot(b_ref[rows, :], wpb_ref[...])
        zb = _rmsnorm_rows(x, g_ref[...]).astype(BF16)
        gate_a = jax.nn.sigmoid(_dot(zb, wg_ref[:, :d]) + bm_ref[:, :d])
        gate_b = jax.nn.sigmoid(_dot(zb, wg_ref[:, d:]) + bm_ref[:, d:])
        out = _dot((gate_a * pa + gate_b * pb).astype(BF16), wo_ref[...])
        pos = block_pos + r * TS_MERGE + lax.broadcasted_iota(jnp.int32, (TS_MERGE, 1), 0)
        y = x + jnp.where(pos >= padf, out, 0.0)
        yb = _rmsnorm_rows(y, g2_ref[...]).astype(BF16)
        hidden = (_silu(_dot(yb, fg_ref[...])) * _dot(yb, fu_ref[...])).astype(BF16)
        o_ref[rows, :] = y + 0.5 * _dot(hidden, fd_ref[...])


def _merge_ffn(hp, gain, a, b, wg, bm, wpa, wpb, wo, gain2, fg, fu, fd, lp, padf, layer):
    n, d = hp.shape
    lead = (layer,)
    which = (layer, 1)
    assert lp % TM_MF == 0 and TM_MF % TS_MERGE == 0

    def row(c):
        return pl.BlockSpec((TM_MF, c), lambda i: (i, 0))

    return pl.pallas_call(
        functools.partial(_merge_ffn_body, lp=lp, padf=padf),
        out_shape=jax.ShapeDtypeStruct((n, d), F32),
        grid=(n // TM_MF,),
        in_specs=[row(d), _resident((1, d)), row(GLA_VAL), row(ATT_Q), _resident(wg.shape, lead),
                  _resident(bm.shape, lead), _resident(wpa.shape, lead),
                  _resident(wpb.shape, lead), _resident(wo.shape, lead), _resident((1, d)),
                  _resident(fg.shape, which), _resident(fu.shape, which),
                  _resident(fd.shape, which)],
        out_specs=row(d),
        compiler_params=pltpu.CompilerParams(
            dimension_semantics=("parallel",), vmem_limit_bytes=VMEM_LIMIT),
        name="merge_ffn",
    )(hp, gain, a, b, wg, bm, wpa, wpb, wo, gain2, fg, fu, fd)


def _final_body(h_ref, g_ref, o_ref):
    o_ref[...] = _rmsnorm_rows(h_ref[...], g_ref[...])


def _final(hp, gain, nb, n_tok, lp, skip):
    d = hp.shape[1]
    assert n_tok % TM_FIN == 0
    per_out = n_tok // TM_FIN
    return pl.pallas_call(
        _final_body,
        out_shape=jax.ShapeDtypeStruct((nb * n_tok, d), F32),
        grid=(nb, per_out),
        in_specs=[pl.BlockSpec((pl.Element(TM_FIN), pl.Element(d)),
                               lambda b, i: (pl.multiple_of(b * lp + skip + i * TM_FIN, LANES), 0)),
                  _resident((1, d))],
        out_specs=pl.BlockSpec((TM_FIN, d), lambda b, i: (b * per_out + i, 0)),
        compiler_params=pltpu.CompilerParams(
            dimension_semantics=("parallel", "parallel"), vmem_limit_bytes=VMEM_LIMIT),
        name="final_norm",
    )(hp, gain)


def _rope_tables(n_tok, lp, padf):
    quarter = HEAD_DIM // 4
    pos = jnp.arange(n_tok)
    inv = ROPE_THETA ** (-jnp.arange(0, HEAD_DIM // 2, 2, dtype=F32) / (HEAD_DIM // 2))
    ang_r = (pos // GRID_W).astype(F32)[:, None] * inv
    ang_c = (pos % GRID_W).astype(F32)[:, None] * inv
    cos = jnp.concatenate([jnp.cos(ang_r)] * 2 + [jnp.cos(ang_c)] * 2, axis=-1)
    sin = jnp.concatenate([-jnp.sin(ang_r), jnp.sin(ang_r), -jnp.sin(ang_c), jnp.sin(ang_c)],
                          axis=-1)
    lead = padf + N_META
    cos = jnp.concatenate([jnp.ones((lead, HEAD_DIM), F32), cos], axis=0)
    sin = jnp.concatenate([jnp.zeros((lead, HEAD_DIM), F32), sin], axis=0)
    assert cos.shape == (lp, 4 * quarter)
    return jnp.tile(cos, (1, 2)), jnp.tile(sin, (1, 2))


def _prep_w_in(w_in):
    sizes = (GLA_KEY, GLA_KEY, GLA_VAL, GLA_VAL, GLA_RANK, GLA_RANK, ATT_Q, ATT_KV, ATT_KV)
    offs = np.concatenate([[0], np.cumsum(sizes)])
    qa, ka, va, ra, lrf, lrb, qb, kb, vb = (w_in[..., offs[i]:offs[i + 1]] for i in range(9))

    def spread(w, twice):
        heads = [w[..., h * HEAD_DIM:(h + 1) * HEAD_DIM] for h in range(ATT_KV_HEADS)]
        return jnp.concatenate([x for h in heads for x in (h, h if twice else jnp.zeros_like(h))],
                               axis=-1)

    lr = jnp.concatenate([lrf, lrb, jnp.zeros(lrf.shape[:-1] + (LANES - 2 * GLA_RANK,), F32)], -1)
    w_all = jnp.concatenate([qa, ka, va, ra, qb, spread(kb, True), lr], axis=-1).astype(BF16)
    return w_all, jnp.swapaxes(spread(vb, False), -1, -2).astype(BF16)


def _prep_gate(w2, b2):
    depth = w2.shape[0]
    w = jnp.zeros((depth, LANES, 2 * GLA_KEY), F32)
    w = w.at[:, :GLA_RANK, :GLA_KEY].set(w2[:, 0]).at[:, GLA_RANK:2 * GLA_RANK, GLA_KEY:].set(w2[:, 1])
    return w.astype(BF16), b2.reshape(depth, 1, 2 * GLA_KEY)


def kernel(x, meta_tokens, norm_gains, ffn_w_gate, ffn_w_up, ffn_w_down, w_in, gla_w2, gla_b2,
           gla_gn, q_norm, k_norm, w_pa, w_pb, b_merge, w_out, final_norm):
    nb, n_tok, d = x.shape
    depth = w_in.shape[0]
    lp = -(-(n_tok + N_META) // LANES) * LANES
    padf = lp - n_tok - N_META
    assert (padf + N_META) % CHUNK == 0 and n_tok % GRID_W == 0

    meta = jnp.broadcast_to(meta_tokens[None].astype(x.dtype), (nb, N_META, d))
    hp = jnp.concatenate([jnp.zeros((nb, padf, d), x.dtype), meta, x], axis=1).reshape(nb * lp, d)
    cos_t, sin_t = _rope_tables(n_tok, lp, padf)

    g_split = w_in.shape[2] - 2 * d
    ffn_w = tuple(w.astype(BF16) for w in (ffn_w_gate, ffn_w_up, ffn_w_down))
    w_all, wvt = _prep_w_in(w_in[:, :, :g_split])
    w2, b2 = _prep_gate(gla_w2, gla_b2)
    merge_w = (w_in[:, :, g_split:].astype(BF16), b_merge.reshape(depth, 1, 2 * d),
               w_pa.astype(BF16), w_pb.astype(BF16), w_out.astype(BF16))
    for l in range(depth):
        gains = norm_gains[l].reshape(3, 1, d)
        hp = _ffn(hp, gains[0], *ffn_w, (l, 0))
        qa, ka, va, sr, gf, gb, qb, kb, vt = _proj(
            hp, gains[1], w_all, wvt, w2, b2,
            jnp.tile(q_norm[l], 2).reshape(1, LANES), jnp.tile(k_norm[l], 2).reshape(1, LANES),
            cos_t, sin_t, lp, l)
        a = _gla(qa, ka, va, gf, gb, sr, gla_gn[l].reshape(1, GLA_VAL), lp)
        small = (jnp.max(jnp.abs(q_norm[l])) * jnp.max(jnp.abs(k_norm[l])) * ATT_SCORE_BOUND
                 <= ATT_SAFE_SCORE)
        b = lax.cond(small, functools.partial(_attn, lp=lp, padf=padf, online=False),
                     functools.partial(_attn, lp=lp, padf=padf, online=True), qb, kb, vt)
        hp = _merge_ffn(hp, gains[1], a, b, *merge_w, gains[2], *ffn_w, lp, padf, l)
    out = _final(hp, final_norm.reshape(1, d), nb, n_tok, lp, padf + N_META)
    return out.reshape(nb, n_tok, d)
```
